```python
import math
import jax, jax.numpy as jnp
from jax import lax
import numpy as np

D_MODEL = 4096
BATCH = 32
SEQ = 256
DEPTH = 1
DEC_BATCH = 4
DEC_SEQ = 4096
PAST_LEN = 512

GRID_W = 64
N_HEADS = 32
QK_NOPE = 128
QK_ROPE = 64
QK_HEAD = QK_NOPE + QK_ROPE
V_HEAD = 128
Q_LORA = 1024
KV_LORA = 512
S5_WIDTH = 2048
S5_GROUP = 16
S5_GROUPS = S5_WIDTH // S5_GROUP
S5_STATE = 64
D_FF = 4 * D_MODEL
ROPE_THETA = 10000.0
EPS = 1e-6
Q_BLOCK = 128
N_MOD = 6
IN_COLS = Q_LORA + KV_LORA + QK_ROPE + S5_WIDTH + 2 * D_MODEL
IN_SPLITS = (Q_LORA, Q_LORA + KV_LORA, Q_LORA + KV_LORA + QK_ROPE,
             Q_LORA + KV_LORA + QK_ROPE + S5_WIDTH,
             Q_LORA + KV_LORA + QK_ROPE + S5_WIDTH + D_MODEL)

kernel_name = 'hybrid_mla_s5_diffusion_step'


def rms_norm(x, g):
    xf = x.astype(jnp.float32)
    y = xf * lax.rsqrt(jnp.mean(jnp.square(xf), axis=-1, keepdims=True) + EPS)
    return (y * g.astype(jnp.float32)).astype(x.dtype)


def axial_rope(n_tokens):
    rows = n_tokens // GRID_W
    row = jnp.repeat(jnp.arange(rows, dtype=jnp.float32), GRID_W)
    col = jnp.tile(jnp.arange(GRID_W, dtype=jnp.float32), rows)
    n_freq = QK_ROPE // 4
    inv_freq = ROPE_THETA ** (-jnp.arange(n_freq, dtype=jnp.float32) / n_freq)
    ang = jnp.concatenate([row[:, None] * inv_freq, col[:, None] * inv_freq], axis=-1)
    ang = jnp.concatenate([ang, ang], axis=-1)
    return jnp.cos(ang), jnp.sin(ang)


def apply_rope(x, cos, sin):
    xf = x.astype(jnp.float32)
    x1, x2 = jnp.split(xf, 2, axis=-1)
    rot = jnp.concatenate([-x2, x1], axis=-1)
    return (xf * cos + rot * sin).astype(x.dtype)


def adaln(cond, w_mod, b_mod):
    m = jax.nn.silu(cond) @ w_mod + b_mod
    return m.reshape(cond.shape[0], 1, N_MOD, D_MODEL)


def modulated_norm(x, g, shift, scale):
    return rms_norm(x, g) * (1 + scale) + shift


def block_attention(q, k, v):
    b, l, h, dqk = q.shape
    nb = l // Q_BLOCK
    qb = q.reshape(b, nb, Q_BLOCK, h, dqk).transpose(1, 0, 2, 3, 4)
    scale = dqk ** -0.5

    def one_block(q_blk):
        s = jnp.einsum('bqhd,bkhd->bhqk', q_blk, k).astype(jnp.float32) * scale
        p = jax.nn.softmax(s, axis=-1).astype(v.dtype)
        return jnp.einsum('bhqk,bkhd->bqhd', p, v)

    o = lax.map(one_block, qb)
    return o.transpose(1, 0, 2, 3, 4).reshape(b, l, h * V_HEAD)


def mla_queries(q_lat, lp):
    q = rms_norm(q_lat, lp['g_q_lat']) @ lp['w_uq']
    return q.reshape(q.shape[0], q.shape[1], N_HEADS, QK_HEAD)


def decompress_kv(ckv_n, k_rope, lp):
    b, lk, _ = ckv_n.shape
    kv = (ckv_n @ lp['w_ukv']).reshape(b, lk, N_HEADS, QK_NOPE + V_HEAD)
    k_nope, v = kv[..., :QK_NOPE], kv[..., QK_NOPE:]
    k_r = jnp.broadcast_to(k_rope[:, :, None, :], (b, lk, N_HEADS, QK_ROPE))
    return jnp.concatenate([k_nope, k_r], axis=-1), v


def diag_scan(a_bar, bu, h0, reverse):
    if reverse:
        bu = jnp.flip(bu, axis=1)
    bu = bu.at[:, 0].add(a_bar * h0)
    a = jnp.broadcast_to(a_bar, (1,) + bu.shape[1:])

    def combine(e1, e2):
        a1, b1 = e1
        a2, b2 = e2
        return a1 * a2, a2 * b1 + b2

    _, h = lax.associative_scan(combine, (a, bu), axis=1)
    h_last = h[:, -1]
    if reverse:
        h = jnp.flip(h, axis=1)
    return h, h_last


def s5_branch(u, h0, lp):
    b, l, _ = u.shape
    f32 = jnp.float32
    uf = u.astype(f32)
    uc = uf.reshape(b, l, S5_GROUPS, S5_GROUP).astype(jnp.complex64)
    y = uf * lp['s5_d'].astype(f32)
    finals = []
    for d, reverse in ((0, False), (1, True)):
        lam = lax.complex(lp['s5_lam_re'][d].astype(f32), lp['s5_lam_im'][d].astype(f32))
        dt = jnp.exp(lp['s5_log_dt'][d].astype(f32))[:, None]
        a_bar = jnp.exp(lam * dt)
        b_mat = lax.complex(lp['s5_b_re'][d].astype(f32), lp['s5_b_im'][d].astype(f32))
        b_bar = ((a_bar - 1.0) / lam)[..., None] * b_mat
        c_mat = lax.complex(lp['s5_c_re'][d].astype(f32), lp['s5_c_im'][d].astype(f32))
        bu = jnp.einsum('blgj,gpj->blgp', uc, b_bar)
        h, h_last = diag_scan(a_bar, bu, h0[:, d], reverse)
        y = y + jnp.einsum('blgp,gjp->blgj', h, c_mat).real.reshape(b, l, S5_WIDTH)
        finals.append(h_last)
    z = jax.nn.gelu(y).astype(u.dtype) @ lp['w_glu']
    za, zb = jnp.split(z, 2, axis=-1)
    return za * jax.nn.sigmoid(zb), jnp.stack(finals, axis=1)


def merge_branches(o_a, o_b, g_a, g_b, lp):
    m = jax.nn.sigmoid(g_a) * o_a + jax.nn.sigmoid(g_b) * o_b
    return m @ lp['w_out']


def context_mixer(h, lp):
    b = h.shape[0]
    q_lat, ckv, k_rope, u, g_a, g_b = jnp.split(h @ lp['w_in'], IN_SPLITS, axis=-1)
    q = mla_queries(q_lat, lp)
    ckv_n = rms_norm(ckv, lp['g_kv_lat'])
    k, v = decompress_kv(ckv_n, k_rope, lp)
    o_a = block_attention(q, k, v)
    h0 = jnp.zeros((b, 2, S5_GROUPS, S5_STATE), jnp.complex64)
    o_b, s5_last = s5_branch(u, h0, lp)
    return merge_branches(o_a, o_b, g_a, g_b, lp), ckv_n, k_rope, s5_last


def latent_mixer(h, ctx_ckv, ctx_krope, ctx_s5, cos, sin, lp):
    q_lat, ckv, k_rope, u, g_a, g_b = jnp.split(h @ lp['w_in'], IN_SPLITS, axis=-1)
    q = mla_queries(q_lat, lp)
    q = jnp.concatenate([q[..., :QK_NOPE], apply_rope(q[..., QK_NOPE:], cos[:, None], sin[:, None])], axis=-1)
    k_rope = apply_rope(k_rope, cos, sin)
    ckv_all = jnp.concatenate([ctx_ckv.astype(h.dtype), rms_norm(ckv, lp['g_kv_lat'])], axis=1)
    krope_all = jnp.concatenate([ctx_krope.astype(h.dtype), k_rope], axis=1)
    k, v = decompress_kv(ckv_all, krope_all, lp)
    o_a = block_attention(q, k, v)
    o_b, _ = s5_branch(u, ctx_s5, lp)
    return merge_branches(o_a, o_b, g_a, g_b, lp)


def mixer_input(x, mod, lp):
    return modulated_norm(x, lp['g_pre_mix'], mod[:, :, 0], mod[:, :, 1])


def add_mixer(x, out, mod, lp):
    return x + mod[:, :, 2] * rms_norm(out, lp['g_post_mix'])


def add_ffn(x, mod, lp):
    h = modulated_norm(x, lp['g_pre_mlp'], mod[:, :, 3], mod[:, :, 4])
    a = jnp.square(jax.nn.relu(h @ lp['w_ff1'])) @ lp['w_ff2']
    return x + mod[:, :, 5] * rms_norm(a, lp['g_post_mlp'])


def setup_inputs(seed: int = 0) -> dict:
    key = jax.random.key(seed)
    ks = jax.random.split(key, 32)
    f32 = jnp.float32

    def nrm(k, shape, scale=1.0):
        return jax.random.normal(k, shape, f32) * scale

    def gain(k, shape):
        return 1.0 + 0.05 * jax.random.normal(k, shape, f32)

    G, P, J = S5_GROUPS, S5_STATE, S5_GROUP
    n_idx = jnp.arange(P, dtype=f32)
    return {
        'x_prompt': nrm(ks[0], (BATCH, SEQ, D_MODEL)),
        'x_sample': nrm(ks[1], (DEC_BATCH, DEC_SEQ, D_MODEL)),
        'c': nrm(ks[2], (DEC_BATCH, D_MODEL)),
        'cache_ckv': nrm(ks[3], (DEC_BATCH, DEPTH, PAST_LEN, KV_LORA)),
        'cache_krope': nrm(ks[4], (DEC_BATCH, DEPTH, PAST_LEN, QK_ROPE)),
        'state_s5': nrm(ks[5], (DEC_BATCH, DEPTH, 2, 2, G, P), 0.1),
        'c_ctx': nrm(ks[6], (D_MODEL,)),
        'w_mod': nrm(ks[7], (DEPTH, D_MODEL, N_MOD * D_MODEL), 0.5 * D_MODEL ** -0.5),
        'b_mod': nrm(ks[8], (DEPTH, N_MOD * D_MODEL), 0.01),
        'g_pre_mix': gain(ks[9], (DEPTH, D_MODEL)),
        'w_in': nrm(ks[10], (DEPTH, D_MODEL, IN_COLS), D_MODEL ** -0.5),
        'g_q_lat': gain(ks[11], (DEPTH, Q_LORA)),
        'g_kv_lat': gain(ks[12], (DEPTH, KV_LORA)),
        'w_uq': nrm(ks[13], (DEPTH, Q_LORA, N_HEADS * QK_HEAD), Q_LORA ** -0.5),
        'w_ukv': nrm(ks[14], (DEPTH, KV_LORA, N_HEADS * (QK_NOPE + V_HEAD)), KV_LORA ** -0.5),
        's5_lam_re': -0.5 + 0.02 * nrm(ks[15], (DEPTH, 2, G, P)),
        's5_lam_im': math.pi * n_idx + 0.01 * nrm(ks[16], (DEPTH, 2, G, P)),
        's5_log_dt': jax.random.uniform(ks[17], (DEPTH, 2, G), f32, math.log(1e-3), math.log(1e-1)),
        's5_b_re': nrm(ks[18], (DEPTH, 2, G, P, J), (2 * J) ** -0.5),
        's5_b_im': nrm(ks[19], (DEPTH, 2, G, P, J), (2 * J) ** -0.5),
        's5_c_re': nrm(ks[20], (DEPTH, 2, G, J, P), (2 * P) ** -0.5),
        's5_c_im': nrm(ks[21], (DEPTH, 2, G, J, P), (2 * P) ** -0.5),
        's5_d': nrm(ks[22], (DEPTH, S5_WIDTH)),
        'w_glu': nrm(ks[23], (DEPTH, S5_WIDTH, 2 * D_MODEL), S5_WIDTH ** -0.5),
        'w_out': nrm(ks[24], (DEPTH, D_MODEL, D_MODEL), D_MODEL ** -0.5),
        'g_post_mix': gain(ks[25], (DEPTH, D_MODEL)),
        'g_pre_mlp': gain(ks[26], (DEPTH, D_MODEL)),
        'w_ff1': nrm(ks[27], (DEPTH, D_MODEL, D_FF), D_MODEL ** -0.5),
        'w_ff2': nrm(ks[28], (DEPTH, D_FF, D_MODEL), D_FF ** -0.5),
        'g_post_mlp': gain(ks[29], (DEPTH, D_MODEL)),
    }


def reference(x_prompt, x_sample, c, cache_ckv, cache_krope, state_s5, c_ctx, w_mod, b_mod,
              g_pre_mix, w_in, g_q_lat, g_kv_lat, w_uq, w_ukv, s5_lam_re, s5_lam_im, s5_log_dt,
              s5_b_re, s5_b_im, s5_c_re, s5_c_im, s5_d, w_glu, w_out, g_post_mix, g_pre_mlp,
              w_ff1, w_ff2, g_post_mlp):
    f32 = jnp.float32
    cos, sin = axial_rope(x_sample.shape[1])
    xp, xs = x_prompt, x_sample
    new_ckv, new_krope, new_s5 = [], [], []
    for l in range(DEPTH):
        lp = {
            'g_pre_mix': g_pre_mix[l], 'w_in': w_in[l], 'g_q_lat': g_q_lat[l], 'g_kv_lat': g_kv_lat[l],
            'w_uq': w_uq[l], 'w_ukv': w_ukv[l], 's5_lam_re': s5_lam_re[l], 's5_lam_im': s5_lam_im[l],
            's5_log_dt': s5_log_dt[l], 's5_b_re': s5_b_re[l], 's5_b_im': s5_b_im[l],
            's5_c_re': s5_c_re[l], 's5_c_im': s5_c_im[l], 's5_d': s5_d[l], 'w_glu': w_glu[l],
            'w_out': w_out[l], 'g_post_mix': g_post_mix[l], 'g_pre_mlp': g_pre_mlp[l],
            'w_ff1': w_ff1[l], 'w_ff2': w_ff2[l], 'g_post_mlp': g_post_mlp[l],
        }
        mod_p = adaln(c_ctx[None, :], w_mod[l], b_mod[l])
        out_p, ckv_n, k_rope_p, s5_last = context_mixer(mixer_input(xp, mod_p, lp), lp)
        xp = add_ffn(add_mixer(xp, out_p, mod_p, lp), mod_p, lp)
        new_ckv.append(ckv_n)
        new_krope.append(k_rope_p)
        new_s5.append(jnp.stack([s5_last.real, s5_last.imag], axis=2))
        mod_s = adaln(c, w_mod[l], b_mod[l])
        ctx_s5 = lax.complex(state_s5[:, l, :, 0].astype(f32), state_s5[:, l, :, 1].astype(f32))
        out_s = latent_mixer(mixer_input(xs, mod_s, lp), cache_ckv[:, l], cache_krope[:, l], ctx_s5, cos, sin, lp)
        xs = add_ffn(add_mixer(xs, out_s, mod_s, lp), mod_s, lp)
    new_cache_ckv = jnp.stack(new_ckv, axis=1)
    new_cache_krope = jnp.stack(new_krope, axis=1)
    new_state_s5 = jnp.stack(new_s5, axis=1).astype(x_prompt.dtype)
    return (xp, xs, new_cache_ckv, new_cache_krope, new_state_s5)
```

```python
import functools
import math

import jax
import jax.numpy as jnp
from jax import lax
from jax.experimental import pallas as pl
from jax.experimental.pallas import tpu as pltpu

N_HEADS = 32
QK_NOPE = 128
QK_ROPE = 64
V_HEAD = 128
GRID_W = 64
ROPE_THETA = 10000.0
EPS = 1e-6
S5_CHUNK = 32
ATTN_KV_CHUNK = 512
VMEM_LIMIT_BYTES = 48 * 1024 * 1024

F32 = jnp.float32
BF16 = jnp.bfloat16


def _params(*semantics):
    return pltpu.CompilerParams(dimension_semantics=semantics, vmem_limit_bytes=VMEM_LIMIT_BYTES)


def _tile(n, target, align=8):
    for d in range(min(n, target), 0, -1):
        if n % d == 0 and d % align == 0:
            return d
    return n


def _adaln_kernel(c_ref, w_ref, b_ref, o_ref):
    c = c_ref[...]
    s = (c * jax.nn.sigmoid(c)).astype(BF16)
    o_ref[...] = jnp.dot(s, w_ref[...].astype(BF16), preferred_element_type=F32) + b_ref[...]


def _adaln(cond, w_mod, b_mod):
    rows, d = cond.shape
    n = w_mod.shape[1]
    tn = _tile(n, 512, 128)
    return pl.pallas_call(
        _adaln_kernel,
        grid=(n // tn,),
        in_specs=[pl.BlockSpec((rows, d), lambda j: (0, 0)),
                  pl.BlockSpec((d, tn), lambda j: (0, j)),
                  pl.BlockSpec((1, tn), lambda j: (0, j))],
        out_specs=pl.BlockSpec((rows, tn), lambda j: (0, j)),
        out_shape=jax.ShapeDtypeStruct((rows, n), F32),
        compiler_params=_params("parallel"),
        name="adaln",
    )(cond, w_mod, b_mod)


def _rms(x, g):
    return x * lax.rsqrt(jnp.mean(x * x, axis=-1, keepdims=True) + EPS) * g


def _modnorm_kernel(x_ref, g_ref, mod_ref, o_ref, *, shift_idx, scale_idx):
    mod = mod_ref[0]
    y = _rms(x_ref[...], g_ref[...])
    o_ref[...] = (y * (1.0 + mod[scale_idx:scale_idx + 1]) + mod[shift_idx:shift_idx + 1]).astype(o_ref.dtype)


def _mod_spec(mod, n_rows, tm):
    rows_per_mod = n_rows // mod.shape[0]
    return pl.BlockSpec((1,) + mod.shape[1:], lambda i: ((i * tm) // rows_per_mod, 0, 0))


def _modnorm(x, g, mod, shift_idx, scale_idx):
    n, d = x.shape
    tm = _tile(n // mod.shape[0], 256)
    return pl.pallas_call(
        functools.partial(_modnorm_kernel, shift_idx=shift_idx, scale_idx=scale_idx),
        grid=(n // tm,),
        in_specs=[pl.BlockSpec((tm, d), lambda i: (i, 0)),
                  pl.BlockSpec((1, d), lambda i: (0, 0)),
                  _mod_spec(mod, n, tm)],
        out_specs=pl.BlockSpec((tm, d), lambda i: (i, 0)),
        out_shape=jax.ShapeDtypeStruct((n, d), BF16),
        compiler_params=_params("parallel"),
        name="modnorm",
    )(x, g, mod)


def _rmsnorm_kernel(x_ref, g_ref, o_ref):
    o_ref[...] = _rms(x_ref[...], g_ref[...]).astype(o_ref.dtype)


def _rmsnorm(x, g):
    n, d = x.shape
    tm = _tile(n, 1024)
    return pl.pallas_call(
        _rmsnorm_kernel,
        grid=(n // tm,),
        in_specs=[pl.BlockSpec((tm, d), lambda i: (i, 0)),
                  pl.BlockSpec((1, d), lambda i: (0, 0))],
        out_specs=pl.BlockSpec((tm, d), lambda i: (i, 0)),
        out_shape=jax.ShapeDtypeStruct((n, d), BF16),
        compiler_params=_params("parallel"),
        name="rmsnorm",
    )(x, g)


def _mm_kernel(a_ref, w_ref, o_ref, acc_ref, *, relu2):
    k = pl.program_id(2)

    @pl.when(k == 0)
    def _():
        acc_ref[...] = jnp.zeros_like(acc_ref)

    acc_ref[...] += jnp.dot(a_ref[...], w_ref[...], preferred_element_type=F32)

    @pl.when(k == pl.num_programs(2) - 1)
    def _():
        r = acc_ref[...]
        if relu2:
            r = jnp.square(jnp.maximum(r, 0.0))
        o_ref[...] = r.astype(o_ref.dtype)


def _mm(a, w, out_dtype, relu2=False):
    m, kdim = a.shape
    n = w.shape[1]
    tm, tn, tk = _tile(m, 1024), _tile(n, 1024, 128), _tile(kdim, 1024, 128)
    return pl.pallas_call(
        functools.partial(_mm_kernel, relu2=relu2),
        grid=(m // tm, n // tn, kdim // tk),
        in_specs=[pl.BlockSpec((tm, tk), lambda i, j, k: (i, k)),
                  pl.BlockSpec((tk, tn), lambda i, j, k: (k, j))],
        out_specs=pl.BlockSpec((tm, tn), lambda i, j, k: (i, j)),
        out_shape=jax.ShapeDtypeStruct((m, n), out_dtype),
        scratch_shapes=[pltpu.VMEM((tm, tn), F32)],
        compiler_params=_params("parallel", "parallel", "arbitrary"),
        name="matmul",
    )(a, w)


def _qproj_kernel(a_ref, w_ref, *rest, scale, use_rope):
    if use_rope:
        cos_ref, sin_ref, o_ref = rest
    else:
        (o_ref,) = rest
    nb, _, tl, _ = o_ref.shape
    acc = jnp.dot(a_ref[...], w_ref[0], preferred_element_type=F32)
    o_ref[:, 0, :, :QK_NOPE] = (acc[:, :QK_NOPE] * scale).reshape(nb, tl, QK_NOPE).astype(o_ref.dtype)
    rope = acc[:, QK_NOPE:QK_NOPE + QK_ROPE]
    if use_rope:
        rope = rope * cos_ref[...] + acc[:, QK_NOPE + QK_ROPE:QK_NOPE + 2 * QK_ROPE] * sin_ref[...]
    o_ref[:, 0, :, QK_NOPE:] = (rope * scale).reshape(nb, tl, QK_ROPE).astype(o_ref.dtype)


def _row_blocking(batch, length, target):
    if length >= target:
        return 1, _tile(length, target)
    return _tile(batch, max(1, target // length), 1), length


def _head_out_spec(nb, tl, length, width):
    per_batch = length // tl
    return pl.BlockSpec((nb, 1, tl, width), lambda i, h: (i // per_batch, h, i % per_batch, 0))


def _qproj(qn, w_heads, batch, length, rope_tables):
    n, kdim = qn.shape
    heads, _, cols = w_heads.shape
    use_rope = rope_tables is not None
    nb, tl = (1, _tile(length, 1024)) if use_rope else _row_blocking(batch, length, 1024)
    tm = nb * tl
    in_specs = [pl.BlockSpec((tm, kdim), lambda i, h: (i, 0)),
                pl.BlockSpec((1, kdim, cols), lambda i, h: (h, 0, 0))]
    args = [qn, w_heads]
    if use_rope:
        per_batch = length // tl
        spec = pl.BlockSpec((tl, QK_ROPE), lambda i, h: (i % per_batch, 0))
        in_specs += [spec, spec]
        args += list(rope_tables)
    width = QK_NOPE + QK_ROPE
    return pl.pallas_call(
        functools.partial(_qproj_kernel, scale=width ** -0.5, use_rope=use_rope),
        grid=(n // tm, heads),
        in_specs=in_specs,
        out_specs=_head_out_spec(nb, tl, length, width),
        out_shape=jax.ShapeDtypeStruct((batch, heads, length, width), BF16),
        compiler_params=_params("parallel", "parallel"),
        name="q_proj",
    )(*args)


def _kvprep_kernel(ck_ref, g_ref, *rest, kv_lora, use_rope):
    if use_rope:
        cos_ref, sin_ref, ckv_ref, kr_ref = rest
    else:
        ckv_ref, kr_ref = rest
    ckv_ref[...] = _rms(ck_ref[:, :kv_lora], g_ref[...])
    kr = ck_ref[:, kv_lora:kv_lora + QK_ROPE]
    if use_rope:
        kr = kr * cos_ref[...] + ck_ref[:, kv_lora + 128:kv_lora + 128 + QK_ROPE] * sin_ref[...]
    kr_ref[...] = kr


def _kvprep(ck, g_kv, length, rope_tables):
    n, cols = ck.shape
    kv_lora = cols - 256
    tm = _tile(length, 512)
    use_rope = rope_tables is not None
    in_specs = [pl.BlockSpec((tm, cols), lambda i: (i, 0)),
                pl.BlockSpec((1, kv_lora), lambda i: (0, 0))]
    args = [ck, g_kv]
    if use_rope:
        per_batch = length // tm
        spec = pl.BlockSpec((tm, QK_ROPE), lambda i: (i % per_batch, 0))
        in_specs += [spec, spec]
        args += list(rope_tables)
    return pl.pallas_call(
        functools.partial(_kvprep_kernel, kv_lora=kv_lora, use_rope=use_rope),
        grid=(n // tm,),
        in_specs=in_specs,
        out_specs=[pl.BlockSpec((tm, kv_lora), lambda i: (i, 0)),
                   pl.BlockSpec((tm, QK_ROPE), lambda i: (i, 0))],
        out_shape=[jax.ShapeDtypeStruct((n, kv_lora), F32),
                   jax.ShapeDtypeStruct((n, QK_ROPE), F32)],
        compiler_params=_params("parallel"),
        name="kv_prep",
    )(*args)


def _kvdec_kernel(c_ref, kr_ref, w_ref, k_ref, v_ref):
    nb, _, tl, _ = k_ref.shape
    acc = jnp.dot(c_ref[...].astype(BF16), w_ref[0], preferred_element_type=F32)
    k_ref[:, 0, :, :QK_NOPE] = acc[:, :QK_NOPE].reshape(nb, tl, QK_NOPE).astype(k_ref.dtype)
    k_ref[:, 0, :, QK_NOPE:] = kr_ref[...].reshape(nb, tl, QK_ROPE).astype(k_ref.dtype)
    v_ref[:, 0] = acc[:, QK_NOPE:].reshape(nb, tl, V_HEAD).astype(v_ref.dtype)


def _kvdec(ckv, krope, w_heads, batch, length):
    n, kdim = ckv.shape
    heads, _, cols = w_heads.shape
    nb, tl = _row_blocking(batch, length, 1536)
    tm = nb * tl
    return pl.pallas_call(
        _kvdec_kernel,
        grid=(n // tm, heads),
        in_specs=[pl.BlockSpec((tm, kdim), lambda i, h: (i, 0)),
                  pl.BlockSpec((tm, QK_ROPE), lambda i, h: (i, 0)),
                  pl.BlockSpec((1, kdim, cols), lambda i, h: (h, 0, 0))],
        out_specs=[_head_out_spec(nb, tl, length, QK_NOPE + QK_ROPE),
                   _head_out_spec(nb, tl, length, V_HEAD)],
        out_shape=[jax.ShapeDtypeStruct((batch, heads, length, QK_NOPE + QK_ROPE), BF16),
                   jax.ShapeDtypeStruct((batch, heads, length, V_HEAD), BF16)],
        compiler_params=_params("parallel", "parallel"),
        name="kv_decompress",
    )(ckv, krope, w_heads)


def _qk(q, k):
    return lax.dot_general(q, k, (((1,), (1,)), ((), ())), preferred_element_type=F32)


def _attn_kernel(q_ref, k_ref, v_ref, o_ref, m_ref, l_ref, acc_ref, *, tk, nk):
    q = q_ref[0, 0]
    s = _qk(q, k_ref[0, 0, :tk, :])
    m = jnp.max(s, axis=-1, keepdims=True)
    p = jnp.exp(s - m)
    m_ref[...] = m
    l_ref[...] = jnp.sum(p, axis=-1, keepdims=True)
    acc_ref[...] = jnp.dot(p.astype(BF16), v_ref[0, 0, :tk, :], preferred_element_type=F32)

    def body(j, carry):
        start = pl.multiple_of(j * tk, tk)
        s = _qk(q, k_ref[0, 0, pl.ds(start, tk), :])
        m_old = m_ref[...]
        m_new = jnp.maximum(m_old, jnp.max(s, axis=-1, keepdims=True))
        alpha = jnp.exp(m_old - m_new)
        p = jnp.exp(s - m_new)
        m_ref[...] = m_new
        l_ref[...] = alpha * l_ref[...] + jnp.sum(p, axis=-1, keepdims=True)
        acc_ref[...] = alpha * acc_ref[...] + jnp.dot(
            p.astype(BF16), v_ref[0, 0, pl.ds(start, tk), :], preferred_element_type=F32)
        return carry

    if nk > 1:
        lax.fori_loop(1, nk, body, 0)
    o_ref[0] = (acc_ref[...] / l_ref[...]).astype(o_ref.dtype)


def _attention(q, k, v):
    batch, heads, lq, dqk = q.shape
    lk, dv = k.shape[2], v.shape[3]
    tq = _tile(lq, 1024)
    tk = _tile(lk, ATTN_KV_CHUNK)
    return pl.pallas_call(
        functools.partial(_attn_kernel, tk=tk, nk=lk // tk),
        grid=(batch, heads, lq // tq),
        in_specs=[pl.BlockSpec((1, 1, tq, dqk), lambda b, h, i: (b, h, i, 0)),
                  pl.BlockSpec((1, 1, lk, dqk), lambda b, h, i: (b, h, 0, 0)),
                  pl.BlockSpec((1, 1, lk, dv), lambda b, h, i: (b, h, 0, 0))],
        out_specs=pl.BlockSpec((1, tq, dv), lambda b, h, i: (b, i, h)),
        out_shape=jax.ShapeDtypeStruct((batch, lq, heads * dv), F32),
        scratch_shapes=[pltpu.VMEM((tq, 1), F32), pltpu.VMEM((tq, 1), F32), pltpu.VMEM((tq, dv), F32)],
        compiler_params=_params("parallel", "parallel", "arbitrary"),
        name="attention",
    )(q, k, v)


def _gelu_tanh(x):
    return 0.5 * x * (1.0 + jnp.tanh(math.sqrt(2.0 / math.pi) * (x + 0.044715 * (x * x * x))))


def _s5_kernel(u_ref, w_ref, g_ref, e_ref, d_ref, ar_ref, ai_ref, h0_ref, y_ref, fin_ref, *, n_chunks):
    u = u_ref[0]
    rows = u.shape[0]
    batch = rows // n_chunks
    half = g_ref.shape[2] // 2
    g = jnp.dot(u, g_ref[0], preferred_element_type=F32)
    h0 = jnp.broadcast_to(h0_ref[0][:, None, :], (batch, n_chunks, 2 * half)).reshape(rows, 2 * half)
    chunk = lax.broadcasted_iota(jnp.int32, (rows, half), 0) % n_chunks
    ar = ar_ref[0]
    ai = ai_ref[0]

    def cmul(x, step, lo):
        swapped = pltpu.roll(x, half // 2, axis=1)
        return x * ar[step:step + 1, lo:lo + half] + swapped * ai[step:step + 1, lo:lo + half]

    def scan(gd, h0d, lo, backward):
        if backward:
            z = jnp.where(chunk == n_chunks - 1, h0d, pltpu.roll(gd, rows - 1, axis=0))
        else:
            z = jnp.where(chunk == 0, h0d, pltpu.roll(gd, 1, axis=0))
        step, dist = 0, 1
        while dist < n_chunks:
            if backward:
                moved = jnp.where(chunk < n_chunks - dist, pltpu.roll(z, rows - dist, axis=0), 0.0)
            else:
                moved = jnp.where(chunk >= dist, pltpu.roll(z, dist, axis=0), 0.0)
            z = z + cmul(moved, step, lo)
            step, dist = step + 1, dist * 2
        return z

    gf, gb = g[:, :half], g[:, half:]
    pf = scan(gf, h0[:, :half], 0, False)
    pb = scan(gb, h0[:, half:], half, True)
    p = jnp.concatenate([pf, pb], axis=1).astype(BF16)
    y = (jnp.dot(u, w_ref[0], preferred_element_type=F32)
         + jnp.dot(p, e_ref[0], preferred_element_type=F32)
         + u.astype(F32) * d_ref[0])
    y_ref[0] = _gelu_tanh(y).astype(y_ref.dtype)
    def pick(x, which):
        return jnp.sum(jnp.where(chunk == which, x, 0.0).reshape(batch, n_chunks, half), axis=1)

    ff = pick(cmul(pf, 0, 0) + gf, n_chunks - 1)
    fb = pick(cmul(pb, 0, half) + gb, 0)
    fin_ref[0] = jnp.concatenate([ff, fb], axis=1)


def _s5(u_rows, ops, h0, n_chunks):
    groups, rows, width = u_rows.shape
    batch = rows // n_chunks
    wt, gt, et, dvec, ar, ai = ops
    lanes = gt.shape[2]

    def per_group(*shape):
        return pl.BlockSpec((1,) + shape, lambda g: (g,) + (0,) * len(shape))

    return pl.pallas_call(
        functools.partial(_s5_kernel, n_chunks=n_chunks),
        grid=(groups,),
        in_specs=[per_group(rows, width), per_group(width, width), per_group(width, lanes),
                  per_group(lanes, width), per_group(1, width), per_group(*ar.shape[1:]),
                  per_group(*ai.shape[1:]), per_group(batch, lanes)],
        out_specs=[per_group(rows, width), per_group(batch, lanes)],
        out_shape=[jax.ShapeDtypeStruct((groups, rows, width), BF16),
                   jax.ShapeDtypeStruct((groups, batch, lanes), F32)],
        compiler_params=_params("parallel"),
        name="s5",
    )(u_rows, wt, gt, et, dvec, ar, ai, h0)


def _s5_operators(lam_re, lam_im, log_dt, b_re, b_im, c_re, c_im, d_skip, chunk, max_chunks):
    hi = lax.Precision.HIGHEST
    groups, p_dim, j_dim = b_re.shape[1:]
    t = chunk
    lam = lax.complex(lam_re, lam_im)
    z = lam * jnp.exp(log_dt)[..., None]
    a = jnp.exp(z)
    b_bar = ((a - 1.0) / lam)[..., None] * lax.complex(b_re, b_im)
    c_mat = lax.complex(c_re, c_im)
    k_idx = jnp.arange(t + 1, dtype=F32)
    apow = jnp.exp(z[None] * k_idx[:, None, None, None])

    cb = c_mat[..., None] * b_bar[:, :, None]
    ap = jnp.moveaxis(apow[:t], 1, 0)
    kern = (jnp.einsum('dtgp,dgopi->dtgoi', ap.real, cb.real, precision=hi)
            - jnp.einsum('dtgp,dgopi->dtgoi', ap.imag, cb.imag, precision=hi))
    i_idx = jnp.arange(t)
    lag = i_idx[None, :] - i_idx[:, None]
    w_f = jnp.where((lag >= 0)[:, :, None, None, None], kern[0][jnp.clip(lag, 0, t - 1)], 0.0)
    w_b = jnp.where((lag <= 0)[:, :, None, None, None], kern[1][jnp.clip(-lag, 0, t - 1)], 0.0)
    w_all = (w_f + w_b).transpose(2, 0, 4, 1, 3).reshape(groups, t * j_dim, t * j_dim)

    inj_f = apow[t - 1 - i_idx, 0][..., None] * b_bar[0][None]
    inj_b = apow[i_idx, 1][..., None] * b_bar[1][None]
    g_all = jnp.concatenate([inj_f.real, inj_f.imag, inj_b.real, inj_b.imag], axis=2)
    g_all = g_all.transpose(1, 0, 3, 2).reshape(groups, t * j_dim, 4 * p_dim)

    out_f = c_mat[0][None] * apow[i_idx + 1, 0][:, :, None, :]
    out_b = c_mat[1][None] * apow[t - i_idx, 1][:, :, None, :]
    e_all = jnp.concatenate([out_f.real, -out_f.imag, out_b.real, -out_b.imag], axis=3)
    e_all = e_all.transpose(1, 3, 0, 2).reshape(groups, 4 * p_dim, t * j_dim)

    d_row = jnp.tile(d_skip.reshape(groups, 1, j_dim), (1, t, 1)).reshape(groups, 1, t * j_dim)

    n_steps = max(1, (max_chunks - 1).bit_length())
    steps = (t * 2 ** jnp.arange(n_steps)).astype(F32)
    tp = jnp.exp(z[None] * steps[:, None, None, None])
    a_r = jnp.concatenate([tp[:, 0].real, tp[:, 0].real, tp[:, 1].real, tp[:, 1].real], axis=-1)
    a_i = jnp.concatenate([-tp[:, 0].imag, tp[:, 0].imag, -tp[:, 1].imag, tp[:, 1].imag], axis=-1)
    a_r = a_r.transpose(1, 0, 2)
    a_i = a_i.transpose(1, 0, 2)
    return w_all.astype(BF16), g_all.astype(BF16), e_all.astype(BF16), d_row, a_r, a_i


def _merge_kernel(ga_ref, gb_ref, oa_ref, za_ref, zb_ref, o_ref):
    o_b = za_ref[...] * jax.nn.sigmoid(zb_ref[...])
    o_ref[...] = (jax.nn.sigmoid(ga_ref[...]) * oa_ref[...]
                  + jax.nn.sigmoid(gb_ref[...]) * o_b).astype(o_ref.dtype)


def _merge(gates, o_a, z):
    n, d = o_a.shape
    tm = _tile(n, 128)
    left = pl.BlockSpec((tm, d), lambda i: (i, 0))
    right = pl.BlockSpec((tm, d), lambda i: (i, 1))
    return pl.pallas_call(
        _merge_kernel,
        grid=(n // tm,),
        in_specs=[left, right, left, left, right],
        out_specs=left,
        out_shape=jax.ShapeDtypeStruct((n, d), BF16),
        compiler_params=_params("parallel"),
        name="merge",
    )(gates, gates, o_a, z, z)


def _postmix_kernel(x_ref, y_ref, gpost_ref, gpre_ref, mod_ref, x1_ref, h_ref):
    mod = mod_ref[0]
    x1 = x_ref[...] + mod[2:3] * _rms(y_ref[...], gpost_ref[...])
    x1_ref[...] = x1
    h_ref[...] = (_rms(x1, gpre_ref[...]) * (1.0 + mod[4:5]) + mod[3:4]).astype(h_ref.dtype)


def _postmix(x, y, g_post, g_pre, mod):
    n, d = x.shape
    tm = _tile(n // mod.shape[0], 256)
    row = pl.BlockSpec((tm, d), lambda i: (i, 0))
    gain = pl.BlockSpec((1, d), lambda i: (0, 0))
    return pl.pallas_call(
        _postmix_kernel,
        grid=(n // tm,),
        in_specs=[row, row, gain, gain, _mod_spec(mod, n, tm)],
        out_specs=[row, row],
        out_shape=[jax.ShapeDtypeStruct((n, d), F32), jax.ShapeDtypeStruct((n, d), BF16)],
        compiler_params=_params("parallel"),
        name="post_mix",
    )(x, y, g_post, g_pre, mod)


def _postmlp_kernel(x_ref, y_ref, g_ref, mod_ref, o_ref):
    o_ref[...] = x_ref[...] + mod_ref[0][5:6] * _rms(y_ref[...], g_ref[...])


def _postmlp(x, y, g_post, mod):
    n, d = x.shape
    tm = _tile(n // mod.shape[0], 256)
    row = pl.BlockSpec((tm, d), lambda i: (i, 0))
    return pl.pallas_call(
        _postmlp_kernel,
        grid=(n // tm,),
        in_specs=[row, row, pl.BlockSpec((1, d), lambda i: (0, 0)), _mod_spec(mod, n, tm)],
        out_specs=row,
        out_shape=jax.ShapeDtypeStruct((n, d), F32),
        compiler_params=_params("parallel"),
        name="post_mlp",
    )(x, y, g_post, mod)


def _axial_rope(n_tokens):
    rows = n_tokens // GRID_W
    row = jnp.repeat(jnp.arange(rows, dtype=F32), GRID_W)
    col = jnp.tile(jnp.arange(GRID_W, dtype=F32), rows)
    n_freq = QK_ROPE // 4
    inv_freq = ROPE_THETA ** (-jnp.arange(n_freq, dtype=F32) / n_freq)
    ang = jnp.concatenate([row[:, None] * inv_freq, col[:, None] * inv_freq], axis=-1)
    ang = jnp.concatenate([ang, ang], axis=-1)
    return jnp.cos(ang), jnp.sin(ang)


def _rotate_half_cols(w):
    half = w.shape[-1] // 2
    return jnp.concatenate([-w[..., half:], w[..., :half]], axis=-1)


def _prepare_weights(w_in, w_uq, w_ukv, w_glu, w_out, w_ff1, w_ff2, q_lora, kv_lora, s5_width, d_model):
    o1, o2, o3 = q_lora, q_lora + kv_lora, q_lora + kv_lora + QK_ROPE
    o4 = o3 + s5_width
    w_kr = w_in[:, o2:o3]
    pad = jnp.zeros((w_in.shape[0], 128 - QK_ROPE), w_in.dtype)
    w_ck = jnp.concatenate([w_in[:, o1:o2], w_kr, pad, _rotate_half_cols(w_kr), pad], axis=1)
    uq = w_uq.reshape(q_lora, N_HEADS, QK_NOPE + QK_ROPE)
    uq_plain = uq.transpose(1, 0, 2)
    uq_rope = jnp.concatenate([uq, _rotate_half_cols(uq[..., QK_NOPE:])], axis=-1).transpose(1, 0, 2)
    ukv = w_ukv.reshape(kv_lora, N_HEADS, QK_NOPE + V_HEAD).transpose(1, 0, 2)
    cast = lambda w: w.astype(BF16)
    return dict(w_q=cast(w_in[:, :o1]), w_ck=cast(w_ck), w_u=cast(w_in[:, o3:o4]), w_g=cast(w_in[:, o4:]),
                uq_plain=cast(uq_plain), uq_rope=cast(uq_rope), ukv=cast(ukv), w_glu=cast(w_glu),
                w_out=cast(w_out), w_ff1=cast(w_ff1), w_ff2=cast(w_ff2))


def _to_group_rows(u, batch, length, groups, j_dim, chunk):
    n_chunks = length // chunk
    u = u.reshape(batch, n_chunks, chunk, groups, j_dim).transpose(3, 0, 1, 2, 4)
    return u.reshape(groups, batch * n_chunks, chunk * j_dim)


def _from_group_rows(y, batch, length, groups, j_dim, chunk):
    n_chunks = length // chunk
    y = y.reshape(groups, batch, n_chunks, chunk, j_dim).transpose(1, 2, 3, 0, 4)
    return y.reshape(batch * length, groups * j_dim)


def _trunk(x, mod, wts, gains, s5_ops, s5_dims, ctx):
    batch, length, d_model = x.shape
    n = batch * length
    groups, p_dim, j_dim, chunk = s5_dims
    x2 = x.reshape(n, d_model)
    rope = None if ctx is None else ctx[3]

    h = _modnorm(x2, gains['g_pre_mix'], mod, 0, 1)
    q_lat = _mm(h, wts['w_q'], F32)
    ck = _mm(h, wts['w_ck'], F32)
    u = _mm(h, wts['w_u'], BF16)
    gates = _mm(h, wts['w_g'], F32)

    qn = _rmsnorm(q_lat, gains['g_q_lat'])
    q = _qproj(qn, wts['uq_plain'] if rope is None else wts['uq_rope'], batch, length, rope)
    ckv_n, k_rope = _kvprep(ck, gains['g_kv_lat'], length, rope)
    if ctx is None:
        ckv_all, kr_all, lk = ckv_n, k_rope, length
    else:
        lk = ctx[0].shape[1] + length
        ckv_all = jnp.concatenate([ctx[0], ckv_n.reshape(batch, length, -1)], axis=1).reshape(batch * lk, -1)
        kr_all = jnp.concatenate([ctx[1], k_rope.reshape(batch, length, -1)], axis=1).reshape(batch * lk, -1)
    k, v = _kvdec(ckv_all, kr_all, wts['ukv'], batch, lk)
    o_a = _attention(q, k, v).reshape(n, -1)

    if ctx is None:
        h0 = jnp.zeros((groups, batch, 4 * p_dim), F32)
    else:
        h0 = ctx[2].transpose(3, 0, 1, 2, 4).reshape(groups, batch, 4 * p_dim)
    u_rows = _to_group_rows(u, batch, length, groups, j_dim, chunk)
    y_rows, s5_fin = _s5(u_rows, s5_ops, h0, length // chunk)
    y = _from_group_rows(y_rows, batch, length, groups, j_dim, chunk)
    z = _mm(y, wts['w_glu'], F32)

    m = _merge(gates, o_a, z)
    mixed = _mm(m, wts['w_out'], F32)
    x1, h2 = _postmix(x2, mixed, gains['g_post_mix'], gains['g_pre_mlp'], mod)
    a1 = _mm(h2, wts['w_ff1'], BF16, relu2=True)
    a2 = _mm(a1, wts['w_ff2'], F32)
    y_out = _postmlp(x1, a2, gains['g_post_mlp'], mod).reshape(batch, length, d_model)
    s5_fin = s5_fin.reshape(groups, batch, 2, 2, p_dim).transpose(1, 2, 3, 0, 4)
    return y_out, ckv_n.reshape(batch, length, -1), k_rope.reshape(batch, length, -1), s5_fin


def kernel(x_prompt, x_sample, c, cache_ckv, cache_krope, state_s5, c_ctx, w_mod, b_mod, g_pre_mix, w_in, g_q_lat, g_kv_lat, w_uq, w_ukv, s5_lam_re, s5_lam_im, s5_log_dt, s5_b_re, s5_b_im, s5_c_re, s5_c_im, s5_d, w_glu, w_out, g_post_mix, g_pre_mlp, w_ff1, w_ff2, g_post_mlp):
    depth = w_mod.shape[0]
    d_model = x_prompt.shape[-1]
    dec_batch, dec_seq = x_sample.shape[:2]
    q_lora, kv_lora = g_q_lat.shape[1], g_kv_lat.shape[1]
    groups, p_dim, j_dim = s5_b_re.shape[2:]
    s5_width = groups * j_dim
    chunk = min(S5_CHUNK, x_prompt.shape[1] // 8)
    rope = _axial_rope(dec_seq)

    cond = jnp.concatenate([c_ctx[None, :], c], axis=0)
    cond = jnp.pad(cond, ((0, -cond.shape[0] % 8), (0, 0)))

    xp, xs = x_prompt, x_sample
    new_ckv, new_krope, new_s5 = [], [], []
    for l in range(depth):
        mods = _adaln(cond, w_mod[l], b_mod[l][None, :])
        mod_p = mods[0:1].reshape(1, -1, d_model)
        mod_s = mods[1:1 + dec_batch].reshape(dec_batch, -1, d_model)
        wts = _prepare_weights(w_in[l], w_uq[l], w_ukv[l], w_glu[l], w_out[l], w_ff1[l], w_ff2[l],
                               q_lora, kv_lora, s5_width, d_model)
        gains = dict(g_pre_mix=g_pre_mix[l][None], g_q_lat=g_q_lat[l][None], g_kv_lat=g_kv_lat[l][None],
                     g_post_mix=g_post_mix[l][None], g_pre_mlp=g_pre_mlp[l][None], g_post_mlp=g_post_mlp[l][None])
        max_chunks = max(x_prompt.shape[1], dec_seq) // chunk
        s5_ops = _s5_operators(s5_lam_re[l], s5_lam_im[l], s5_log_dt[l], s5_b_re[l], s5_b_im[l],
                               s5_c_re[l], s5_c_im[l], s5_d[l], chunk, max_chunks)
        s5_dims = (groups, p_dim, j_dim, chunk)

        xp, ckv_n, k_rope_p, s5_last = _trunk(xp, mod_p, wts, gains, s5_ops, s5_dims, None)
        new_ckv.append(ckv_n)
        new_krope.append(k_rope_p)
        new_s5.append(s5_last)
        ctx = (cache_ckv[:, l], cache_krope[:, l], state_s5[:, l], rope)
        xs, _, _, _ = _trunk(xs, mod_s, wts, gains, s5_ops, s5_dims, ctx)
    return (xp, xs, jnp.stack(new_ckv, axis=1), jnp.stack(new_krope, axis=1),
            jnp.stack(new_s5, axis=1).astype(x_prompt.dtype))
```

```python
import functools
import math

import jax
import jax.numpy as jnp
from jax import lax
from jax.experimental import pallas as pl
from jax.experimental.pallas import tpu as pltpu

N_HEADS = 32
QK_NOPE = 128
QK_ROPE = 64
V_HEAD = 128
GRID_W = 64
ROPE_THETA = 10000.0
EPS = 1e-6
S5_CHUNK = 32
VMEM_LIMIT_BYTES = 48 * 1024 * 1024

F32 = jnp.float32
BF16 = jnp.bfloat16


def _params(*semantics):
    return pltpu.CompilerParams(dimension_semantics=semantics, vmem_limit_bytes=VMEM_LIMIT_BYTES)


def _tile(n, target, align=8):
    for d in range(min(n, target), 0, -1):
        if n % d == 0 and d % align == 0:
            return d
    return n


def _adaln_kernel(c_ref, w_ref, b_ref, o_ref):
    c = c_ref[...]
    s = (c * jax.nn.sigmoid(c)).astype(BF16)
    o_ref[...] = jnp.dot(s, w_ref[...].astype(BF16), preferred_element_type=F32) + b_ref[...]


def _adaln(cond, w_mod, b_mod):
    rows, d = cond.shape
    n = w_mod.shape[1]
    tn = _tile(n, 512, 128)
    return pl.pallas_call(
        _adaln_kernel,
        grid=(n // tn,),
        in_specs=[pl.BlockSpec((rows, d), lambda j: (0, 0)),
                  pl.BlockSpec((d, tn), lambda j: (0, j)),
                  pl.BlockSpec((1, tn), lambda j: (0, j))],
        out_specs=pl.BlockSpec((rows, tn), lambda j: (0, j)),
        out_shape=jax.ShapeDtypeStruct((rows, n), F32),
        compiler_params=_params("parallel"),
        name="adaln",
    )(cond, w_mod, b_mod)


def _rms(x, g):
    return x * lax.rsqrt(jnp.mean(x * x, axis=-1, keepdims=True) + EPS) * g


def _modnorm_kernel(x_ref, g_ref, mod_ref, o_ref, *, shift_idx, scale_idx):
    mod = mod_ref[0]
    y = _rms(x_ref[...], g_ref[...])
    o_ref[...] = (y * (1.0 + mod[scale_idx:scale_idx + 1]) + mod[shift_idx:shift_idx + 1]).astype(o_ref.dtype)


def _mod_spec(mod, n_rows, tm):
    rows_per_mod = n_rows // mod.shape[0]
    return pl.BlockSpec((1,) + mod.shape[1:], lambda i: ((i * tm) // rows_per_mod, 0, 0))


def _modnorm(x, g, mod, shift_idx, scale_idx):
    n, d = x.shape
    tm = _tile(n // mod.shape[0], 256)
    return pl.pallas_call(
        functools.partial(_modnorm_kernel, shift_idx=shift_idx, scale_idx=scale_idx),
        grid=(n // tm,),
        in_specs=[pl.BlockSpec((tm, d), lambda i: (i, 0)),
                  pl.BlockSpec((1, d), lambda i: (0, 0)),
                  _mod_spec(mod, n, tm)],
        out_specs=pl.BlockSpec((tm, d), lambda i: (i, 0)),
        out_shape=jax.ShapeDtypeStruct((n, d), BF16),
        compiler_params=_params("parallel"),
        name="modnorm",
    )(x, g, mod)


def _rmsnorm_kernel(x_ref, g_ref, o_ref):
    o_ref[...] = _rms(x_ref[...], g_ref[...]).astype(o_ref.dtype)


def _rmsnorm(x, g):
    n, d = x.shape
    tm = _tile(n, 1024)
    return pl.pallas_call(
        _rmsnorm_kernel,
        grid=(n // tm,),
        in_specs=[pl.BlockSpec((tm, d), lambda i: (i, 0)),
                  pl.BlockSpec((1, d), lambda i: (0, 0))],
        out_specs=pl.BlockSpec((tm, d), lambda i: (i, 0)),
        out_shape=jax.ShapeDtypeStruct((n, d), BF16),
        compiler_params=_params("parallel"),
        name="rmsnorm",
    )(x, g)


def _epilogue(r, kind):
    if kind == "relu2":
        return jnp.square(jnp.maximum(r, 0.0))
    if kind == "sigmoid":
        return jax.nn.sigmoid(r)
    if kind == "glu":
        half = r.shape[1] // 2
        return r[:, :half] * jax.nn.sigmoid(r[:, half:])
    assert kind is None
    return r


def _mm_full_k_kernel(a_ref, w_ref, o_ref, *, epilogue):
    r = jnp.dot(a_ref[...], w_ref[...], preferred_element_type=F32)
    o_ref[...] = _epilogue(r, epilogue).astype(o_ref.dtype)


def _mm_kernel(a_ref, w_ref, o_ref, acc_ref, *, epilogue):
    k = pl.program_id(2)

    @pl.when(k == 0)
    def _():
        acc_ref[...] = jnp.zeros_like(acc_ref)

    acc_ref[...] += jnp.dot(a_ref[...], w_ref[...], preferred_element_type=F32)

    @pl.when(k == pl.num_programs(2) - 1)
    def _():
        o_ref[...] = _epilogue(acc_ref[...], epilogue).astype(o_ref.dtype)


MM_MAX_FULL_K = 4096


def _mm_tile_n(n, epilogue):
    return _tile(n, 1024, 256 if epilogue == "glu" else 128)


def _interleave_glu_cols(w):
    kdim, n = w.shape
    half = _mm_tile_n(n, "glu") // 2
    value, gate = w[:, :n // 2], w[:, n // 2:]
    return jnp.stack([value.reshape(kdim, -1, half), gate.reshape(kdim, -1, half)], axis=2).reshape(kdim, n)


def _mm(a, w, out_dtype, epilogue=None):
    m, kdim = a.shape
    n = w.shape[1]
    out_div = 2 if epilogue == "glu" else 1
    if kdim <= MM_MAX_FULL_K:
        tm = _tile(m, 512 if kdim > 2048 else 1024)
        tn = _mm_tile_n(n, epilogue)
        return pl.pallas_call(
            functools.partial(_mm_full_k_kernel, epilogue=epilogue),
            grid=(n // tn, m // tm),
            in_specs=[pl.BlockSpec((tm, kdim), lambda j, i: (i, 0)),
                      pl.BlockSpec((kdim, tn), lambda j, i: (0, j))],
            out_specs=pl.BlockSpec((tm, tn // out_div), lambda j, i: (i, j)),
            out_shape=jax.ShapeDtypeStruct((m, n // out_div), out_dtype),
            compiler_params=_params("parallel", "parallel"),
            name="matmul",
        )(a, w)
    tm, tn, tk = _tile(m, 1024), _mm_tile_n(n, epilogue), _tile(kdim, MM_MAX_FULL_K, 128)
    return pl.pallas_call(
        functools.partial(_mm_kernel, epilogue=epilogue),
        grid=(m // tm, n // tn, kdim // tk),
        in_specs=[pl.BlockSpec((tm, tk), lambda i, j, k: (i, k)),
                  pl.BlockSpec((tk, tn), lambda i, j, k: (k, j))],
        out_specs=pl.BlockSpec((tm, tn // out_div), lambda i, j, k: (i, j)),
        out_shape=jax.ShapeDtypeStruct((m, n // out_div), out_dtype),
        scratch_shapes=[pltpu.VMEM((tm, tn), F32)],
        compiler_params=_params("parallel", "parallel", "arbitrary"),
        name="matmul_k",
    )(a, w)


def _qproj_kernel(a_ref, w_ref, *rest, scale, use_rope):
    if use_rope:
        cos_ref, sin_ref, o_ref = rest
    else:
        (o_ref,) = rest
    nb, hb, tl, _ = o_ref.shape
    a = a_ref[...]
    for h in range(hb):
        acc = jnp.dot(a, w_ref[h], preferred_element_type=F32)
        o_ref[:, h, :, :QK_NOPE] = (acc[:, :QK_NOPE] * scale).reshape(nb, tl, QK_NOPE).astype(o_ref.dtype)
        rope = acc[:, QK_NOPE:QK_NOPE + QK_ROPE]
        if use_rope:
            rope = rope * cos_ref[...] + acc[:, QK_NOPE + QK_ROPE:QK_NOPE + 2 * QK_ROPE] * sin_ref[...]
        o_ref[:, h, :, QK_NOPE:] = (rope * scale).reshape(nb, tl, QK_ROPE).astype(o_ref.dtype)


HEADS_PER_STEP = 4


def _row_blocking(batch, length, target):
    if length >= target:
        return 1, _tile(length, target)
    return _tile(batch, max(1, target // length), 1), length


def _head_out_spec(nb, hb, tl, length, width):
    per_batch = length // tl
    return pl.BlockSpec((nb, hb, tl, width), lambda i, h: (i // per_batch, h, i % per_batch, 0))


def _qproj(qn, w_heads, batch, length, rope_tables):
    n, kdim = qn.shape
    heads, _, cols = w_heads.shape
    use_rope = rope_tables is not None
    nb, tl = (1, _tile(length, 1024)) if use_rope else _row_blocking(batch, length, 1024)
    tm = nb * tl
    hb = _tile(heads, HEADS_PER_STEP, 1)
    in_specs = [pl.BlockSpec((tm, kdim), lambda i, h: (i, 0)),
                pl.BlockSpec((hb, kdim, cols), lambda i, h: (h, 0, 0))]
    args = [qn, w_heads]
    if use_rope:
        per_batch = length // tl
        spec = pl.BlockSpec((tl, QK_ROPE), lambda i, h: (i % per_batch, 0))
        in_specs += [spec, spec]
        args += list(rope_tables)
    width = QK_NOPE + QK_ROPE
    return pl.pallas_call(
        functools.partial(_qproj_kernel, scale=width ** -0.5, use_rope=use_rope),
        grid=(n // tm, heads // hb),
        in_specs=in_specs,
        out_specs=_head_out_spec(nb, hb, tl, length, width),
        out_shape=jax.ShapeDtypeStruct((batch, heads, length, width), BF16),
        compiler_params=_params("parallel", "parallel"),
        name="q_proj",
    )(*args)


def _kvprep_kernel(ck_ref, g_ref, *rest, kv_lora, use_rope):
    if use_rope:
        cos_ref, sin_ref, ckv_ref, kr_ref = rest
    else:
        ckv_ref, kr_ref = rest
    ckv_ref[...] = _rms(ck_ref[:, :kv_lora], g_ref[...])
    kr = ck_ref[:, kv_lora:kv_lora + QK_ROPE]
    if use_rope:
        kr = kr * cos_ref[...] + ck_ref[:, kv_lora + 128:kv_lora + 128 + QK_ROPE] * sin_ref[...]
    kr_ref[...] = kr


def _kvprep(ck, g_kv, length, rope_tables):
    n, cols = ck.shape
    kv_lora = cols - 256
    tm = _tile(length, 512)
    use_rope = rope_tables is not None
    in_specs = [pl.BlockSpec((tm, cols), lambda i: (i, 0)),
                pl.BlockSpec((1, kv_lora), lambda i: (0, 0))]
    args = [ck, g_kv]
    if use_rope:
        per_batch = length // tm
        spec = pl.BlockSpec((tm, QK_ROPE), lambda i: (i % per_batch, 0))
        in_specs += [spec, spec]
        args += list(rope_tables)
    return pl.pallas_call(
        functools.partial(_kvprep_kernel, kv_lora=kv_lora, use_rope=use_rope),
        grid=(n // tm,),
        in_specs=in_specs,
        out_specs=[pl.BlockSpec((tm, kv_lora), lambda i: (i, 0)),
                   pl.BlockSpec((tm, QK_ROPE), lambda i: (i, 0))],
        out_shape=[jax.ShapeDtypeStruct((n, kv_lora), F32),
                   jax.ShapeDtypeStruct((n, QK_ROPE), F32)],
        compiler_params=_params("parallel"),
        name="kv_prep",
    )(*args)


def _kvdec_kernel(c_ref, kr_ref, w_ref, k_ref, v_ref):
    nb, hb, tl, _ = k_ref.shape
    c = c_ref[...].astype(BF16)
    k_rope = kr_ref[...].reshape(nb, tl, QK_ROPE).astype(k_ref.dtype)
    for h in range(hb):
        acc = jnp.dot(c, w_ref[h], preferred_element_type=F32)
        k_ref[:, h, :, :QK_NOPE] = acc[:, :QK_NOPE].reshape(nb, tl, QK_NOPE).astype(k_ref.dtype)
        k_ref[:, h, :, QK_NOPE:] = k_rope
        v_ref[:, h] = acc[:, QK_NOPE:].reshape(nb, tl, V_HEAD).astype(v_ref.dtype)


def _kvdec(ckv, krope, w_heads, batch, length):
    n, kdim = ckv.shape
    heads, _, cols = w_heads.shape
    nb, tl = _row_blocking(batch, length, 1536)
    tm = nb * tl
    hb = _tile(heads, HEADS_PER_STEP, 1)
    return pl.pallas_call(
        _kvdec_kernel,
        grid=(n // tm, heads // hb),
        in_specs=[pl.BlockSpec((tm, kdim), lambda i, h: (i, 0)),
                  pl.BlockSpec((tm, QK_ROPE), lambda i, h: (i, 0)),
                  pl.BlockSpec((hb, kdim, cols), lambda i, h: (h, 0, 0))],
        out_specs=[_head_out_spec(nb, hb, tl, length, QK_NOPE + QK_ROPE),
                   _head_out_spec(nb, hb, tl, length, V_HEAD)],
        out_shape=[jax.ShapeDtypeStruct((batch, heads, length, QK_NOPE + QK_ROPE), BF16),
                   jax.ShapeDtypeStruct((batch, heads, length, V_HEAD), BF16)],
        compiler_params=_params("parallel", "parallel"),
        name="kv_decompress",
    )(ckv, krope, w_heads)


def _qk(q, k):
    return lax.dot_general(q, k, (((1,), (1,)), ((), ())), preferred_element_type=F32)


def _attn_kernel(q_ref, k_ref, v_ref, o_ref, *stat_refs, tk, nk, sub):
    hb, tq, dv = q_ref.shape[1], q_ref.shape[2], v_ref.shape[3]
    if nk > 1:
        m_ref, l_ref, acc_ref = stat_refs
    lanes = dv
    for h in range(hb):
        for r in range(tq // sub):
            rows = pl.ds(r * sub, sub)
            q = q_ref[0, h, rows, :]
            s = _qk(q, k_ref[0, h, :tk, :])
            m = jnp.max(s, axis=-1, keepdims=True)
            p = jnp.exp(s - m)
            l = jnp.sum(p, axis=-1, keepdims=True)
            acc = jnp.dot(p.astype(BF16), v_ref[0, h, :tk, :], preferred_element_type=F32)
            if nk > 1:
                m_ref[h, rows, :] = jnp.broadcast_to(m, (sub, lanes))
                l_ref[h, rows, :] = jnp.broadcast_to(l, (sub, lanes))
                acc_ref[h, rows, :] = acc
            else:
                o_ref[0, rows, h * dv:(h + 1) * dv] = (acc / l).astype(o_ref.dtype)
        for j in range(1, nk):
            for r in range(tq // sub):
                rows = pl.ds(r * sub, sub)
                q = q_ref[0, h, rows, :]
                s = _qk(q, k_ref[0, h, j * tk:(j + 1) * tk, :])
                m_old = m_ref[h, rows, :]
                m_new = jnp.maximum(m_old, jnp.max(s, axis=-1, keepdims=True))
                alpha = jnp.exp(m_old - m_new)
                p = jnp.exp(s - m_new[:, :1])
                l = alpha * l_ref[h, rows, :] + jnp.sum(p, axis=-1, keepdims=True)
                acc = alpha * acc_ref[h, rows, :] + jnp.dot(
                    p.astype(BF16), v_ref[0, h, j * tk:(j + 1) * tk, :], preferred_element_type=F32)
                if j < nk - 1:
                    m_ref[h, rows, :] = m_new
                    l_ref[h, rows, :] = l
                    acc_ref[h, rows, :] = acc
                else:
                    o_ref[0, rows, h * dv:(h + 1) * dv] = (acc / l).astype(o_ref.dtype)


ATTN_MAX_KV_CHUNK = 2304
ATTN_STEP_ROWS = 2048


def _attention(q, k, v):
    batch, heads, lq, dqk = q.shape
    lk, dv = k.shape[2], v.shape[3]
    tq = _tile(lq, 1024)
    sub = _tile(tq, 512)
    hb = _tile(heads, max(1, ATTN_STEP_ROWS // lq), 1) if tq == lq else 1
    tk = _tile(lk, ATTN_MAX_KV_CHUNK, 128) if lk > ATTN_MAX_KV_CHUNK else lk
    nk = lk // tk
    stats = [pltpu.VMEM((hb, tq, dv), F32)] * 3 if nk > 1 else []
    return pl.pallas_call(
        functools.partial(_attn_kernel, tk=tk, nk=nk, sub=sub),
        grid=(batch, heads // hb, lq // tq),
        in_specs=[pl.BlockSpec((1, hb, tq, dqk), lambda b, h, i: (b, h, i, 0)),
                  pl.BlockSpec((1, hb, lk, dqk), lambda b, h, i: (b, h, 0, 0)),
                  pl.BlockSpec((1, hb, lk, dv), lambda b, h, i: (b, h, 0, 0))],
        out_specs=pl.BlockSpec((1, tq, hb * dv), lambda b, h, i: (b, i, h)),
        out_shape=jax.ShapeDtypeStruct((batch, lq, heads * dv), BF16),
        scratch_shapes=stats,
        compiler_params=_params("parallel", "parallel", "arbitrary"),
        name="attention",
    )(q, k, v)


def _gelu_tanh(x):
    return 0.5 * x * (1.0 + jnp.tanh(math.sqrt(2.0 / math.pi) * (x + 0.044715 * (x * x * x))))


def _s5_kernel(u_ref, w_ref, g_ref, e_ref, d_ref, ar_ref, ai_ref, h0_ref, y_ref, fin_ref, *, n_chunks):
    u = u_ref[0]
    rows = u.shape[0]
    batch = rows // n_chunks
    half = g_ref.shape[2] // 2
    g = jnp.dot(u, g_ref[0], preferred_element_type=F32)
    h0 = jnp.broadcast_to(h0_ref[0][:, None, :], (batch, n_chunks, 2 * half)).reshape(rows, 2 * half)
    chunk = lax.broadcasted_iota(jnp.int32, (rows, half), 0) % n_chunks
    ar = ar_ref[0]
    ai = ai_ref[0]

    def cmul(x, step, lo):
        swapped = pltpu.roll(x, half // 2, axis=1)
        return x * ar[step:step + 1, lo:lo + half] + swapped * ai[step:step + 1, lo:lo + half]

    def scan(gd, h0d, lo, backward):
        if backward:
            z = jnp.where(chunk == n_chunks - 1, h0d, pltpu.roll(gd, rows - 1, axis=0))
        else:
            z = jnp.where(chunk == 0, h0d, pltpu.roll(gd, 1, axis=0))
        step, dist = 0, 1
        while dist < n_chunks:
            if backward:
                moved = jnp.where(chunk < n_chunks - dist, pltpu.roll(z, rows - dist, axis=0), 0.0)
            else:
                moved = jnp.where(chunk >= dist, pltpu.roll(z, dist, axis=0), 0.0)
            z = z + cmul(moved, step, lo)
            step, dist = step + 1, dist * 2
        return z

    gf, gb = g[:, :half], g[:, half:]
    pf = scan(gf, h0[:, :half], 0, False)
    pb = scan(gb, h0[:, half:], half, True)
    p = jnp.concatenate([pf, pb], axis=1).astype(BF16)
    y = (jnp.dot(u, w_ref[0], preferred_element_type=F32)
         + jnp.dot(p, e_ref[0], preferred_element_type=F32)
         + u.astype(F32) * d_ref[0])
    y_ref[0] = _gelu_tanh(y).astype(y_ref.dtype)
    def pick(x, which):
        return jnp.sum(jnp.where(chunk == which, x, 0.0).reshape(batch, n_chunks, half), axis=1)

    ff = pick(cmul(pf, 0, 0) + gf, n_chunks - 1)
    fb = pick(cmul(pb, 0, half) + gb, 0)
    fin_ref[0] = jnp.concatenate([ff, fb], axis=1)


def _s5(u_rows, ops, h0, n_chunks):
    groups, rows, width = u_rows.shape
    batch = rows // n_chunks
    wt, gt, et, dvec, ar, ai = ops
    lanes = gt.shape[2]

    def per_group(*shape):
        return pl.BlockSpec((1,) + shape, lambda g: (g,) + (0,) * len(shape))

    return pl.pallas_call(
        functools.partial(_s5_kernel, n_chunks=n_chunks),
        grid=(groups,),
        in_specs=[per_group(rows, width), per_group(width, width), per_group(width, lanes),
                  per_group(lanes, width), per_group(1, width), per_group(*ar.shape[1:]),
                  per_group(*ai.shape[1:]), per_group(batch, lanes)],
        out_specs=[per_group(rows, width), per_group(batch, lanes)],
        out_shape=[jax.ShapeDtypeStruct((groups, rows, width), BF16),
                   jax.ShapeDtypeStruct((groups, batch, lanes), F32)],
        compiler_params=_params("parallel"),
        name="s5",
    )(u_rows, wt, gt, et, dvec, ar, ai, h0)


def _s5_operators(lam_re, lam_im, log_dt, b_re, b_im, c_re, c_im, d_skip, chunk, max_chunks):
    hi = lax.Precision.HIGHEST
    groups, p_dim, j_dim = b_re.shape[1:]
    t = chunk
    lam = lax.complex(lam_re, lam_im)
    z = lam * jnp.exp(log_dt)[..., None]
    a = jnp.exp(z)
    b_bar = ((a - 1.0) / lam)[..., None] * lax.complex(b_re, b_im)
    c_mat = lax.complex(c_re, c_im)
    k_idx = jnp.arange(t + 1, dtype=F32)
    apow = jnp.exp(z[None] * k_idx[:, None, None, None])

    cb = c_mat[..., None] * b_bar[:, :, None]
    ap = jnp.moveaxis(apow[:t], 1, 0)
    kern = (jnp.einsum('dtgp,dgopi->dtgoi', ap.real, cb.real, precision=hi)
            - jnp.einsum('dtgp,dgopi->dtgoi', ap.imag, cb.imag, precision=hi))
    i_idx = jnp.arange(t)
    lag = i_idx[None, :] - i_idx[:, None]
    w_f = jnp.where((lag >= 0)[:, :, None, None, None], kern[0][jnp.clip(lag, 0, t - 1)], 0.0)
    w_b = jnp.where((lag <= 0)[:, :, None, None, None], kern[1][jnp.clip(-lag, 0, t - 1)], 0.0)
    w_all = (w_f + w_b).transpose(2, 0, 4, 1, 3).reshape(groups, t * j_dim, t * j_dim)

    inj_f = apow[t - 1 - i_idx, 0][..., None] * b_bar[0][None]
    inj_b = apow[i_idx, 1][..., None] * b_bar[1][None]
    g_all = jnp.concatenate([inj_f.real, inj_f.imag, inj_b.real, inj_b.imag], axis=2)
    g_all = g_all.transpose(1, 0, 3, 2).reshape(groups, t * j_dim, 4 * p_dim)

    out_f = c_mat[0][None] * apow[i_idx + 1, 0][:, :, None, :]
    out_b = c_mat[1][None] * apow[t - i_idx, 1][:, :, None, :]
    e_all = jnp.concatenate([out_f.real, -out_f.imag, out_b.real, -out_b.imag], axis=3)
    e_all = e_all.transpose(1, 3, 0, 2).reshape(groups, 4 * p_dim, t * j_dim)

    d_row = jnp.tile(d_skip.reshape(groups, 1, j_dim), (1, t, 1)).reshape(groups, 1, t * j_dim)

    n_steps = max(1, (max_chunks - 1).bit_length())
    steps = (t * 2 ** jnp.arange(n_steps)).astype(F32)
    tp = jnp.exp(z[None] * steps[:, None, None, None])
    a_r = jnp.concatenate([tp[:, 0].real, tp[:, 0].real, tp[:, 1].real, tp[:, 1].real], axis=-1)
    a_i = jnp.concatenate([-tp[:, 0].imag, tp[:, 0].imag, -tp[:, 1].imag, tp[:, 1].imag], axis=-1)
    a_r = a_r.transpose(1, 0, 2)
    a_i = a_i.transpose(1, 0, 2)
    return w_all.astype(BF16), g_all.astype(BF16), e_all.astype(BF16), d_row, a_r, a_i


def _merge_kernel(sa_ref, sb_ref, oa_ref, ob_ref, o_ref):
    o_ref[...] = (sa_ref[...].astype(F32) * oa_ref[...].astype(F32)
                  + sb_ref[...].astype(F32) * ob_ref[...].astype(F32)).astype(o_ref.dtype)


def _merge(gate_sig, o_a, o_b):
    n, d = o_a.shape
    tm = _tile(n, 256)
    left = pl.BlockSpec((tm, d), lambda i: (i, 0))
    right = pl.BlockSpec((tm, d), lambda i: (i, 1))
    return pl.pallas_call(
        _merge_kernel,
        grid=(n // tm,),
        in_specs=[left, right, left, left],
        out_specs=left,
        out_shape=jax.ShapeDtypeStruct((n, d), BF16),
        compiler_params=_params("parallel"),
        name="merge",
    )(gate_sig, gate_sig, o_a, o_b)


def _postmix_kernel(x_ref, y_ref, gpost_ref, gpre_ref, mod_ref, x1_ref, h_ref):
    mod = mod_ref[0]
    x1 = x_ref[...] + mod[2:3] * _rms(y_ref[...].astype(F32), gpost_ref[...])
    x1_ref[...] = x1
    h_ref[...] = (_rms(x1, gpre_ref[...]) * (1.0 + mod[4:5]) + mod[3:4]).astype(h_ref.dtype)


def _postmix(x, y, g_post, g_pre, mod):
    n, d = x.shape
    tm = _tile(n // mod.shape[0], 256)
    row = pl.BlockSpec((tm, d), lambda i: (i, 0))
    gain = pl.BlockSpec((1, d), lambda i: (0, 0))
    return pl.pallas_call(
        _postmix_kernel,
        grid=(n // tm,),
        in_specs=[row, row, gain, gain, _mod_spec(mod, n, tm)],
        out_specs=[row, row],
        out_shape=[jax.ShapeDtypeStruct((n, d), F32), jax.ShapeDtypeStruct((n, d), BF16)],
        compiler_params=_params("parallel"),
        name="post_mix",
    )(x, y, g_post, g_pre, mod)


def _postmlp_kernel(x_ref, y_ref, g_ref, mod_ref, o_ref):
    o_ref[...] = x_ref[...] + mod_ref[0][5:6] * _rms(y_ref[...].astype(F32), g_ref[...])


def _postmlp(x, y, g_post, mod):
    n, d = x.shape
    tm = _tile(n // mod.shape[0], 256)
    row = pl.BlockSpec((tm, d), lambda i: (i, 0))
    return pl.pallas_call(
        _postmlp_kernel,
        grid=(n // tm,),
        in_specs=[row, row, pl.BlockSpec((1, d), lambda i: (0, 0)), _mod_spec(mod, n, tm)],
        out_specs=row,
        out_shape=jax.ShapeDtypeStruct((n, d), F32),
        compiler_params=_params("parallel"),
        name="post_mlp",
    )(x, y, g_post, mod)


def _axial_rope(n_tokens):
    rows = n_tokens // GRID_W
    row = jnp.repeat(jnp.arange(rows, dtype=F32), GRID_W)
    col = jnp.tile(jnp.arange(GRID_W, dtype=F32), rows)
    n_freq = QK_ROPE // 4
    inv_freq = ROPE_THETA ** (-jnp.arange(n_freq, dtype=F32) / n_freq)
    ang = jnp.concatenate([row[:, None] * inv_freq, col[:, None] * inv_freq], axis=-1)
    ang = jnp.concatenate([ang, ang], axis=-1)
    return jnp.cos(ang), jnp.sin(ang)


def _rotate_half_cols(w):
    half = w.shape[-1] // 2
    return jnp.concatenate([-w[..., half:], w[..., :half]], axis=-1)


def _prepare_weights(w_in, w_uq, w_ukv, w_glu, w_out, w_ff1, w_ff2, q_lora, kv_lora, s5_width, d_model):
    o1, o2, o3 = q_lora, q_lora + kv_lora, q_lora + kv_lora + QK_ROPE
    o4 = o3 + s5_width
    w_kr = w_in[:, o2:o3]
    pad = jnp.zeros((w_in.shape[0], 128 - QK_ROPE), w_in.dtype)
    w_ck = jnp.concatenate([w_in[:, o1:o2], w_kr, pad, _rotate_half_cols(w_kr), pad], axis=1)
    uq = w_uq.reshape(q_lora, N_HEADS, QK_NOPE + QK_ROPE)
    uq_plain = uq.transpose(1, 0, 2)
    uq_rope = jnp.concatenate([uq, _rotate_half_cols(uq[..., QK_NOPE:])], axis=-1).transpose(1, 0, 2)
    ukv = w_ukv.reshape(kv_lora, N_HEADS, QK_NOPE + V_HEAD).transpose(1, 0, 2)
    cast = lambda w: w.astype(BF16)
    return dict(w_q=cast(w_in[:, :o1]), w_ck=cast(w_ck), w_u=cast(w_in[:, o3:o4]), w_g=cast(w_in[:, o4:]),
                uq_plain=cast(uq_plain), uq_rope=cast(uq_rope), ukv=cast(ukv),
                w_glu=cast(_interleave_glu_cols(w_glu)),
                w_out=cast(w_out), w_ff1=cast(w_ff1), w_ff2=cast(w_ff2))


def _to_group_rows(u, batch, length, groups, j_dim, chunk):
    n_chunks = length // chunk
    u = u.reshape(batch, n_chunks, chunk, groups, j_dim).transpose(3, 0, 1, 2, 4)
    return u.reshape(groups, batch * n_chunks, chunk * j_dim)


def _from_group_rows(y, batch, length, groups, j_dim, chunk):
    n_chunks = length // chunk
    y = y.reshape(groups, batch, n_chunks, chunk, j_dim).transpose(1, 2, 3, 0, 4)
    return y.reshape(batch * length, groups * j_dim)


def _trunk(x, mod, wts, gains, s5_ops, s5_dims, ctx):
    batch, length, d_model = x.shape
    n = batch * length
    groups, p_dim, j_dim, chunk = s5_dims
    x2 = x.reshape(n, d_model)
    rope = None if ctx is None else ctx[3]

    h = _modnorm(x2, gains['g_pre_mix'], mod, 0, 1)
    q_lat = _mm(h, wts['w_q'], F32)
    ck = _mm(h, wts['w_ck'], F32)
    u = _mm(h, wts['w_u'], BF16)
    gate_sig = _mm(h, wts['w_g'], BF16, epilogue="sigmoid")

    qn = _rmsnorm(q_lat, gains['g_q_lat'])
    q = _qproj(qn, wts['uq_plain'] if rope is None else wts['uq_rope'], batch, length, rope)
    ckv_n, k_rope = _kvprep(ck, gains['g_kv_lat'], length, rope)
    if ctx is None:
        ckv_all, kr_all, lk = ckv_n, k_rope, length
    else:
        lk = ctx[0].shape[1] + length
        ckv_all = jnp.concatenate([ctx[0], ckv_n.reshape(batch, length, -1)], axis=1).reshape(batch * lk, -1)
        kr_all = jnp.concatenate([ctx[1], k_rope.reshape(batch, length, -1)], axis=1).reshape(batch * lk, -1)
    k, v = _kvdec(ckv_all, kr_all, wts['ukv'], batch, lk)
    o_a = _attention(q, k, v).reshape(n, -1)

    if ctx is None:
        h0 = jnp.zeros((groups, batch, 4 * p_dim), F32)
    else:
        h0 = ctx[2].transpose(3, 0, 1, 2, 4).reshape(groups, batch, 4 * p_dim)
    u_rows = _to_group_rows(u, batch, length, groups, j_dim, chunk)
    y_rows, s5_fin = _s5(u_rows, s5_ops, h0, length // chunk)
    y = _from_group_rows(y_rows, batch, length, groups, j_dim, chunk)
    o_b = _mm(y, wts['w_glu'], BF16, epilogue="glu")

    m = _merge(gate_sig, o_a, o_b)
    mixed = _mm(m, wts['w_out'], BF16)
    x1, h2 = _postmix(x2, mixed, gains['g_post_mix'], gains['g_pre_mlp'], mod)
    a1 = _mm(h2, wts['w_ff1'], BF16, epilogue="relu2")
    a2 = _mm(a1, wts['w_ff2'], BF16)
    y_out = _postmlp(x1, a2, gains['g_post_mlp'], mod).reshape(batch, length, d_model)
    s5_fin = s5_fin.reshape(groups, batch, 2, 2, p_dim).transpose(1, 2, 3, 0, 4)
    return y_out, ckv_n.reshape(batch, length, -1), k_rope.reshape(batch, length, -1), s5_fin


def kernel(x_prompt, x_sample, c, cache_ckv, cache_krope, state_s5, c_ctx, w_mod, b_mod, g_pre_mix, w_in, g_q_lat, g_kv_lat, w_uq, w_ukv, s5_lam_re, s5_lam_im, s5_log_dt, s5_b_re, s5_b_im, s5_c_re, s5_c_im, s5_d, w_glu, w_out, g_post_mix, g_pre_mlp, w_ff1, w_ff2, g_post_mlp):
    depth = w_mod.shape[0]
    d_model = x_prompt.shape[-1]
    dec_batch, dec_seq = x_sample.shape[:2]
    q_lora, kv_lora = g_q_lat.shape[1], g_kv_lat.shape[1]
    groups, p_dim, j_dim = s5_b_re.shape[2:]
    s5_width = groups * j_dim
    chunk = min(S5_CHUNK, x_prompt.shape[1] // 8)
    rope = _axial_rope(dec_seq)

    cond = jnp.concatenate([c_ctx[None, :], c], axis=0)
    cond = jnp.pad(cond, ((0, -cond.shape[0] % 8), (0, 0)))

    xp, xs = x_prompt, x_sample
    new_ckv, new_krope, new_s5 = [], [], []
    for l in range(depth):
        mods = _adaln(cond, w_mod[l], b_mod[l][None, :])
        mod_p = mods[0:1].reshape(1, -1, d_model)
        mod_s = mods[1:1 + dec_batch].reshape(dec_batch, -1, d_model)
        wts = _prepare_weights(w_in[l], w_uq[l], w_ukv[l], w_glu[l], w_out[l], w_ff1[l], w_ff2[l],
                               q_lora, kv_lora, s5_width, d_model)
        gains = dict(g_pre_mix=g_pre_mix[l][None], g_q_lat=g_q_lat[l][None], g_kv_lat=g_kv_lat[l][None],
                     g_post_mix=g_post_mix[l][None], g_pre_mlp=g_pre_mlp[l][None], g_post_mlp=g_post_mlp[l][None])
        max_chunks = max(x_prompt.shape[1], dec_seq) // chunk
        s5_ops = _s5_operators(s5_lam_re[l], s5_lam_im[l], s5_log_dt[l], s5_b_re[l], s5_b_im[l],
                               s5_c_re[l], s5_c_im[l], s5_d[l], chunk, max_chunks)
        s5_dims = (groups, p_dim, j_dim, chunk)

        xp, ckv_n, k_rope_p, s5_last = _trunk(xp, mod_p, wts, gains, s5_ops, s5_dims, None)
        new_ckv.append(ckv_n)
        new_krope.append(k_rope_p)
        new_s5.append(s5_last)
        ctx = (cache_ckv[:, l], cache_krope[:, l], state_s5[:, l], rope)
        xs, _, _, _ = _trunk(xs, mod_s, wts, gains, s5_ops, s5_dims, ctx)
    return (xp, xs, jnp.stack(new_ckv, axis=1), jnp.stack(new_krope, axis=1),
            jnp.stack(new_s5, axis=1).astype(x_prompt.dtype))
```

```python
import functools
import math

import jax
import jax.numpy as jnp
from jax import lax
from jax.experimental import pallas as pl
from jax.experimental.pallas import tpu as pltpu

N_HEADS = 32
QK_NOPE = 128
QK_ROPE = 64
V_HEAD = 128
GRID_W = 64
ROPE_THETA = 10000.0
EPS = 1e-6
S5_CHUNK = 32
VMEM_LIMIT_BYTES = 48 * 1024 * 1024

F32 = jnp.float32
BF16 = jnp.bfloat16


def _params(*semantics):
    return pltpu.CompilerParams(dimension_semantics=semantics, vmem_limit_bytes=VMEM_LIMIT_BYTES)


def _tile(n, target, align=8):
    for d in range(min(n, target), 0, -1):
        if n % d == 0 and d % align == 0:
            return d
    return n


def _adaln_kernel(c_ref, w_ref, b_ref, o_ref):
    c = c_ref[...]
    s = (c * jax.nn.sigmoid(c)).astype(BF16)
    o_ref[...] = jnp.dot(s, w_ref[...].astype(BF16), preferred_element_type=F32) + b_ref[...]


def _adaln(cond, w_mod, b_mod):
    rows, d = cond.shape
    n = w_mod.shape[1]
    tn = _tile(n, 512, 128)
    return pl.pallas_call(
        _adaln_kernel,
        grid=(n // tn,),
        in_specs=[pl.BlockSpec((rows, d), lambda j: (0, 0)),
                  pl.BlockSpec((d, tn), lambda j: (0, j)),
                  pl.BlockSpec((1, tn), lambda j: (0, j))],
        out_specs=pl.BlockSpec((rows, tn), lambda j: (0, j)),
        out_shape=jax.ShapeDtypeStruct((rows, n), F32),
        compiler_params=_params("parallel"),
        name="adaln",
    )(cond, w_mod, b_mod)


def _rms(x, g):
    return x * lax.rsqrt(jnp.mean(x * x, axis=-1, keepdims=True) + EPS) * g


def _modnorm_kernel(x_ref, g_ref, mod_ref, o_ref, *, shift_idx, scale_idx):
    mod = mod_ref[0]
    y = _rms(x_ref[...], g_ref[...])
    o_ref[...] = (y * (1.0 + mod[scale_idx:scale_idx + 1]) + mod[shift_idx:shift_idx + 1]).astype(o_ref.dtype)


def _mod_spec(mod, n_rows, tm):
    rows_per_mod = n_rows // mod.shape[0]
    return pl.BlockSpec((1,) + mod.shape[1:], lambda i: ((i * tm) // rows_per_mod, 0, 0))


def _modnorm(x, g, mod, shift_idx, scale_idx):
    n, d = x.shape
    tm = _tile(n // mod.shape[0], 256)
    return pl.pallas_call(
        functools.partial(_modnorm_kernel, shift_idx=shift_idx, scale_idx=scale_idx),
        grid=(n // tm,),
        in_specs=[pl.BlockSpec((tm, d), lambda i: (i, 0)),
                  pl.BlockSpec((1, d), lambda i: (0, 0)),
                  _mod_spec(mod, n, tm)],
        out_specs=pl.BlockSpec((tm, d), lambda i: (i, 0)),
        out_shape=jax.ShapeDtypeStruct((n, d), BF16),
        compiler_params=_params("parallel"),
        name="modnorm",
    )(x, g, mod)


def _rmsnorm_kernel(x_ref, g_ref, o_ref):
    o_ref[...] = _rms(x_ref[...], g_ref[...]).astype(o_ref.dtype)


def _rmsnorm(x, g):
    n, d = x.shape
    tm = _tile(n, 1024)
    return pl.pallas_call(
        _rmsnorm_kernel,
        grid=(n // tm,),
        in_specs=[pl.BlockSpec((tm, d), lambda i: (i, 0)),
                  pl.BlockSpec((1, d), lambda i: (0, 0))],
        out_specs=pl.BlockSpec((tm, d), lambda i: (i, 0)),
        out_shape=jax.ShapeDtypeStruct((n, d), BF16),
        compiler_params=_params("parallel"),
        name="rmsnorm",
    )(x, g)


def _epilogue(r, kind):
    if kind == "relu2":
        return jnp.square(jnp.maximum(r, 0.0))
    if kind == "sigmoid":
        return jax.nn.sigmoid(r)
    assert kind is None
    return r


def _mm_full_k_kernel(a_ref, w_ref, o_ref, *, epilogue):
    r = jnp.dot(a_ref[...], w_ref[...], preferred_element_type=F32)
    o_ref[...] = _epilogue(r, epilogue).astype(o_ref.dtype)


def _mm_kernel(a_ref, w_ref, o_ref, acc_ref, *, epilogue):
    k = pl.program_id(2)

    @pl.when(k == 0)
    def _():
        acc_ref[...] = jnp.zeros_like(acc_ref)

    acc_ref[...] += jnp.dot(a_ref[...], w_ref[...], preferred_element_type=F32)

    @pl.when(k == pl.num_programs(2) - 1)
    def _():
        o_ref[...] = _epilogue(acc_ref[...], epilogue).astype(o_ref.dtype)


MM_MAX_FULL_K = 4096


def _mm(a, w, out_dtype, epilogue=None):
    m, kdim = a.shape
    n = w.shape[1]
    if kdim <= MM_MAX_FULL_K:
        tm = _tile(m, 512 if kdim > 2048 else 1024)
        tn = _tile(n, 1024, 128)
        return pl.pallas_call(
            functools.partial(_mm_full_k_kernel, epilogue=epilogue),
            grid=(n // tn, m // tm),
            in_specs=[pl.BlockSpec((tm, kdim), lambda j, i: (i, 0)),
                      pl.BlockSpec((kdim, tn), lambda j, i: (0, j))],
            out_specs=pl.BlockSpec((tm, tn), lambda j, i: (i, j)),
            out_shape=jax.ShapeDtypeStruct((m, n), out_dtype),
            compiler_params=_params("parallel", "parallel"),
            name="matmul",
        )(a, w)
    tm, tn, tk = _tile(m, 1024), _tile(n, 1024, 128), _tile(kdim, MM_MAX_FULL_K, 128)
    return pl.pallas_call(
        functools.partial(_mm_kernel, epilogue=epilogue),
        grid=(m // tm, n // tn, kdim // tk),
        in_specs=[pl.BlockSpec((tm, tk), lambda i, j, k: (i, k)),
                  pl.BlockSpec((tk, tn), lambda i, j, k: (k, j))],
        out_specs=pl.BlockSpec((tm, tn), lambda i, j, k: (i, j)),
        out_shape=jax.ShapeDtypeStruct((m, n), out_dtype),
        scratch_shapes=[pltpu.VMEM((tm, tn), F32)],
        compiler_params=_params("parallel", "parallel", "arbitrary"),
        name="matmul_k",
    )(a, w)


def _glu_merge_kernel(a_ref, wv_ref, wg_ref, sb_ref, oa_ref, o_ref):
    a = a_ref[...]
    value = jnp.dot(a, wv_ref[...], preferred_element_type=F32)
    gate = jnp.dot(a, wg_ref[...], preferred_element_type=F32)
    o_b = value * jax.nn.sigmoid(gate)
    o_ref[...] = (oa_ref[...].astype(F32) + sb_ref[...].astype(F32) * o_b).astype(o_ref.dtype)


def _glu_merge(y, w_glu, gate_sig, oa_gated):
    m, kdim = y.shape
    d = w_glu.shape[1] // 2
    tm, tn = _tile(m, 1024), _tile(d, 512, 128)
    nj = d // tn
    return pl.pallas_call(
        _glu_merge_kernel,
        grid=(nj, m // tm),
        in_specs=[pl.BlockSpec((tm, kdim), lambda j, i: (i, 0)),
                  pl.BlockSpec((kdim, tn), lambda j, i: (0, j)),
                  pl.BlockSpec((kdim, tn), lambda j, i: (0, j + nj)),
                  pl.BlockSpec((tm, tn), lambda j, i: (i, j + nj)),
                  pl.BlockSpec((tm, tn), lambda j, i: (i, j))],
        out_specs=pl.BlockSpec((tm, tn), lambda j, i: (i, j)),
        out_shape=jax.ShapeDtypeStruct((m, d), BF16),
        compiler_params=_params("parallel", "parallel"),
        name="glu_merge",
    )(y, w_glu, w_glu, gate_sig, oa_gated)


def _qproj_kernel(a_ref, w_ref, *rest, scale, use_rope):
    if use_rope:
        cos_ref, sin_ref, o_ref = rest
    else:
        (o_ref,) = rest
    nb, hb, tl, _ = o_ref.shape
    a = a_ref[...]
    for h in range(hb):
        acc = jnp.dot(a, w_ref[h], preferred_element_type=F32)
        o_ref[:, h, :, :QK_NOPE] = (acc[:, :QK_NOPE] * scale).reshape(nb, tl, QK_NOPE).astype(o_ref.dtype)
        rope = acc[:, QK_NOPE:QK_NOPE + QK_ROPE]
        if use_rope:
            rope = rope * cos_ref[...] + acc[:, QK_NOPE + QK_ROPE:QK_NOPE + 2 * QK_ROPE] * sin_ref[...]
        o_ref[:, h, :, QK_NOPE:] = (rope * scale).reshape(nb, tl, QK_ROPE).astype(o_ref.dtype)


HEADS_PER_STEP = 4


def _row_blocking(batch, length, target):
    if length >= target:
        return 1, _tile(length, target)
    return _tile(batch, max(1, target // length), 1), length


def _head_out_spec(nb, hb, tl, length, width):
    per_batch = length // tl
    return pl.BlockSpec((nb, hb, tl, width), lambda i, h: (i // per_batch, h, i % per_batch, 0))


def _qproj(qn, w_heads, batch, length, rope_tables):
    n, kdim = qn.shape
    heads, _, cols = w_heads.shape
    use_rope = rope_tables is not None
    nb, tl = (1, _tile(length, 1024)) if use_rope else _row_blocking(batch, length, 1024)
    tm = nb * tl
    hb = _tile(heads, HEADS_PER_STEP, 1)
    in_specs = [pl.BlockSpec((tm, kdim), lambda i, h: (i, 0)),
                pl.BlockSpec((hb, kdim, cols), lambda i, h: (h, 0, 0))]
    args = [qn, w_heads]
    if use_rope:
        per_batch = length // tl
        spec = pl.BlockSpec((tl, QK_ROPE), lambda i, h: (i % per_batch, 0))
        in_specs += [spec, spec]
        args += list(rope_tables)
    width = QK_NOPE + QK_ROPE
    return pl.pallas_call(
        functools.partial(_qproj_kernel, scale=width ** -0.5, use_rope=use_rope),
        grid=(n // tm, heads // hb),
        in_specs=in_specs,
        out_specs=_head_out_spec(nb, hb, tl, length, width),
        out_shape=jax.ShapeDtypeStruct((batch, heads, length, width), BF16),
        compiler_params=_params("parallel", "parallel"),
        name="q_proj",
    )(*args)


def _kvprep_kernel(ck_ref, g_ref, *rest, kv_lora, use_rope):
    if use_rope:
        cos_ref, sin_ref, ckv_ref, kr_ref = rest
    else:
        ckv_ref, kr_ref = rest
    ckv_ref[...] = _rms(ck_ref[:, :kv_lora], g_ref[...])
    kr = ck_ref[:, kv_lora:kv_lora + QK_ROPE]
    if use_rope:
        kr = kr * cos_ref[...] + ck_ref[:, kv_lora + 128:kv_lora + 128 + QK_ROPE] * sin_ref[...]
    kr_ref[...] = kr


def _kvprep(ck, g_kv, length, rope_tables):
    n, cols = ck.shape
    kv_lora = cols - 256
    tm = _tile(length, 512)
    use_rope = rope_tables is not None
    in_specs = [pl.BlockSpec((tm, cols), lambda i: (i, 0)),
                pl.BlockSpec((1, kv_lora), lambda i: (0, 0))]
    args = [ck, g_kv]
    if use_rope:
        per_batch = length // tm
        spec = pl.BlockSpec((tm, QK_ROPE), lambda i: (i % per_batch, 0))
        in_specs += [spec, spec]
        args += list(rope_tables)
    return pl.pallas_call(
        functools.partial(_kvprep_kernel, kv_lora=kv_lora, use_rope=use_rope),
        grid=(n // tm,),
        in_specs=in_specs,
        out_specs=[pl.BlockSpec((tm, kv_lora), lambda i: (i, 0)),
                   pl.BlockSpec((tm, QK_ROPE), lambda i: (i, 0))],
        out_shape=[jax.ShapeDtypeStruct((n, kv_lora), F32),
                   jax.ShapeDtypeStruct((n, QK_ROPE), F32)],
        compiler_params=_params("parallel"),
        name="kv_prep",
    )(*args)


def _kvdec_kernel(c_ref, kr_ref, w_ref, k_ref, v_ref):
    nb, hb, tl, _ = k_ref.shape
    c = c_ref[...].astype(BF16)
    k_rope = kr_ref[...].reshape(nb, tl, QK_ROPE).astype(k_ref.dtype)
    for h in range(hb):
        acc = jnp.dot(c, w_ref[h], preferred_element_type=F32)
        k_ref[:, h, :, :QK_NOPE] = acc[:, :QK_NOPE].reshape(nb, tl, QK_NOPE).astype(k_ref.dtype)
        k_ref[:, h, :, QK_NOPE:] = k_rope
        v_ref[:, h] = acc[:, QK_NOPE:].reshape(nb, tl, V_HEAD).astype(v_ref.dtype)


def _kvdec(ckv, krope, w_heads, batch, length):
    n, kdim = ckv.shape
    heads, _, cols = w_heads.shape
    nb, tl = _row_blocking(batch, length, 1536)
    tm = nb * tl
    hb = _tile(heads, HEADS_PER_STEP, 1)
    return pl.pallas_call(
        _kvdec_kernel,
        grid=(n // tm, heads // hb),
        in_specs=[pl.BlockSpec((tm, kdim), lambda i, h: (i, 0)),
                  pl.BlockSpec((tm, QK_ROPE), lambda i, h: (i, 0)),
                  pl.BlockSpec((hb, kdim, cols), lambda i, h: (h, 0, 0))],
        out_specs=[_head_out_spec(nb, hb, tl, length, QK_NOPE + QK_ROPE),
                   _head_out_spec(nb, hb, tl, length, V_HEAD)],
        out_shape=[jax.ShapeDtypeStruct((batch, heads, length, QK_NOPE + QK_ROPE), BF16),
                   jax.ShapeDtypeStruct((batch, heads, length, V_HEAD), BF16)],
        compiler_params=_params("parallel", "parallel"),
        name="kv_decompress",
    )(ckv, krope, w_heads)


def _qk(q, k):
    return lax.dot_general(q, k, (((1,), (1,)), ((), ())), preferred_element_type=F32)


def _attn_kernel(q_ref, k_ref, v_ref, gate_ref, o_ref, *stat_refs, tk, nk, sub):
    hb, tq, dv = q_ref.shape[1], q_ref.shape[2], v_ref.shape[3]
    if nk > 1:
        m_ref, l_ref, acc_ref = stat_refs
    lanes = dv

    def write_out(h, rows, acc, l):
        cols = slice(h * dv, (h + 1) * dv)
        o_ref[0, rows, cols] = (acc / l * gate_ref[0, rows, cols].astype(F32)).astype(o_ref.dtype)

    for h in range(hb):
        for r in range(tq // sub):
            rows = pl.ds(r * sub, sub)
            q = q_ref[0, h, rows, :]
            s = _qk(q, k_ref[0, h, :tk, :])
            m = jnp.max(s, axis=-1, keepdims=True)
            p = jnp.exp(s - m)
            l = jnp.sum(p, axis=-1, keepdims=True)
            acc = jnp.dot(p.astype(BF16), v_ref[0, h, :tk, :], preferred_element_type=F32)
            if nk > 1:
                m_ref[h, rows, :] = jnp.broadcast_to(m, (sub, lanes))
                l_ref[h, rows, :] = jnp.broadcast_to(l, (sub, lanes))
                acc_ref[h, rows, :] = acc
            else:
                write_out(h, rows, acc, l)
        for j in range(1, nk):
            for r in range(tq // sub):
                rows = pl.ds(r * sub, sub)
                q = q_ref[0, h, rows, :]
                s = _qk(q, k_ref[0, h, j * tk:(j + 1) * tk, :])
                m_old = m_ref[h, rows, :]
                m_new = jnp.maximum(m_old, jnp.max(s, axis=-1, keepdims=True))
                alpha = jnp.exp(m_old - m_new)
                p = jnp.exp(s - m_new[:, :1])
                l = alpha * l_ref[h, rows, :] + jnp.sum(p, axis=-1, keepdims=True)
                acc = alpha * acc_ref[h, rows, :] + jnp.dot(
                    p.astype(BF16), v_ref[0, h, j * tk:(j + 1) * tk, :], preferred_element_type=F32)
                if j < nk - 1:
                    m_ref[h, rows, :] = m_new
                    l_ref[h, rows, :] = l
                    acc_ref[h, rows, :] = acc
                else:
                    write_out(h, rows, acc, l)


ATTN_MAX_KV_CHUNK = 2304
ATTN_STEP_ROWS = 2048


def _attention(q, k, v, gate):
    batch, heads, lq, dqk = q.shape
    lk, dv = k.shape[2], v.shape[3]
    tq = _tile(lq, 2048)
    sub = _tile(tq, 512)
    hb = _tile(heads, max(1, ATTN_STEP_ROWS // lq), 1) if tq == lq else 1
    tk = _tile(lk, ATTN_MAX_KV_CHUNK, 128) if lk > ATTN_MAX_KV_CHUNK else lk
    nk = lk // tk
    stats = [pltpu.VMEM((hb, tq, dv), F32)] * 3 if nk > 1 else []
    return pl.pallas_call(
        functools.partial(_attn_kernel, tk=tk, nk=nk, sub=sub),
        grid=(batch, heads // hb, lq // tq),
        in_specs=[pl.BlockSpec((1, hb, tq, dqk), lambda b, h, i: (b, h, i, 0)),
                  pl.BlockSpec((1, hb, lk, dqk), lambda b, h, i: (b, h, 0, 0)),
                  pl.BlockSpec((1, hb, lk, dv), lambda b, h, i: (b, h, 0, 0)),
                  pl.BlockSpec((1, tq, hb * dv), lambda b, h, i: (b, i, h))],
        out_specs=pl.BlockSpec((1, tq, hb * dv), lambda b, h, i: (b, i, h)),
        out_shape=jax.ShapeDtypeStruct((batch, lq, heads * dv), BF16),
        scratch_shapes=stats,
        compiler_params=_params("parallel", "parallel", "arbitrary"),
        name="attention",
    )(q, k, v, gate)


def _gelu_tanh(x):
    return 0.5 * x * (1.0 + jnp.tanh(math.sqrt(2.0 / math.pi) * (x + 0.044715 * (x * x * x))))


def _s5_kernel(u_ref, w_ref, g_ref, e_ref, d_ref, ar_ref, ai_ref, h0_ref, y_ref, fin_ref, *, n_chunks):
    u = u_ref[0]
    rows = u.shape[0]
    batch = rows // n_chunks
    half = g_ref.shape[2] // 2
    g = jnp.dot(u, g_ref[0], preferred_element_type=F32)
    h0 = jnp.broadcast_to(h0_ref[0][:, None, :], (batch, n_chunks, 2 * half)).reshape(rows, 2 * half)
    chunk = lax.broadcasted_iota(jnp.int32, (rows, half), 0) % n_chunks
    ar = ar_ref[0]
    ai = ai_ref[0]

    def cmul(x, step, lo):
        swapped = pltpu.roll(x, half // 2, axis=1)
        return x * ar[step:step + 1, lo:lo + half] + swapped * ai[step:step + 1, lo:lo + half]

    def scan(gd, h0d, lo, backward):
        if backward:
            z = jnp.where(chunk == n_chunks - 1, h0d, pltpu.roll(gd, rows - 1, axis=0))
        else:
            z = jnp.where(chunk == 0, h0d, pltpu.roll(gd, 1, axis=0))
        step, dist = 0, 1
        while dist < n_chunks:
            if backward:
                moved = jnp.where(chunk < n_chunks - dist, pltpu.roll(z, rows - dist, axis=0), 0.0)
            else:
                moved = jnp.where(chunk >= dist, pltpu.roll(z, dist, axis=0), 0.0)
            z = z + cmul(moved, step, lo)
            step, dist = step + 1, dist * 2
        return z

    gf, gb = g[:, :half], g[:, half:]
    pf = scan(gf, h0[:, :half], 0, False)
    pb = scan(gb, h0[:, half:], half, True)
    p = jnp.concatenate([pf, pb], axis=1).astype(BF16)
    y = (jnp.dot(u, w_ref[0], preferred_element_type=F32)
         + jnp.dot(p, e_ref[0], preferred_element_type=F32)
         + u.astype(F32) * d_ref[0])
    y_ref[0] = _gelu_tanh(y).astype(y_ref.dtype)
    def pick(x, which):
        return jnp.sum(jnp.where(chunk == which, x, 0.0).reshape(batch, n_chunks, half), axis=1)

    ff = pick(cmul(pf, 0, 0) + gf, n_chunks - 1)
    fb = pick(cmul(pb, 0, half) + gb, 0)
    fin_ref[0] = jnp.concatenate([ff, fb], axis=1)


def _s5(u_rows, ops, h0, n_chunks):
    groups, rows, width = u_rows.shape
    batch = rows // n_chunks
    wt, gt, et, dvec, ar, ai = ops
    lanes = gt.shape[2]

    def per_group(*shape):
        return pl.BlockSpec((1,) + shape, lambda g: (g,) + (0,) * len(shape))

    return pl.pallas_call(
        functools.partial(_s5_kernel, n_chunks=n_chunks),
        grid=(groups,),
        in_specs=[per_group(rows, width), per_group(width, width), per_group(width, lanes),
                  per_group(lanes, width), per_group(1, width), per_group(*ar.shape[1:]),
                  per_group(*ai.shape[1:]), per_group(batch, lanes)],
        out_specs=[per_group(rows, width), per_group(batch, lanes)],
        out_shape=[jax.ShapeDtypeStruct((groups, rows, width), BF16),
                   jax.ShapeDtypeStruct((groups, batch, lanes), F32)],
        compiler_params=_params("parallel"),
        name="s5",
    )(u_rows, wt, gt, et, dvec, ar, ai, h0)


def _s5_operators(lam_re, lam_im, log_dt, b_re, b_im, c_re, c_im, d_skip, chunk, max_chunks):
    hi = lax.Precision.HIGHEST
    groups, p_dim, j_dim = b_re.shape[1:]
    t = chunk
    lam = lax.complex(lam_re, lam_im)
    z = lam * jnp.exp(log_dt)[..., None]
    a = jnp.exp(z)
    b_bar = ((a - 1.0) / lam)[..., None] * lax.complex(b_re, b_im)
    c_mat = lax.complex(c_re, c_im)
    k_idx = jnp.arange(t + 1, dtype=F32)
    apow = jnp.exp(z[None] * k_idx[:, None, None, None])

    cb = c_mat[..., None] * b_bar[:, :, None]
    ap = jnp.moveaxis(apow[:t], 1, 0)
    kern = (jnp.einsum('dtgp,dgopi->dtgoi', ap.real, cb.real, precision=hi)
            - jnp.einsum('dtgp,dgopi->dtgoi', ap.imag, cb.imag, precision=hi))
    i_idx = jnp.arange(t)
    zeros = jnp.zeros((t - 1,) + kern.shape[2:], F32)
    k_lag = jnp.concatenate([zeros, kern[0]], axis=0) + jnp.concatenate([kern[1][::-1], zeros], axis=0)
    k_lag = k_lag.transpose(1, 3, 0, 2).reshape(groups, j_dim, (2 * t - 1) * j_dim).astype(BF16)
    w_all = jnp.stack([k_lag[:, :, (t - 1 - j) * j_dim:(2 * t - 1 - j) * j_dim] for j in range(t)], axis=1)
    w_all = w_all.reshape(groups, t * j_dim, t * j_dim)

    inj_f = apow[t - 1 - i_idx, 0][..., None] * b_bar[0][None]
    inj_b = apow[i_idx, 1][..., None] * b_bar[1][None]
    g_all = jnp.concatenate([inj_f.real, inj_f.imag, inj_b.real, inj_b.imag], axis=2)
    g_all = g_all.transpose(1, 0, 3, 2).reshape(groups, t * j_dim, 4 * p_dim)

    out_f = c_mat[0][None] * apow[i_idx + 1, 0][:, :, None, :]
    out_b = c_mat[1][None] * apow[t - i_idx, 1][:, :, None, :]
    e_all = jnp.concatenate([out_f.real, -out_f.imag, out_b.real, -out_b.imag], axis=3)
    e_all = e_all.transpose(1, 3, 0, 2).reshape(groups, 4 * p_dim, t * j_dim)

    d_row = jnp.tile(d_skip.reshape(groups, 1, j_dim), (1, t, 1)).reshape(groups, 1, t * j_dim)

    n_steps = max(1, (max_chunks - 1).bit_length())
    steps = (t * 2 ** jnp.arange(n_steps)).astype(F32)
    tp = jnp.exp(z[None] * steps[:, None, None, None])
    a_r = jnp.concatenate([tp[:, 0].real, tp[:, 0].real, tp[:, 1].real, tp[:, 1].real], axis=-1)
    a_i = jnp.concatenate([-tp[:, 0].imag, tp[:, 0].imag, -tp[:, 1].imag, tp[:, 1].imag], axis=-1)
    a_r = a_r.transpose(1, 0, 2)
    a_i = a_i.transpose(1, 0, 2)
    return w_all, g_all.astype(BF16), e_all.astype(BF16), d_row, a_r, a_i


def _postmix_kernel(x_ref, y_ref, gpost_ref, gpre_ref, mod_ref, x1_ref, h_ref):
    mod = mod_ref[0]
    x1 = x_ref[...] + mod[2:3] * _rms(y_ref[...].astype(F32), gpost_ref[...])
    x1_ref[...] = x1
    h_ref[...] = (_rms(x1, gpre_ref[...]) * (1.0 + mod[4:5]) + mod[3:4]).astype(h_ref.dtype)


def _postmix(x, y, g_post, g_pre, mod):
    n, d = x.shape
    tm = _tile(n // mod.shape[0], 256)
    row = pl.BlockSpec((tm, d), lambda i: (i, 0))
    gain = pl.BlockSpec((1, d), lambda i: (0, 0))
    return pl.pallas_call(
        _postmix_kernel,
        grid=(n // tm,),
        in_specs=[row, row, gain, gain, _mod_spec(mod, n, tm)],
        out_specs=[row, row],
        out_shape=[jax.ShapeDtypeStruct((n, d), F32), jax.ShapeDtypeStruct((n, d), BF16)],
        compiler_params=_params("parallel"),
        name="post_mix",
    )(x, y, g_post, g_pre, mod)


def _postmlp_kernel(x_ref, y_ref, g_ref, mod_ref, o_ref):
    o_ref[...] = x_ref[...] + mod_ref[0][5:6] * _rms(y_ref[...].astype(F32), g_ref[...])


def _postmlp(x, y, g_post, mod):
    n, d = x.shape
    tm = _tile(n // mod.shape[0], 256)
    row = pl.BlockSpec((tm, d), lambda i: (i, 0))
    return pl.pallas_call(
        _postmlp_kernel,
        grid=(n // tm,),
        in_specs=[row, row, pl.BlockSpec((1, d), lambda i: (0, 0)), _mod_spec(mod, n, tm)],
        out_specs=row,
        out_shape=jax.ShapeDtypeStruct((n, d), F32),
        compiler_params=_params("parallel"),
        name="post_mlp",
    )(x, y, g_post, mod)


def _axial_rope(n_tokens):
    rows = n_tokens // GRID_W
    row = jnp.repeat(jnp.arange(rows, dtype=F32), GRID_W)
    col = jnp.tile(jnp.arange(GRID_W, dtype=F32), rows)
    n_freq = QK_ROPE // 4
    inv_freq = ROPE_THETA ** (-jnp.arange(n_freq, dtype=F32) / n_freq)
    ang = jnp.concatenate([row[:, None] * inv_freq, col[:, None] * inv_freq], axis=-1)
    ang = jnp.concatenate([ang, ang], axis=-1)
    return jnp.cos(ang), jnp.sin(ang)


def _rotate_half_cols(w):
    half = w.shape[-1] // 2
    return jnp.concatenate([-w[..., half:], w[..., :half]], axis=-1)


def _prepare_weights(w_in, w_uq, w_ukv, w_glu, w_out, w_ff1, w_ff2, q_lora, kv_lora, s5_width, d_model):
    o1, o2, o3 = q_lora, q_lora + kv_lora, q_lora + kv_lora + QK_ROPE
    o4 = o3 + s5_width
    w_kr = w_in[:, o2:o3]
    pad = jnp.zeros((w_in.shape[0], 128 - QK_ROPE), w_in.dtype)
    w_ck = jnp.concatenate([w_in[:, o1:o2], w_kr, pad, _rotate_half_cols(w_kr), pad], axis=1)
    uq = w_uq.reshape(q_lora, N_HEADS, QK_NOPE + QK_ROPE)
    uq_plain = uq.transpose(1, 0, 2)
    uq_rope = jnp.concatenate([uq, _rotate_half_cols(uq[..., QK_NOPE:])], axis=-1).transpose(1, 0, 2)
    ukv = w_ukv.reshape(kv_lora, N_HEADS, QK_NOPE + V_HEAD).transpose(1, 0, 2)
    cast = lambda w: w.astype(BF16)
    return dict(w_q=cast(w_in[:, :o1]), w_ck=cast(w_ck), w_u=cast(w_in[:, o3:o4]), w_g=cast(w_in[:, o4:]),
                uq_plain=cast(uq_plain), uq_rope=cast(uq_rope), ukv=cast(ukv), w_glu=cast(w_glu),
                w_out=cast(w_out), w_ff1=cast(w_ff1), w_ff2=cast(w_ff2))


def _to_group_rows(u, batch, length, groups, j_dim, chunk):
    n_chunks = length // chunk
    u = u.reshape(batch, n_chunks, chunk, groups, j_dim).transpose(3, 0, 1, 2, 4)
    return u.reshape(groups, batch * n_chunks, chunk * j_dim)


def _from_group_rows(y, batch, length, groups, j_dim, chunk):
    n_chunks = length // chunk
    y = y.reshape(groups, batch, n_chunks, chunk, j_dim).transpose(1, 2, 3, 0, 4)
    return y.reshape(batch * length, groups * j_dim)


def _trunk(x, mod, wts, gains, s5_ops, s5_dims, ctx):
    batch, length, d_model = x.shape
    n = batch * length
    groups, p_dim, j_dim, chunk = s5_dims
    x2 = x.reshape(n, d_model)
    rope = None if ctx is None else ctx[3]

    h = _modnorm(x2, gains['g_pre_mix'], mod, 0, 1)
    q_lat = _mm(h, wts['w_q'], F32)
    ck = _mm(h, wts['w_ck'], F32)
    u = _mm(h, wts['w_u'], BF16)
    gate_sig = _mm(h, wts['w_g'], BF16, epilogue="sigmoid")

    qn = _rmsnorm(q_lat, gains['g_q_lat'])
    q = _qproj(qn, wts['uq_plain'] if rope is None else wts['uq_rope'], batch, length, rope)
    ckv_n, k_rope = _kvprep(ck, gains['g_kv_lat'], length, rope)
    if ctx is None:
        ckv_all, kr_all, lk = ckv_n, k_rope, length
    else:
        lk = ctx[0].shape[1] + length
        ckv_all = jnp.concatenate([ctx[0], ckv_n.reshape(batch, length, -1)], axis=1).reshape(batch * lk, -1)
        kr_all = jnp.concatenate([ctx[1], k_rope.reshape(batch, length, -1)], axis=1).reshape(batch * lk, -1)
    k, v = _kvdec(ckv_all, kr_all, wts['ukv'], batch, lk)
    oa_gated = _attention(q, k, v, gate_sig.reshape(batch, length, -1)).reshape(n, -1)

    if ctx is None:
        h0 = jnp.zeros((groups, batch, 4 * p_dim), F32)
    else:
        h0 = ctx[2].transpose(3, 0, 1, 2, 4).reshape(groups, batch, 4 * p_dim)
    u_rows = _to_group_rows(u, batch, length, groups, j_dim, chunk)
    y_rows, s5_fin = _s5(u_rows, s5_ops, h0, length // chunk)
    y = _from_group_rows(y_rows, batch, length, groups, j_dim, chunk)
    m = _glu_merge(y, wts['w_glu'], gate_sig, oa_gated)
    mixed = _mm(m, wts['w_out'], BF16)
    x1, h2 = _postmix(x2, mixed, gains['g_post_mix'], gains['g_pre_mlp'], mod)
    a1 = _mm(h2, wts['w_ff1'], BF16, epilogue="relu2")
    a2 = _mm(a1, wts['w_ff2'], BF16)
    y_out = _postmlp(x1, a2, gains['g_post_mlp'], mod).reshape(batch, length, d_model)
    s5_fin = s5_fin.reshape(groups, batch, 2, 2, p_dim).transpose(1, 2, 3, 0, 4)
    return y_out, ckv_n.reshape(batch, length, -1), k_rope.reshape(batch, length, -1), s5_fin


def kernel(x_prompt, x_sample, c, cache_ckv, cache_krope, state_s5, c_ctx, w_mod, b_mod, g_pre_mix, w_in, g_q_lat, g_kv_lat, w_uq, w_ukv, s5_lam_re, s5_lam_im, s5_log_dt, s5_b_re, s5_b_im, s5_c_re, s5_c_im, s5_d, w_glu, w_out, g_post_mix, g_pre_mlp, w_ff1, w_ff2, g_post_mlp):
    depth = w_mod.shape[0]
    d_model = x_prompt.shape[-1]
    dec_batch, dec_seq = x_sample.shape[:2]
    q_lora, kv_lora = g_q_lat.shape[1], g_kv_lat.shape[1]
    groups, p_dim, j_dim = s5_b_re.shape[2:]
    s5_width = groups * j_dim
    chunk = min(S5_CHUNK, x_prompt.shape[1] // 8)
    rope = _axial_rope(dec_seq)

    cond = jnp.concatenate([c_ctx[None, :], c], axis=0)
    cond = jnp.pad(cond, ((0, -cond.shape[0] % 8), (0, 0)))

    xp, xs = x_prompt, x_sample
    new_ckv, new_krope, new_s5 = [], [], []
    for l in range(depth):
        mods = _adaln(cond, w_mod[l], b_mod[l][None, :])
        mod_p = mods[0:1].reshape(1, -1, d_model)
        mod_s = mods[1:1 + dec_batch].reshape(dec_batch, -1, d_model)
        wts = _prepare_weights(w_in[l], w_uq[l], w_ukv[l], w_glu[l], w_out[l], w_ff1[l], w_ff2[l],
                               q_lora, kv_lora, s5_width, d_model)
        gains = dict(g_pre_mix=g_pre_mix[l][None], g_q_lat=g_q_lat[l][None], g_kv_lat=g_kv_lat[l][None],
                     g_post_mix=g_post_mix[l][None], g_pre_mlp=g_pre_mlp[l][None], g_post_mlp=g_post_mlp[l][None])
        max_chunks = max(x_prompt.shape[1], dec_seq) // chunk
        s5_ops = _s5_operators(s5_lam_re[l], s5_lam_im[l], s5_log_dt[l], s5_b_re[l], s5_b_im[l],
                               s5_c_re[l], s5_c_im[l], s5_d[l], chunk, max_chunks)
        s5_dims = (groups, p_dim, j_dim, chunk)

        xp, ckv_n, k_rope_p, s5_last = _trunk(xp, mod_p, wts, gains, s5_ops, s5_dims, None)
        new_ckv.append(ckv_n)
        new_krope.append(k_rope_p)
        new_s5.append(s5_last)
        ctx = (cache_ckv[:, l], cache_krope[:, l], state_s5[:, l], rope)
        xs, _, _, _ = _trunk(xs, mod_s, wts, gains, s5_ops, s5_dims, ctx)
    return (xp, xs, jnp.stack(new_ckv, axis=1), jnp.stack(new_krope, axis=1),
            jnp.stack(new_s5, axis=1).astype(x_prompt.dtype))
```

```python
import functools
import math

import jax
import jax.numpy as jnp
import numpy as np
from jax import lax
from jax.experimental import pallas as pl
from jax.experimental.pallas import tpu as pltpu

N_HEADS = 32
QK_NOPE = 128
QK_ROPE = 64
V_HEAD = 128
GRID_W = 64
ROPE_THETA = 10000.0
EPS = 1e-6
S5_CHUNK = 32
VMEM_LIMIT_BYTES = 48 * 1024 * 1024

F32 = jnp.float32
BF16 = jnp.bfloat16


def _params(*semantics):
    return pltpu.CompilerParams(dimension_semantics=semantics, vmem_limit_bytes=VMEM_LIMIT_BYTES)


def _tile(n, target, align=8):
    for d in range(min(n, target), 0, -1):
        if n % d == 0 and d % align == 0:
            return d
    return n


def _adaln_kernel(c_ref, w_ref, b_ref, o_ref):
    c = c_ref[...]
    s = (c * jax.nn.sigmoid(c)).astype(BF16)
    o_ref[...] = jnp.dot(s, w_ref[...].astype(BF16), preferred_element_type=F32) + b_ref[...]


def _adaln(cond, w_mod, b_mod):
    rows, d = cond.shape
    n = w_mod.shape[1]
    tn = _tile(n, 512, 128)
    return pl.pallas_call(
        _adaln_kernel,
        grid=(n // tn,),
        in_specs=[pl.BlockSpec((rows, d), lambda j: (0, 0)),
                  pl.BlockSpec((d, tn), lambda j: (0, j)),
                  pl.BlockSpec((1, tn), lambda j: (0, j))],
        out_specs=pl.BlockSpec((rows, tn), lambda j: (0, j)),
        out_shape=jax.ShapeDtypeStruct((rows, n), F32),
        compiler_params=_params("parallel"),
        name="adaln",
    )(cond, w_mod, b_mod)


def _rms(x, g):
    return x * lax.rsqrt(jnp.mean(x * x, axis=-1, keepdims=True) + EPS) * g


def _modnorm_kernel(x_ref, g_ref, mod_ref, o_ref, *, shift_idx, scale_idx):
    mod = mod_ref[0]
    y = _rms(x_ref[...], g_ref[...])
    o_ref[...] = (y * (1.0 + mod[scale_idx:scale_idx + 1]) + mod[shift_idx:shift_idx + 1]).astype(o_ref.dtype)


def _mod_spec(mod, n_rows, tm):
    rows_per_mod = n_rows // mod.shape[0]
    return pl.BlockSpec((1,) + mod.shape[1:], lambda i: ((i * tm) // rows_per_mod, 0, 0))


def _modnorm(x, g, mod, shift_idx, scale_idx):
    n, d = x.shape
    tm = _tile(n // mod.shape[0], 256)
    return pl.pallas_call(
        functools.partial(_modnorm_kernel, shift_idx=shift_idx, scale_idx=scale_idx),
        grid=(n // tm,),
        in_specs=[pl.BlockSpec((tm, d), lambda i: (i, 0)),
                  pl.BlockSpec((1, d), lambda i: (0, 0)),
                  _mod_spec(mod, n, tm)],
        out_specs=pl.BlockSpec((tm, d), lambda i: (i, 0)),
        out_shape=jax.ShapeDtypeStruct((n, d), BF16),
        compiler_params=_params("parallel"),
        name="modnorm",
    )(x, g, mod)


def _rmsnorm_kernel(x_ref, g_ref, o_ref):
    o_ref[...] = _rms(x_ref[...], g_ref[...]).astype(o_ref.dtype)


def _rmsnorm(x, g):
    n, d = x.shape
    tm = _tile(n, 1024)
    return pl.pallas_call(
        _rmsnorm_kernel,
        grid=(n // tm,),
        in_specs=[pl.BlockSpec((tm, d), lambda i: (i, 0)),
                  pl.BlockSpec((1, d), lambda i: (0, 0))],
        out_specs=pl.BlockSpec((tm, d), lambda i: (i, 0)),
        out_shape=jax.ShapeDtypeStruct((n, d), BF16),
        compiler_params=_params("parallel"),
        name="rmsnorm",
    )(x, g)


def _epilogue(r, kind):
    if kind == "relu2":
        return jnp.square(jnp.maximum(r, 0.0))
    if kind == "sigmoid":
        return jax.nn.sigmoid(r)
    assert kind is None
    return r


def _bf16_panel(w_ref, scratch_ref):
    if scratch_ref is None:
        return w_ref[...]

    @pl.when(pl.program_id(1) == 0)
    def _():
        scratch_ref[...] = w_ref[...].astype(BF16)

    return scratch_ref[...]


def _mm_full_k_kernel(a_ref, w_ref, o_ref, wb_ref=None, *, epilogue):
    w = _bf16_panel(w_ref, wb_ref)
    r = jnp.dot(a_ref[...], w, preferred_element_type=F32)
    o_ref[...] = _epilogue(r, epilogue).astype(o_ref.dtype)


def _mm_kernel(a_ref, w_ref, o_ref, acc_ref, *, epilogue):
    k = pl.program_id(2)

    @pl.when(k == 0)
    def _():
        acc_ref[...] = jnp.zeros_like(acc_ref)

    acc_ref[...] += jnp.dot(a_ref[...], w_ref[...], preferred_element_type=F32)

    @pl.when(k == pl.num_programs(2) - 1)
    def _():
        o_ref[...] = _epilogue(acc_ref[...], epilogue).astype(o_ref.dtype)


MM_MAX_FULL_K = 4096


def _mm(a, w, out_dtype, epilogue=None):
    m, kdim = a.shape
    n = w.shape[1]
    if kdim <= MM_MAX_FULL_K:
        cast_w = w.dtype != BF16
        if cast_w:
            tm, tn = _tile(m, 1024), _tile(n, 512, 128)
        else:
            tm, tn = _tile(m, 512 if kdim > 2048 else 1024), _tile(n, 1024, 128)
        return pl.pallas_call(
            functools.partial(_mm_full_k_kernel, epilogue=epilogue),
            grid=(n // tn, m // tm),
            in_specs=[pl.BlockSpec((tm, kdim), lambda j, i: (i, 0)),
                      pl.BlockSpec((kdim, tn), lambda j, i: (0, j))],
            out_specs=pl.BlockSpec((tm, tn), lambda j, i: (i, j)),
            out_shape=jax.ShapeDtypeStruct((m, n), out_dtype),
            scratch_shapes=[pltpu.VMEM((kdim, tn), BF16)] if cast_w else [],
            compiler_params=_params("parallel", "arbitrary" if cast_w else "parallel"),
            name="matmul",
        )(a, w)
    assert w.dtype == BF16
    tm, tn, tk = _tile(m, 1024), _tile(n, 1024, 128), _tile(kdim, MM_MAX_FULL_K, 128)
    return pl.pallas_call(
        functools.partial(_mm_kernel, epilogue=epilogue),
        grid=(m // tm, n // tn, kdim // tk),
        in_specs=[pl.BlockSpec((tm, tk), lambda i, j, k: (i, k)),
                  pl.BlockSpec((tk, tn), lambda i, j, k: (k, j))],
        out_specs=pl.BlockSpec((tm, tn), lambda i, j, k: (i, j)),
        out_shape=jax.ShapeDtypeStruct((m, n), out_dtype),
        scratch_shapes=[pltpu.VMEM((tm, tn), F32)],
        compiler_params=_params("parallel", "parallel", "arbitrary"),
        name="matmul_k",
    )(a, w)


def _glu_merge_kernel(a_ref, wv_ref, wg_ref, sb_ref, oa_ref, o_ref, wvb_ref, wgb_ref):
    w_value, w_gate = _bf16_panel(wv_ref, wvb_ref), _bf16_panel(wg_ref, wgb_ref)
    a = a_ref[...].astype(BF16)
    value = jnp.dot(a, w_value, preferred_element_type=F32)
    gate = jnp.dot(a, w_gate, preferred_element_type=F32)
    o_b = value * jax.nn.sigmoid(gate)
    o_ref[...] = (oa_ref[...].astype(F32) + sb_ref[...].astype(F32) * o_b).astype(o_ref.dtype)


def _glu_merge(y, w_glu, gate_sig, oa_gated):
    m, kdim = y.shape
    d = w_glu.shape[1] // 2
    tm, tn = _tile(m, 1024), _tile(d, 512, 128)
    nj = d // tn
    return pl.pallas_call(
        _glu_merge_kernel,
        grid=(nj, m // tm),
        in_specs=[pl.BlockSpec((tm, kdim), lambda j, i: (i, 0)),
                  pl.BlockSpec((kdim, tn), lambda j, i: (0, j)),
                  pl.BlockSpec((kdim, tn), lambda j, i: (0, j + nj)),
                  pl.BlockSpec((tm, tn), lambda j, i: (i, j + nj)),
                  pl.BlockSpec((tm, tn), lambda j, i: (i, j))],
        out_specs=pl.BlockSpec((tm, tn), lambda j, i: (i, j)),
        out_shape=jax.ShapeDtypeStruct((m, d), BF16),
        scratch_shapes=[pltpu.VMEM((kdim, tn), BF16)] * 2,
        compiler_params=_params("parallel", "arbitrary"),
        name="glu_merge",
    )(y, w_glu, w_glu, gate_sig, oa_gated)


def _qproj_kernel(a_ref, w_ref, *rest, scale, use_rope):
    if use_rope:
        cos_ref, sin_ref, o_ref = rest
    else:
        (o_ref,) = rest
    nb, hb, tl, _ = o_ref.shape
    a = a_ref[...]
    for h in range(hb):
        acc = jnp.dot(a, w_ref[h], preferred_element_type=F32)
        o_ref[:, h, :, :QK_NOPE] = (acc[:, :QK_NOPE] * scale).reshape(nb, tl, QK_NOPE).astype(o_ref.dtype)
        rope = acc[:, QK_NOPE:QK_NOPE + QK_ROPE]
        if use_rope:
            rope = rope * cos_ref[...] + acc[:, QK_NOPE + QK_ROPE:QK_NOPE + 2 * QK_ROPE] * sin_ref[...]
        o_ref[:, h, :, QK_NOPE:] = (rope * scale).reshape(nb, tl, QK_ROPE).astype(o_ref.dtype)


HEADS_PER_STEP = 4


def _row_blocking(batch, length, target):
    if length >= target:
        return 1, _tile(length, target)
    return _tile(batch, max(1, target // length), 1), length


def _head_out_spec(nb, hb, tl, length, width):
    per_batch = length // tl
    return pl.BlockSpec((nb, hb, tl, width), lambda i, h: (i // per_batch, h, i % per_batch, 0))


def _qproj(qn, w_heads, batch, length, rope_tables):
    n, kdim = qn.shape
    heads, _, cols = w_heads.shape
    use_rope = rope_tables is not None
    nb, tl = (1, _tile(length, 1024)) if use_rope else _row_blocking(batch, length, 1024)
    tm = nb * tl
    hb = _tile(heads, HEADS_PER_STEP, 1)
    in_specs = [pl.BlockSpec((tm, kdim), lambda i, h: (i, 0)),
                pl.BlockSpec((hb, kdim, cols), lambda i, h: (h, 0, 0))]
    args = [qn, w_heads]
    if use_rope:
        per_batch = length // tl
        spec = pl.BlockSpec((tl, QK_ROPE), lambda i, h: (i % per_batch, 0))
        in_specs += [spec, spec]
        args += list(rope_tables)
    width = QK_NOPE + QK_ROPE
    return pl.pallas_call(
        functools.partial(_qproj_kernel, scale=width ** -0.5, use_rope=use_rope),
        grid=(n // tm, heads // hb),
        in_specs=in_specs,
        out_specs=_head_out_spec(nb, hb, tl, length, width),
        out_shape=jax.ShapeDtypeStruct((batch, heads, length, width), BF16),
        compiler_params=_params("parallel", "parallel"),
        name="q_proj",
    )(*args)


def _kvprep_kernel(ck_ref, g_ref, *rest, kv_lora, use_rope):
    if use_rope:
        cos_ref, sin_ref, ckv_ref, kr_ref = rest
    else:
        ckv_ref, kr_ref = rest
    ckv_ref[...] = _rms(ck_ref[:, :kv_lora], g_ref[...])
    kr = ck_ref[:, kv_lora:kv_lora + QK_ROPE]
    if use_rope:
        kr = kr * cos_ref[...] + ck_ref[:, kv_lora + 128:kv_lora + 128 + QK_ROPE] * sin_ref[...]
    kr_ref[...] = kr


def _kvprep(ck, g_kv, length, rope_tables):
    n, cols = ck.shape
    kv_lora = cols - 256
    tm = _tile(length, 512)
    use_rope = rope_tables is not None
    in_specs = [pl.BlockSpec((tm, cols), lambda i: (i, 0)),
                pl.BlockSpec((1, kv_lora), lambda i: (0, 0))]
    args = [ck, g_kv]
    if use_rope:
        per_batch = length // tm
        spec = pl.BlockSpec((tm, QK_ROPE), lambda i: (i % per_batch, 0))
        in_specs += [spec, spec]
        args += list(rope_tables)
    return pl.pallas_call(
        functools.partial(_kvprep_kernel, kv_lora=kv_lora, use_rope=use_rope),
        grid=(n // tm,),
        in_specs=in_specs,
        out_specs=[pl.BlockSpec((tm, kv_lora), lambda i: (i, 0)),
                   pl.BlockSpec((tm, QK_ROPE), lambda i: (i, 0))],
        out_shape=[jax.ShapeDtypeStruct((n, kv_lora), F32),
                   jax.ShapeDtypeStruct((n, QK_ROPE), F32)],
        compiler_params=_params("parallel"),
        name="kv_prep",
    )(*args)


def _kvdec_kernel(c_ref, kr_ref, w_ref, k_ref, v_ref):
    nb, hb, tl, _ = k_ref.shape
    c = c_ref[...].astype(BF16)
    k_rope = kr_ref[...].reshape(nb, tl, QK_ROPE).astype(k_ref.dtype)
    for h in range(hb):
        acc = jnp.dot(c, w_ref[h], preferred_element_type=F32)
        k_ref[:, h, :, :QK_NOPE] = acc[:, :QK_NOPE].reshape(nb, tl, QK_NOPE).astype(k_ref.dtype)
        k_ref[:, h, :, QK_NOPE:] = k_rope
        v_ref[:, h] = acc[:, QK_NOPE:].reshape(nb, tl, V_HEAD).astype(v_ref.dtype)


def _kvdec(ckv, krope, w_heads, batch, length):
    n, kdim = ckv.shape
    heads, _, cols = w_heads.shape
    nb, tl = _row_blocking(batch, length, 1536)
    tm = nb * tl
    hb = _tile(heads, HEADS_PER_STEP, 1)
    return pl.pallas_call(
        _kvdec_kernel,
        grid=(n // tm, heads // hb),
        in_specs=[pl.BlockSpec((tm, kdim), lambda i, h: (i, 0)),
                  pl.BlockSpec((tm, QK_ROPE), lambda i, h: (i, 0)),
                  pl.BlockSpec((hb, kdim, cols), lambda i, h: (h, 0, 0))],
        out_specs=[_head_out_spec(nb, hb, tl, length, QK_NOPE + QK_ROPE),
                   _head_out_spec(nb, hb, tl, length, V_HEAD)],
        out_shape=[jax.ShapeDtypeStruct((batch, heads, length, QK_NOPE + QK_ROPE), BF16),
                   jax.ShapeDtypeStruct((batch, heads, length, V_HEAD), BF16)],
        compiler_params=_params("parallel", "parallel"),
        name="kv_decompress",
    )(ckv, krope, w_heads)


def _qk(q, k):
    return lax.dot_general(q, k, (((1,), (1,)), ((), ())), preferred_element_type=F32)


def _attn_kernel(q_ref, k_ref, v_ref, gate_ref, o_ref, *stat_refs, tk, nk, sub):
    hb, tq, dv = q_ref.shape[1], q_ref.shape[2], v_ref.shape[3]
    if nk > 1:
        m_ref, l_ref, acc_ref = stat_refs
    lanes = dv

    def write_out(h, rows, acc, l):
        cols = slice(h * dv, (h + 1) * dv)
        o_ref[0, rows, cols] = (acc / l * gate_ref[0, rows, cols].astype(F32)).astype(o_ref.dtype)

    for h in range(hb):
        for r in range(tq // sub):
            rows = pl.ds(r * sub, sub)
            q = q_ref[0, h, rows, :]
            s = _qk(q, k_ref[0, h, :tk, :])
            m = jnp.max(s, axis=-1, keepdims=True)
            p = jnp.exp(s - m)
            l = jnp.sum(p, axis=-1, keepdims=True)
            acc = jnp.dot(p.astype(BF16), v_ref[0, h, :tk, :], preferred_element_type=F32)
            if nk > 1:
                m_ref[h, rows, :] = jnp.broadcast_to(m, (sub, lanes))
                l_ref[h, rows, :] = jnp.broadcast_to(l, (sub, lanes))
                acc_ref[h, rows, :] = acc
            else:
                write_out(h, rows, acc, l)
        for j in range(1, nk):
            for r in range(tq // sub):
                rows = pl.ds(r * sub, sub)
                q = q_ref[0, h, rows, :]
                s = _qk(q, k_ref[0, h, j * tk:(j + 1) * tk, :])
                m_old = m_ref[h, rows, :]
                m_new = jnp.maximum(m_old, jnp.max(s, axis=-1, keepdims=True))
                alpha = jnp.exp(m_old - m_new)
                p = jnp.exp(s - m_new[:, :1])
                l = alpha * l_ref[h, rows, :] + jnp.sum(p, axis=-1, keepdims=True)
                acc = alpha * acc_ref[h, rows, :] + jnp.dot(
                    p.astype(BF16), v_ref[0, h, j * tk:(j + 1) * tk, :], preferred_element_type=F32)
                if j < nk - 1:
                    m_ref[h, rows, :] = m_new
                    l_ref[h, rows, :] = l
                    acc_ref[h, rows, :] = acc
                else:
                    write_out(h, rows, acc, l)


ATTN_MAX_KV_CHUNK = 2304
ATTN_STEP_ROWS = 2048


def _attention(q, k, v, gate):
    batch, heads, lq, dqk = q.shape
    lk, dv = k.shape[2], v.shape[3]
    tq = _tile(lq, 2048)
    sub = _tile(tq, 512)
    hb = _tile(heads, max(1, ATTN_STEP_ROWS // lq), 1) if tq == lq else 1
    tk = _tile(lk, ATTN_MAX_KV_CHUNK, 128) if lk > ATTN_MAX_KV_CHUNK else lk
    nk = lk // tk
    stats = [pltpu.VMEM((hb, tq, dv), F32)] * 3 if nk > 1 else []
    return pl.pallas_call(
        functools.partial(_attn_kernel, tk=tk, nk=nk, sub=sub),
        grid=(batch, heads // hb, lq // tq),
        in_specs=[pl.BlockSpec((1, hb, tq, dqk), lambda b, h, i: (b, h, i, 0)),
                  pl.BlockSpec((1, hb, lk, dqk), lambda b, h, i: (b, h, 0, 0)),
                  pl.BlockSpec((1, hb, lk, dv), lambda b, h, i: (b, h, 0, 0)),
                  pl.BlockSpec((1, tq, hb * dv), lambda b, h, i: (b, i, h))],
        out_specs=pl.BlockSpec((1, tq, hb * dv), lambda b, h, i: (b, i, h)),
        out_shape=jax.ShapeDtypeStruct((batch, lq, heads * dv), BF16),
        scratch_shapes=stats,
        compiler_params=_params("parallel", "parallel", "arbitrary"),
        name="attention",
    )(q, k, v, gate)


def _gelu_tanh(x):
    return 0.5 * x * (1.0 + jnp.tanh(math.sqrt(2.0 / math.pi) * (x + 0.044715 * (x * x * x))))


def _s5_kernel(u_ref, w_ref, g_ref, e_ref, d_ref, ar_ref, ai_ref, h0_ref, y_ref, fin_ref, *, n_chunks):
    u = u_ref[0]
    rows = u.shape[0]
    batch = rows // n_chunks
    half = g_ref.shape[2] // 2
    g = jnp.dot(u, g_ref[0], preferred_element_type=F32)
    h0 = jnp.broadcast_to(h0_ref[0][:, None, :], (batch, n_chunks, 2 * half)).reshape(rows, 2 * half)
    chunk = lax.broadcasted_iota(jnp.int32, (rows, half), 0) % n_chunks
    ar = ar_ref[0]
    ai = ai_ref[0]

    def cmul(x, step, lo):
        swapped = pltpu.roll(x, half // 2, axis=1)
        return x * ar[step:step + 1, lo:lo + half] + swapped * ai[step:step + 1, lo:lo + half]

    def scan(gd, h0d, lo, backward):
        if backward:
            z = jnp.where(chunk == n_chunks - 1, h0d, pltpu.roll(gd, rows - 1, axis=0))
        else:
            z = jnp.where(chunk == 0, h0d, pltpu.roll(gd, 1, axis=0))
        step, dist = 0, 1
        while dist < n_chunks:
            if backward:
                moved = jnp.where(chunk < n_chunks - dist, pltpu.roll(z, rows - dist, axis=0), 0.0)
            else:
                moved = jnp.where(chunk >= dist, pltpu.roll(z, dist, axis=0), 0.0)
            z = z + cmul(moved, step, lo)
            step, dist = step + 1, dist * 2
        return z

    gf, gb = g[:, :half], g[:, half:]
    pf = scan(gf, h0[:, :half], 0, False)
    pb = scan(gb, h0[:, half:], half, True)
    p = jnp.concatenate([pf, pb], axis=1).astype(BF16)
    y = (jnp.dot(u, w_ref[0], preferred_element_type=F32)
         + jnp.dot(p, e_ref[0], preferred_element_type=F32)
         + u.astype(F32) * d_ref[0])
    y_ref[0] = _gelu_tanh(y).astype(y_ref.dtype)
    def pick(x, which):
        return jnp.sum(jnp.where(chunk == which, x, 0.0).reshape(batch, n_chunks, half), axis=1)

    ff = pick(cmul(pf, 0, 0) + gf, n_chunks - 1)
    fb = pick(cmul(pb, 0, half) + gb, 0)
    fin_ref[0] = jnp.concatenate([ff, fb], axis=1)


def _s5(u_rows, ops, h0, n_chunks):
    groups, rows, width = u_rows.shape
    batch = rows // n_chunks
    wt, gt, et, dvec, ar, ai = ops
    lanes = gt.shape[2]

    def per_group(*shape):
        return pl.BlockSpec((1,) + shape, lambda g: (g,) + (0,) * len(shape))

    return pl.pallas_call(
        functools.partial(_s5_kernel, n_chunks=n_chunks),
        grid=(groups,),
        in_specs=[per_group(rows, width), per_group(width, width), per_group(width, lanes),
                  per_group(lanes, width), per_group(1, width), per_group(*ar.shape[1:]),
                  per_group(*ai.shape[1:]), per_group(batch, lanes)],
        out_specs=[per_group(rows, width), per_group(batch, lanes)],
        out_shape=[jax.ShapeDtypeStruct((groups, rows, width), BF16),
                   jax.ShapeDtypeStruct((groups, batch, lanes), F32)],
        compiler_params=_params("parallel"),
        name="s5",
    )(u_rows, wt, gt, et, dvec, ar, ai, h0)


def _s5_operators(lam_re, lam_im, log_dt, b_re, b_im, c_re, c_im, d_skip, chunk, max_chunks):
    hi = lax.Precision.HIGHEST
    groups, p_dim, j_dim = b_re.shape[1:]
    t = chunk
    lam = lax.complex(lam_re, lam_im)
    z = lam * jnp.exp(log_dt)[..., None]
    a = jnp.exp(z)
    b_bar = ((a - 1.0) / lam)[..., None] * lax.complex(b_re, b_im)
    c_mat = lax.complex(c_re, c_im)
    k_idx = jnp.arange(t + 1, dtype=F32)
    apow = jnp.exp(z[None] * k_idx[:, None, None, None])

    cb = c_mat[..., None] * b_bar[:, :, None]
    ap = jnp.moveaxis(apow[:t], 1, 0)
    kern = (jnp.einsum('dtgp,dgopi->dtgoi', ap.real, cb.real, precision=hi)
            - jnp.einsum('dtgp,dgopi->dtgoi', ap.imag, cb.imag, precision=hi))
    i_idx = jnp.arange(t)
    zeros = jnp.zeros((t - 1,) + kern.shape[2:], F32)
    k_lag = jnp.concatenate([zeros, kern[0]], axis=0) + jnp.concatenate([kern[1][::-1], zeros], axis=0)
    k_lag = k_lag.transpose(1, 3, 0, 2).reshape(groups, j_dim, (2 * t - 1) * j_dim).astype(BF16)
    w_all = jnp.stack([k_lag[:, :, (t - 1 - j) * j_dim:(2 * t - 1 - j) * j_dim] for j in range(t)], axis=1)
    w_all = w_all.reshape(groups, t * j_dim, t * j_dim)

    inj_f = apow[t - 1 - i_idx, 0][..., None] * b_bar[0][None]
    inj_b = apow[i_idx, 1][..., None] * b_bar[1][None]
    g_all = jnp.concatenate([inj_f.real, inj_f.imag, inj_b.real, inj_b.imag], axis=2)
    g_all = g_all.transpose(1, 0, 3, 2).reshape(groups, t * j_dim, 4 * p_dim)

    out_f = c_mat[0][None] * apow[i_idx + 1, 0][:, :, None, :]
    out_b = c_mat[1][None] * apow[t - i_idx, 1][:, :, None, :]
    e_all = jnp.concatenate([out_f.real, -out_f.imag, out_b.real, -out_b.imag], axis=3)
    e_all = e_all.transpose(1, 3, 0, 2).reshape(groups, 4 * p_dim, t * j_dim)

    d_row = jnp.tile(d_skip.reshape(groups, 1, j_dim), (1, t, 1)).reshape(groups, 1, t * j_dim)

    n_steps = max(1, (max_chunks - 1).bit_length())
    steps = (t * 2 ** jnp.arange(n_steps)).astype(F32)
    tp = jnp.exp(z[None] * steps[:, None, None, None])
    a_r = jnp.concatenate([tp[:, 0].real, tp[:, 0].real, tp[:, 1].real, tp[:, 1].real], axis=-1)
    a_i = jnp.concatenate([-tp[:, 0].imag, tp[:, 0].imag, -tp[:, 1].imag, tp[:, 1].imag], axis=-1)
    a_r = a_r.transpose(1, 0, 2)
    a_i = a_i.transpose(1, 0, 2)
    return w_all, g_all.astype(BF16), e_all.astype(BF16), d_row, a_r, a_i


def _postmix_kernel(x_ref, y_ref, gpost_ref, gpre_ref, mod_ref, x1_ref, h_ref):
    mod = mod_ref[0]
    x1 = x_ref[...] + mod[2:3] * _rms(y_ref[...].astype(F32), gpost_ref[...])
    x1_ref[...] = x1
    h_ref[...] = (_rms(x1, gpre_ref[...]) * (1.0 + mod[4:5]) + mod[3:4]).astype(h_ref.dtype)


def _postmix(x, y, g_post, g_pre, mod):
    n, d = x.shape
    tm = _tile(n // mod.shape[0], 256)
    row = pl.BlockSpec((tm, d), lambda i: (i, 0))
    gain = pl.BlockSpec((1, d), lambda i: (0, 0))
    return pl.pallas_call(
        _postmix_kernel,
        grid=(n // tm,),
        in_specs=[row, row, gain, gain, _mod_spec(mod, n, tm)],
        out_specs=[row, row],
        out_shape=[jax.ShapeDtypeStruct((n, d), F32), jax.ShapeDtypeStruct((n, d), BF16)],
        compiler_params=_params("parallel"),
        name="post_mix",
    )(x, y, g_post, g_pre, mod)


def _postmlp_kernel(x_ref, y_ref, g_ref, mod_ref, o_ref):
    o_ref[...] = x_ref[...] + mod_ref[0][5:6] * _rms(y_ref[...].astype(F32), g_ref[...])


def _postmlp(x, y, g_post, mod):
    n, d = x.shape
    tm = _tile(n // mod.shape[0], 256)
    row = pl.BlockSpec((tm, d), lambda i: (i, 0))
    return pl.pallas_call(
        _postmlp_kernel,
        grid=(n // tm,),
        in_specs=[row, row, pl.BlockSpec((1, d), lambda i: (0, 0)), _mod_spec(mod, n, tm)],
        out_specs=row,
        out_shape=jax.ShapeDtypeStruct((n, d), F32),
        compiler_params=_params("parallel"),
        name="post_mlp",
    )(x, y, g_post, mod)


def _axial_rope(n_tokens):
    rows = n_tokens // GRID_W
    row = jnp.repeat(jnp.arange(rows, dtype=F32), GRID_W)
    col = jnp.tile(jnp.arange(GRID_W, dtype=F32), rows)
    n_freq = QK_ROPE // 4
    inv_freq = ROPE_THETA ** (-jnp.arange(n_freq, dtype=F32) / n_freq)
    ang = jnp.concatenate([row[:, None] * inv_freq, col[:, None] * inv_freq], axis=-1)
    ang = jnp.concatenate([ang, ang], axis=-1)
    return jnp.cos(ang), jnp.sin(ang)


def _rotate_half_cols(w):
    half = w.shape[-1] // 2
    return jnp.concatenate([-w[..., half:], w[..., :half]], axis=-1)


def _prepare_weights(w_in, w_uq, w_ukv, w_glu, w_out, w_ff1, w_ff2, q_lora, kv_lora, s5_width, d_model):
    o1, o2, o3 = q_lora, q_lora + kv_lora, q_lora + kv_lora + QK_ROPE
    o4 = o3 + s5_width
    w_kr = w_in[:, o2:o3]
    pad = jnp.zeros((w_in.shape[0], 128 - QK_ROPE), w_in.dtype)
    w_ck = jnp.concatenate([w_in[:, o1:o2], w_kr, pad, _rotate_half_cols(w_kr), pad], axis=1)
    uq = w_uq.reshape(q_lora, N_HEADS, QK_NOPE + QK_ROPE)
    uq_plain = uq.transpose(1, 0, 2)
    uq_rope = jnp.concatenate([uq, _rotate_half_cols(uq[..., QK_NOPE:])], axis=-1).transpose(1, 0, 2)
    ukv = w_ukv.reshape(kv_lora, N_HEADS, QK_NOPE + V_HEAD).transpose(1, 0, 2)
    cast = lambda w: w.astype(BF16)
    return dict(w_q=cast(w_in[:, :o1]), w_ck=cast(w_ck), w_u=cast(w_in[:, o3:o4]), w_g=cast(w_in[:, o4:]),
                uq_plain=cast(uq_plain), uq_rope=cast(uq_rope), ukv=cast(ukv), w_glu=w_glu,
                w_out=w_out, w_ff1=w_ff1, w_ff2=cast(w_ff2))


LANES = 128


def _slab_permutation(j_dim):
    gb = LANES // j_dim
    src = np.arange(gb * LANES)
    i8, g8, j = src // LANES, (src % LANES) // j_dim, src % j_dim
    p = np.zeros((gb * LANES, gb * LANES), np.float32)
    p[src, g8 * LANES + i8 * j_dim + j] = 1.0
    return jnp.asarray(p, BF16)


def _to_rows_kernel(u_ref, p_ref, o_ref, *, chunk):
    gb, rows, _ = o_ref.shape
    tiles = [u_ref[pl.ds(i, rows, stride=chunk), :].astype(BF16) for i in range(chunk)]
    for k in range(chunk // gb):
        z = jnp.concatenate(tiles[gb * k:gb * (k + 1)], axis=1)
        uk = jnp.dot(z, p_ref[...], preferred_element_type=F32).astype(o_ref.dtype)
        for g in range(gb):
            o_ref[g, :, k * LANES:(k + 1) * LANES] = uk[:, g * LANES:(g + 1) * LANES]


def _to_group_rows(u, perm, groups, j_dim, chunk):
    n = u.shape[0]
    gb = LANES // j_dim
    rows = _tile(n // chunk, 256, 16)
    return pl.pallas_call(
        functools.partial(_to_rows_kernel, chunk=chunk),
        grid=(groups // gb, n // (rows * chunk)),
        in_specs=[pl.BlockSpec((rows * chunk, LANES), lambda g, r: (r, g)),
                  pl.BlockSpec(perm.shape, lambda g, r: (0, 0))],
        out_specs=pl.BlockSpec((gb, rows, chunk * j_dim), lambda g, r: (g, r, 0)),
        out_shape=jax.ShapeDtypeStruct((groups, n // chunk, chunk * j_dim), BF16),
        compiler_params=_params("parallel", "parallel"),
        name="to_group_rows",
    )(u, perm)


def _from_rows_kernel(y_ref, p_ref, o_ref, *, chunk):
    gb, rows, _ = y_ref.shape
    for k in range(chunk // gb):
        yk = jnp.concatenate([y_ref[g, :, k * LANES:(k + 1) * LANES] for g in range(gb)], axis=1)
        z = jnp.dot(yk, p_ref[...], preferred_element_type=F32)
        for i8 in range(gb):
            o_ref[pl.ds(gb * k + i8, rows, stride=chunk), :] = z[:, i8 * LANES:(i8 + 1) * LANES]


def _from_group_rows(y, perm_t, groups, j_dim, chunk):
    n_rows = y.shape[1]
    gb = LANES // j_dim
    rows = _tile(n_rows, 256, 16)
    return pl.pallas_call(
        functools.partial(_from_rows_kernel, chunk=chunk),
        grid=(groups // gb, n_rows // rows),
        in_specs=[pl.BlockSpec((gb, rows, chunk * j_dim), lambda g, r: (g, r, 0)),
                  pl.BlockSpec(perm_t.shape, lambda g, r: (0, 0))],
        out_specs=pl.BlockSpec((rows * chunk, LANES), lambda g, r: (r, g)),
        out_shape=jax.ShapeDtypeStruct((n_rows * chunk, groups * j_dim), F32),
        compiler_params=_params("parallel", "parallel"),
        name="from_group_rows",
    )(y, perm_t)


def _trunk(x, mod, wts, gains, s5_ops, s5_dims, ctx):
    batch, length, d_model = x.shape
    n = batch * length
    groups, p_dim, j_dim, chunk = s5_dims
    x2 = x.reshape(n, d_model)
    rope = None if ctx is None else ctx[3]

    h = _modnorm(x2, gains['g_pre_mix'], mod, 0, 1)
    q_lat = _mm(h, wts['w_q'], F32)
    ck = _mm(h, wts['w_ck'], F32)
    u = _mm(h, wts['w_u'], F32)
    gate_sig = _mm(h, wts['w_g'], BF16, epilogue="sigmoid")

    qn = _rmsnorm(q_lat, gains['g_q_lat'])
    q = _qproj(qn, wts['uq_plain'] if rope is None else wts['uq_rope'], batch, length, rope)
    ckv_n, k_rope = _kvprep(ck, gains['g_kv_lat'], length, rope)
    if ctx is None:
        ckv_all, kr_all, lk = ckv_n, k_rope, length
    else:
        lk = ctx[0].shape[1] + length
        ckv_all = jnp.concatenate([ctx[0], ckv_n.reshape(batch, length, -1)], axis=1).reshape(batch * lk, -1)
        kr_all = jnp.concatenate([ctx[1], k_rope.reshape(batch, length, -1)], axis=1).reshape(batch * lk, -1)
    k, v = _kvdec(ckv_all, kr_all, wts['ukv'], batch, lk)
    oa_gated = _attention(q, k, v, gate_sig.reshape(batch, length, -1)).reshape(n, -1)

    if ctx is None:
        h0 = jnp.zeros((groups, batch, 4 * p_dim), F32)
    else:
        h0 = ctx[2].transpose(3, 0, 1, 2, 4).reshape(groups, batch, 4 * p_dim)
    perm = _slab_permutation(j_dim)
    u_rows = _to_group_rows(u, perm, groups, j_dim, chunk)
    y_rows, s5_fin = _s5(u_rows, s5_ops, h0, length // chunk)
    y = _from_group_rows(y_rows, perm.T, groups, j_dim, chunk)
    m = _glu_merge(y, wts['w_glu'], gate_sig, oa_gated)
    mixed = _mm(m, wts['w_out'], BF16)
    x1, h2 = _postmix(x2, mixed, gains['g_post_mix'], gains['g_pre_mlp'], mod)
    a1 = _mm(h2, wts['w_ff1'], BF16, epilogue="relu2")
    a2 = _mm(a1, wts['w_ff2'], BF16)
    y_out = _postmlp(x1, a2, gains['g_post_mlp'], mod).reshape(batch, length, d_model)
    s5_fin = s5_fin.reshape(groups, batch, 2, 2, p_dim).transpose(1, 2, 3, 0, 4)
    return y_out, ckv_n.reshape(batch, length, -1), k_rope.reshape(batch, length, -1), s5_fin


def kernel(x_prompt, x_sample, c, cache_ckv, cache_krope, state_s5, c_ctx, w_mod, b_mod, g_pre_mix, w_in, g_q_lat, g_kv_lat, w_uq, w_ukv, s5_lam_re, s5_lam_im, s5_log_dt, s5_b_re, s5_b_im, s5_c_re, s5_c_im, s5_d, w_glu, w_out, g_post_mix, g_pre_mlp, w_ff1, w_ff2, g_post_mlp):
    depth = w_mod.shape[0]
    d_model = x_prompt.shape[-1]
    dec_batch, dec_seq = x_sample.shape[:2]
    q_lora, kv_lora = g_q_lat.shape[1], g_kv_lat.shape[1]
    groups, p_dim, j_dim = s5_b_re.shape[2:]
    s5_width = groups * j_dim
    chunk = min(S5_CHUNK, x_prompt.shape[1] // 8)
    rope = _axial_rope(dec_seq)

    cond = jnp.concatenate([c_ctx[None, :], c], axis=0)
    cond = jnp.pad(cond, ((0, -cond.shape[0] % 8), (0, 0)))

    xp, xs = x_prompt, x_sample
    new_ckv, new_krope, new_s5 = [], [], []
    for l in range(depth):
        mods = _adaln(cond, w_mod[l], b_mod[l][None, :])
        mod_p = mods[0:1].reshape(1, -1, d_model)
        mod_s = mods[1:1 + dec_batch].reshape(dec_batch, -1, d_model)
        wts = _prepare_weights(w_in[l], w_uq[l], w_ukv[l], w_glu[l], w_out[l], w_ff1[l], w_ff2[l],
                               q_lora, kv_lora, s5_width, d_model)
        gains = dict(g_pre_mix=g_pre_mix[l][None], g_q_lat=g_q_lat[l][None], g_kv_lat=g_kv_lat[l][None],
                     g_post_mix=g_post_mix[l][None], g_pre_mlp=g_pre_mlp[l][None], g_post_mlp=g_post_mlp[l][None])
        max_chunks = max(x_prompt.shape[1], dec_seq) // chunk
        s5_ops = _s5_operators(s5_lam_re[l], s5_lam_im[l], s5_log_dt[l], s5_b_re[l], s5_b_im[l],
                               s5_c_re[l], s5_c_im[l], s5_d[l], chunk, max_chunks)
        s5_dims = (groups, p_dim, j_dim, chunk)

        xp, ckv_n, k_rope_p, s5_last = _trunk(xp, mod_p, wts, gains, s5_ops, s5_dims, None)
        new_ckv.append(ckv_n)
        new_krope.append(k_rope_p)
        new_s5.append(s5_last)
        ctx = (cache_ckv[:, l], cache_krope[:, l], state_s5[:, l], rope)
        xs, _, _, _ = _trunk(xs, mod_s, wts, gains, s5_ops, s5_dims, ctx)
    return (xp, xs, jnp.stack(new_ckv, axis=1), jnp.stack(new_krope, axis=1),
            jnp.stack(new_s5, axis=1).astype(x_prompt.dtype))
```

```python
import functools
import math

import jax
import jax.numpy as jnp
import numpy as np
from jax import lax
from jax.experimental import pallas as pl
from jax.experimental.pallas import tpu as pltpu

N_HEADS = 32
QK_NOPE = 128
QK_ROPE = 64
V_HEAD = 128
GRID_W = 64
ROPE_THETA = 10000.0
EPS = 1e-6
S5_CHUNK = 32
LANES = 128
VMEM_LIMIT_BYTES = 48 * 1024 * 1024

F32 = jnp.float32
BF16 = jnp.bfloat16


VMEM_LIMIT_BIG_BYTES = 56 * 1024 * 1024


def _params(*semantics, vmem=VMEM_LIMIT_BYTES):
    return pltpu.CompilerParams(dimension_semantics=semantics, vmem_limit_bytes=vmem)


def _tile(n, target, align=8):
    for d in range(min(n, target), 0, -1):
        if n % d == 0 and d % align == 0:
            return d
    return n


def _adaln_kernel(c_ref, w_ref, b_ref, o_ref):
    c = c_ref[...]
    s = (c * jax.nn.sigmoid(c)).astype(BF16)
    o_ref[...] = jnp.dot(s, w_ref[...].astype(BF16), preferred_element_type=F32) + b_ref[...]


def _adaln(cond, w_mod, b_mod):
    rows, d = cond.shape
    n = w_mod.shape[1]
    tn = _tile(n, 512, 128)
    return pl.pallas_call(
        _adaln_kernel,
        grid=(n // tn,),
        in_specs=[pl.BlockSpec((rows, d), lambda j: (0, 0)),
                  pl.BlockSpec((d, tn), lambda j: (0, j)),
                  pl.BlockSpec((1, tn), lambda j: (0, j))],
        out_specs=pl.BlockSpec((rows, tn), lambda j: (0, j)),
        out_shape=jax.ShapeDtypeStruct((rows, n), F32),
        compiler_params=_params("parallel"),
        name="adaln",
    )(cond, w_mod, b_mod)


def _rms(x, g):
    return x * lax.rsqrt(jnp.mean(x * x, axis=-1, keepdims=True) + EPS) * g


def _modnorm_kernel(x_ref, g_ref, mod_ref, o_ref, *, shift_idx, scale_idx):
    mod = mod_ref[0]
    y = _rms(x_ref[...], g_ref[...])
    o_ref[...] = (y * (1.0 + mod[scale_idx:scale_idx + 1]) + mod[shift_idx:shift_idx + 1]).astype(o_ref.dtype)


def _mod_spec(mod, n_rows, tm):
    rows_per_mod = n_rows // mod.shape[0]
    return pl.BlockSpec((1,) + mod.shape[1:], lambda i: ((i * tm) // rows_per_mod, 0, 0))


def _modnorm(x, g, mod, shift_idx, scale_idx):
    n, d = x.shape
    tm = _tile(n // mod.shape[0], 256)
    return pl.pallas_call(
        functools.partial(_modnorm_kernel, shift_idx=shift_idx, scale_idx=scale_idx),
        grid=(n // tm,),
        in_specs=[pl.BlockSpec((tm, d), lambda i: (i, 0)),
                  pl.BlockSpec((1, d), lambda i: (0, 0)),
                  _mod_spec(mod, n, tm)],
        out_specs=pl.BlockSpec((tm, d), lambda i: (i, 0)),
        out_shape=jax.ShapeDtypeStruct((n, d), BF16),
        compiler_params=_params("parallel"),
        name="modnorm",
    )(x, g, mod)


def _rmsnorm_kernel(x_ref, g_ref, o_ref):
    o_ref[...] = _rms(x_ref[...], g_ref[...]).astype(o_ref.dtype)


def _rmsnorm(x, g):
    n, d = x.shape
    tm = _tile(n, 1024)
    return pl.pallas_call(
        _rmsnorm_kernel,
        grid=(n // tm,),
        in_specs=[pl.BlockSpec((tm, d), lambda i: (i, 0)),
                  pl.BlockSpec((1, d), lambda i: (0, 0))],
        out_specs=pl.BlockSpec((tm, d), lambda i: (i, 0)),
        out_shape=jax.ShapeDtypeStruct((n, d), BF16),
        compiler_params=_params("parallel"),
        name="rmsnorm",
    )(x, g)


def _epilogue(r, kind):
    if kind == "relu2":
        return jnp.square(jnp.maximum(r, 0.0))
    if kind == "sigmoid":
        return jax.nn.sigmoid(r)
    assert kind is None
    return r


def _bf16_panel(w_ref, scratch_ref):
    if scratch_ref is None:
        return w_ref[...]

    @pl.when(pl.program_id(1) == 0)
    def _():
        scratch_ref[...] = w_ref[...].astype(BF16)

    return scratch_ref[...]


def _mm_full_k_kernel(a_ref, w_ref, o_ref, wb_ref=None, *, epilogue):
    w = _bf16_panel(w_ref, wb_ref)
    r = jnp.dot(a_ref[...], w, preferred_element_type=F32)
    o_ref[...] = _epilogue(r, epilogue).astype(o_ref.dtype)


def _mm_kernel(a_ref, w_ref, o_ref, acc_ref, *, epilogue):
    k = pl.program_id(2)

    @pl.when(k == 0)
    def _():
        acc_ref[...] = jnp.zeros_like(acc_ref)

    acc_ref[...] += jnp.dot(a_ref[...], w_ref[...], preferred_element_type=F32)

    @pl.when(k == pl.num_programs(2) - 1)
    def _():
        o_ref[...] = _epilogue(acc_ref[...], epilogue).astype(o_ref.dtype)


MM_MAX_FULL_K = 4096


def _mm(a, w, out_dtype, epilogue=None):
    m, kdim = a.shape
    n = w.shape[1]
    if kdim <= MM_MAX_FULL_K:
        cast_w = w.dtype != BF16
        tm, tn = _tile(m, 512 if kdim > 2048 else 1024), _tile(n, 1024, 128)
        return pl.pallas_call(
            functools.partial(_mm_full_k_kernel, epilogue=epilogue),
            grid=(n // tn, m // tm),
            in_specs=[pl.BlockSpec((tm, kdim), lambda j, i: (i, 0)),
                      pl.BlockSpec((kdim, tn), lambda j, i: (0, j))],
            out_specs=pl.BlockSpec((tm, tn), lambda j, i: (i, j)),
            out_shape=jax.ShapeDtypeStruct((m, n), out_dtype),
            scratch_shapes=[pltpu.VMEM((kdim, tn), BF16)] if cast_w else [],
            compiler_params=(_params("parallel", "arbitrary", vmem=VMEM_LIMIT_BIG_BYTES) if cast_w
                             else _params("parallel", "parallel")),
            name="matmul",
        )(a, w)
    assert w.dtype == BF16
    tm, tn, tk = _tile(m, 1024), _tile(n, 1024, 128), _tile(kdim, MM_MAX_FULL_K, 128)
    return pl.pallas_call(
        functools.partial(_mm_kernel, epilogue=epilogue),
        grid=(m // tm, n // tn, kdim // tk),
        in_specs=[pl.BlockSpec((tm, tk), lambda i, j, k: (i, k)),
                  pl.BlockSpec((tk, tn), lambda i, j, k: (k, j))],
        out_specs=pl.BlockSpec((tm, tn), lambda i, j, k: (i, j)),
        out_shape=jax.ShapeDtypeStruct((m, n), out_dtype),
        scratch_shapes=[pltpu.VMEM((tm, tn), F32)],
        compiler_params=_params("parallel", "parallel", "arbitrary"),
        name="matmul_k",
    )(a, w)


def _glu_merge_kernel(a_ref, wv_ref, wg_ref, sb_ref, oa_ref, o_ref):
    a = a_ref[...].astype(BF16)
    value = jnp.dot(a, wv_ref[...], preferred_element_type=F32)
    gate = jnp.dot(a, wg_ref[...], preferred_element_type=F32)
    o_b = value * jax.nn.sigmoid(gate)
    o_ref[...] = (oa_ref[...].astype(F32) + sb_ref[...].astype(F32) * o_b).astype(o_ref.dtype)


def _glu_merge(y, w_glu, gate_sig, oa_gated):
    m, kdim = y.shape
    d = w_glu.shape[1] // 2
    tm, tn = _tile(m, 1024), _tile(d, 512, 128)
    nj = d // tn
    return pl.pallas_call(
        _glu_merge_kernel,
        grid=(nj, m // tm),
        in_specs=[pl.BlockSpec((tm, kdim), lambda j, i: (i, 0)),
                  pl.BlockSpec((kdim, tn), lambda j, i: (0, j)),
                  pl.BlockSpec((kdim, tn), lambda j, i: (0, j + nj)),
                  pl.BlockSpec((tm, tn), lambda j, i: (i, j + nj)),
                  pl.BlockSpec((tm, tn), lambda j, i: (i, j))],
        out_specs=pl.BlockSpec((tm, tn), lambda j, i: (i, j)),
        out_shape=jax.ShapeDtypeStruct((m, d), BF16),
        compiler_params=_params("parallel", "parallel"),
        name="glu_merge",
    )(y, w_glu, w_glu, gate_sig, oa_gated)


def _qproj_kernel(a_ref, w_ref, *rest, scale, use_rope):
    if use_rope:
        cos_ref, sin_ref, o_ref = rest
    else:
        (o_ref,) = rest
    nb, hb, tl, _ = o_ref.shape
    a = a_ref[...]
    for h in range(hb):
        acc = jnp.dot(a, w_ref[h], preferred_element_type=F32)
        o_ref[:, h, :, :QK_NOPE] = (acc[:, :QK_NOPE] * scale).reshape(nb, tl, QK_NOPE).astype(o_ref.dtype)
        rope = acc[:, QK_NOPE:QK_NOPE + QK_ROPE]
        if use_rope:
            rope = rope * cos_ref[...] + acc[:, QK_NOPE + QK_ROPE:QK_NOPE + 2 * QK_ROPE] * sin_ref[...]
        o_ref[:, h, :, QK_NOPE:] = (rope * scale).reshape(nb, tl, QK_ROPE).astype(o_ref.dtype)


HEADS_PER_STEP = 4


def _row_blocking(batch, length, target):
    if length >= target:
        return 1, _tile(length, target)
    return _tile(batch, max(1, target // length), 1), length


def _head_out_spec(nb, hb, tl, length, width):
    per_batch = length // tl
    return pl.BlockSpec((nb, hb, tl, width), lambda i, h: (i // per_batch, h, i % per_batch, 0))


def _qproj(qn, w_heads, batch, length, rope_tables):
    n, kdim = qn.shape
    heads, _, cols = w_heads.shape
    use_rope = rope_tables is not None
    nb, tl = (1, _tile(length, 1024)) if use_rope else _row_blocking(batch, length, 1024)
    tm = nb * tl
    hb = _tile(heads, HEADS_PER_STEP, 1)
    in_specs = [pl.BlockSpec((tm, kdim), lambda i, h: (i, 0)),
                pl.BlockSpec((hb, kdim, cols), lambda i, h: (h, 0, 0))]
    args = [qn, w_heads]
    if use_rope:
        per_batch = length // tl
        spec = pl.BlockSpec((tl, QK_ROPE), lambda i, h: (i % per_batch, 0))
        in_specs += [spec, spec]
        args += list(rope_tables)
    width = QK_NOPE + QK_ROPE
    return pl.pallas_call(
        functools.partial(_qproj_kernel, scale=width ** -0.5, use_rope=use_rope),
        grid=(n // tm, heads // hb),
        in_specs=in_specs,
        out_specs=_head_out_spec(nb, hb, tl, length, width),
        out_shape=jax.ShapeDtypeStruct((batch, heads, length, width), BF16),
        compiler_params=_params("parallel", "parallel"),
        name="q_proj",
    )(*args)


def _kvprep_kernel(ck_ref, g_ref, *rest, kv_lora, use_rope):
    if use_rope:
        cos_ref, sin_ref, ckv_ref, kr_ref = rest
    else:
        ckv_ref, kr_ref = rest
    ckv_ref[...] = _rms(ck_ref[:, :kv_lora], g_ref[...])
    kr = ck_ref[:, kv_lora:kv_lora + QK_ROPE]
    if use_rope:
        kr = kr * cos_ref[...] + ck_ref[:, kv_lora + 128:kv_lora + 128 + QK_ROPE] * sin_ref[...]
    kr_ref[...] = kr


def _kvprep(ck, g_kv, length, rope_tables):
    n, cols = ck.shape
    kv_lora = cols - 256
    tm = _tile(length, 512)
    use_rope = rope_tables is not None
    in_specs = [pl.BlockSpec((tm, cols), lambda i: (i, 0)),
                pl.BlockSpec((1, kv_lora), lambda i: (0, 0))]
    args = [ck, g_kv]
    if use_rope:
        per_batch = length // tm
        spec = pl.BlockSpec((tm, QK_ROPE), lambda i: (i % per_batch, 0))
        in_specs += [spec, spec]
        args += list(rope_tables)
    return pl.pallas_call(
        functools.partial(_kvprep_kernel, kv_lora=kv_lora, use_rope=use_rope),
        grid=(n // tm,),
        in_specs=in_specs,
        out_specs=[pl.BlockSpec((tm, kv_lora), lambda i: (i, 0)),
                   pl.BlockSpec((tm, QK_ROPE), lambda i: (i, 0))],
        out_shape=[jax.ShapeDtypeStruct((n, kv_lora), F32),
                   jax.ShapeDtypeStruct((n, QK_ROPE), F32)],
        compiler_params=_params("parallel"),
        name="kv_prep",
    )(*args)


def _kvdec_kernel(c_ref, kr_ref, w_ref, k_ref, v_ref):
    nb, hb, tl, _ = k_ref.shape
    c = c_ref[...].astype(BF16)
    k_rope = kr_ref[...].reshape(nb, tl, QK_ROPE).astype(k_ref.dtype)
    for h in range(hb):
        acc = jnp.dot(c, w_ref[h], preferred_element_type=F32)
        k_ref[:, h, :, :QK_NOPE] = acc[:, :QK_NOPE].reshape(nb, tl, QK_NOPE).astype(k_ref.dtype)
        k_ref[:, h, :, QK_NOPE:] = k_rope
        v_ref[:, h] = acc[:, QK_NOPE:].reshape(nb, tl, V_HEAD).astype(v_ref.dtype)


def _kvdec(ckv, krope, w_heads, batch, length):
    n, kdim = ckv.shape
    heads, _, cols = w_heads.shape
    nb, tl = _row_blocking(batch, length, 1536)
    tm = nb * tl
    hb = _tile(heads, HEADS_PER_STEP, 1)
    return pl.pallas_call(
        _kvdec_kernel,
        grid=(n // tm, heads // hb),
        in_specs=[pl.BlockSpec((tm, kdim), lambda i, h: (i, 0)),
                  pl.BlockSpec((tm, QK_ROPE), lambda i, h: (i, 0)),
                  pl.BlockSpec((hb, kdim, cols), lambda i, h: (h, 0, 0))],
        out_specs=[_head_out_spec(nb, hb, tl, length, QK_NOPE + QK_ROPE),
                   _head_out_spec(nb, hb, tl, length, V_HEAD)],
        out_shape=[jax.ShapeDtypeStruct((batch, heads, length, QK_NOPE + QK_ROPE), BF16),
                   jax.ShapeDtypeStruct((batch, heads, length, V_HEAD), BF16)],
        compiler_params=_params("parallel", "parallel"),
        name="kv_decompress",
    )(ckv, krope, w_heads)


def _qk(q, k):
    return lax.dot_general(q, k, (((1,), (1,)), ((), ())), preferred_element_type=F32)


def _attn_kernel(q_ref, k_ref, v_ref, gate_ref, o_ref, *stat_refs, tk, nk, sub):
    hb, tq, dv = q_ref.shape[1], q_ref.shape[2], v_ref.shape[3]
    if nk > 1:
        m_ref, l_ref, acc_ref = stat_refs
    lanes = dv

    def write_out(h, rows, acc, l):
        cols = slice(h * dv, (h + 1) * dv)
        o_ref[0, rows, cols] = (acc / l * gate_ref[0, rows, cols].astype(F32)).astype(o_ref.dtype)

    for h in range(hb):
        for r in range(tq // sub):
            rows = pl.ds(r * sub, sub)
            q = q_ref[0, h, rows, :]
            s = _qk(q, k_ref[0, h, :tk, :])
            m = jnp.max(s, axis=-1, keepdims=True)
            p = jnp.exp(s - m)
            l = jnp.sum(p, axis=-1, keepdims=True)
            acc = jnp.dot(p.astype(BF16), v_ref[0, h, :tk, :], preferred_element_type=F32)
            if nk > 1:
                m_ref[h, rows, :] = jnp.broadcast_to(m, (sub, lanes))
                l_ref[h, rows, :] = jnp.broadcast_to(l, (sub, lanes))
                acc_ref[h, rows, :] = acc
            else:
                write_out(h, rows, acc, l)
        for j in range(1, nk):
            for r in range(tq // sub):
                rows = pl.ds(r * sub, sub)
                q = q_ref[0, h, rows, :]
                s = _qk(q, k_ref[0, h, j * tk:(j + 1) * tk, :])
                m_old = m_ref[h, rows, :]
                m_new = jnp.maximum(m_old, jnp.max(s, axis=-1, keepdims=True))
                alpha = jnp.exp(m_old - m_new)
                p = jnp.exp(s - m_new[:, :1])
                l = alpha * l_ref[h, rows, :] + jnp.sum(p, axis=-1, keepdims=True)
                acc = alpha * acc_ref[h, rows, :] + jnp.dot(
                    p.astype(BF16), v_ref[0, h, j * tk:(j + 1) * tk, :], preferred_element_type=F32)
                if j < nk - 1:
                    m_ref[h, rows, :] = m_new
                    l_ref[h, rows, :] = l
                    acc_ref[h, rows, :] = acc
                else:
                    write_out(h, rows, acc, l)


ATTN_MAX_KV_CHUNK = 2304
ATTN_STEP_ROWS = 2048


def _attention(q, k, v, gate):
    batch, heads, lq, dqk = q.shape
    lk, dv = k.shape[2], v.shape[3]
    tq = _tile(lq, 2048)
    sub = _tile(tq, 512)
    hb = _tile(heads, max(1, ATTN_STEP_ROWS // lq), 1) if tq == lq else 1
    tk = _tile(lk, ATTN_MAX_KV_CHUNK, 128) if lk > ATTN_MAX_KV_CHUNK else lk
    nk = lk // tk
    stats = [pltpu.VMEM((hb, tq, dv), F32)] * 3 if nk > 1 else []
    return pl.pallas_call(
        functools.partial(_attn_kernel, tk=tk, nk=nk, sub=sub),
        grid=(batch, heads // hb, lq // tq),
        in_specs=[pl.BlockSpec((1, hb, tq, dqk), lambda b, h, i: (b, h, i, 0)),
                  pl.BlockSpec((1, hb, lk, dqk), lambda b, h, i: (b, h, 0, 0)),
                  pl.BlockSpec((1, hb, lk, dv), lambda b, h, i: (b, h, 0, 0)),
                  pl.BlockSpec((1, tq, hb * dv), lambda b, h, i: (b, i, h))],
        out_specs=pl.BlockSpec((1, tq, hb * dv), lambda b, h, i: (b, i, h)),
        out_shape=jax.ShapeDtypeStruct((batch, lq, heads * dv), BF16),
        scratch_shapes=stats,
        compiler_params=_params("parallel", "parallel", "arbitrary"),
        name="attention",
    )(q, k, v, gate)


def _gelu_tanh(x):
    return 0.5 * x * (1.0 + jnp.tanh(math.sqrt(2.0 / math.pi) * (x + 0.044715 * (x * x * x))))


def _s5_kernel(u_ref, w_ref, g_ref, e_ref, d_ref, ar_ref, ai_ref, h0_ref, y_ref, fin_ref, *, n_chunks):
    u = u_ref[0]
    rows = u.shape[0]
    batch = rows // n_chunks
    half = g_ref.shape[2] // 2
    g = jnp.dot(u, g_ref[0], preferred_element_type=F32)
    h0 = jnp.broadcast_to(h0_ref[0][:, None, :], (batch, n_chunks, 2 * half)).reshape(rows, 2 * half)
    chunk = lax.broadcasted_iota(jnp.int32, (rows, half), 0) % n_chunks
    ar = ar_ref[0]
    ai = ai_ref[0]

    def cmul(x, step, lo):
        swapped = pltpu.roll(x, half // 2, axis=1)
        return x * ar[step:step + 1, lo:lo + half] + swapped * ai[step:step + 1, lo:lo + half]

    def scan(gd, h0d, lo, backward):
        if backward:
            z = jnp.where(chunk == n_chunks - 1, h0d, pltpu.roll(gd, rows - 1, axis=0))
        else:
            z = jnp.where(chunk == 0, h0d, pltpu.roll(gd, 1, axis=0))
        step, dist = 0, 1
        while dist < n_chunks:
            if backward:
                moved = jnp.where(chunk < n_chunks - dist, pltpu.roll(z, rows - dist, axis=0), 0.0)
            else:
                moved = jnp.where(chunk >= dist, pltpu.roll(z, dist, axis=0), 0.0)
            z = z + cmul(moved, step, lo)
            step, dist = step + 1, dist * 2
        return z

    gf, gb = g[:, :half], g[:, half:]
    pf = scan(gf, h0[:, :half], 0, False)
    pb = scan(gb, h0[:, half:], half, True)
    p = jnp.concatenate([pf, pb], axis=1).astype(BF16)
    y = (jnp.dot(u, w_ref[0], preferred_element_type=F32)
         + jnp.dot(p, e_ref[0], preferred_element_type=F32)
         + u.astype(F32) * d_ref[0])
    y_ref[0] = _gelu_tanh(y).astype(y_ref.dtype)
    def pick(x, which):
        return jnp.sum(jnp.where(chunk == which, x, 0.0).reshape(batch, n_chunks, half), axis=1)

    ff = pick(cmul(pf, 0, 0) + gf, n_chunks - 1)
    fb = pick(cmul(pb, 0, half) + gb, 0)
    fin_ref[0] = jnp.concatenate([ff, fb], axis=1)


def _s5(u_rows, ops, h0, n_chunks):
    groups, rows, width = u_rows.shape
    batch = rows // n_chunks
    wt, gt, et, dvec, ar, ai = ops
    lanes = gt.shape[2]

    def per_group(*shape):
        return pl.BlockSpec((1,) + shape, lambda g: (g,) + (0,) * len(shape))

    return pl.pallas_call(
        functools.partial(_s5_kernel, n_chunks=n_chunks),
        grid=(groups,),
        in_specs=[per_group(rows, width), per_group(width, width), per_group(width, lanes),
                  per_group(lanes, width), per_group(1, width), per_group(*ar.shape[1:]),
                  per_group(*ai.shape[1:]), per_group(batch, lanes)],
        out_specs=[per_group(rows, width), per_group(batch, lanes)],
        out_shape=[jax.ShapeDtypeStruct((groups, rows, width), BF16),
                   jax.ShapeDtypeStruct((groups, batch, lanes), F32)],
        compiler_params=_params("parallel"),
        name="s5",
    )(u_rows, wt, gt, et, dvec, ar, ai, h0)


def _s5_operators(lam_re, lam_im, log_dt, b_re, b_im, c_re, c_im, d_skip, chunk, max_chunks):
    hi = lax.Precision.HIGHEST
    groups, p_dim, j_dim = b_re.shape[1:]
    t = chunk
    lam = lax.complex(lam_re, lam_im)
    z = lam * jnp.exp(log_dt)[..., None]
    a = jnp.exp(z)
    b_bar = ((a - 1.0) / lam)[..., None] * lax.complex(b_re, b_im)
    c_mat = lax.complex(c_re, c_im)
    k_idx = jnp.arange(t + 1, dtype=F32)
    apow = jnp.exp(z[None] * k_idx[:, None, None, None])

    cb = c_mat[..., None] * b_bar[:, :, None]
    ap = jnp.moveaxis(apow[:t], 1, 0)
    kern = (jnp.einsum('dtgp,dgopi->dtgoi', ap.real, cb.real, precision=hi)
            - jnp.einsum('dtgp,dgopi->dtgoi', ap.imag, cb.imag, precision=hi))
    zeros = jnp.zeros((t - 1,) + kern.shape[2:], F32)
    k_lag = jnp.concatenate([zeros, kern[0]], axis=0) + jnp.concatenate([kern[1][::-1], zeros], axis=0)
    k_lag = k_lag.transpose(1, 3, 0, 2).reshape(groups, j_dim, (2 * t - 1) * j_dim).astype(BF16)
    w_all = jnp.stack([k_lag[:, :, (t - 1 - j) * j_dim:(2 * t - 1 - j) * j_dim] for j in range(t)], axis=1)
    w_all = w_all.reshape(groups, t * j_dim, t * j_dim)

    apow_tp = jnp.moveaxis(apow, 0, 2)
    b_t = jnp.swapaxes(b_bar, -1, -2)
    inj_f = apow_tp[0][:, t - 1::-1][:, :, None, :] * b_t[0][:, None]
    inj_b = apow_tp[1][:, :t][:, :, None, :] * b_t[1][:, None]
    g_all = jnp.concatenate([inj_f.real, inj_f.imag, inj_b.real, inj_b.imag], axis=-1)
    g_all = g_all.reshape(groups, t * j_dim, 4 * p_dim)

    apow_pt = jnp.moveaxis(apow, 0, -1)
    c_t = jnp.swapaxes(c_mat, -1, -2)
    out_f = c_t[0][:, :, None, :] * apow_pt[0][:, :, 1:t + 1][..., None]
    out_b = c_t[1][:, :, None, :] * apow_pt[1][:, :, t:0:-1][..., None]
    e_all = jnp.concatenate([out_f.real, -out_f.imag, out_b.real, -out_b.imag], axis=1)
    e_all = e_all.reshape(groups, 4 * p_dim, t * j_dim)

    d_row = jnp.tile(d_skip.reshape(groups, 1, j_dim), (1, t, 1)).reshape(groups, 1, t * j_dim)

    n_steps = max(1, (max_chunks - 1).bit_length())
    steps = (t * 2 ** jnp.arange(n_steps)).astype(F32)
    tp = jnp.exp(z[None] * steps[:, None, None, None])
    a_r = jnp.concatenate([tp[:, 0].real, tp[:, 0].real, tp[:, 1].real, tp[:, 1].real], axis=-1)
    a_i = jnp.concatenate([-tp[:, 0].imag, tp[:, 0].imag, -tp[:, 1].imag, tp[:, 1].imag], axis=-1)
    a_r = a_r.transpose(1, 0, 2)
    a_i = a_i.transpose(1, 0, 2)
    return w_all, g_all.astype(BF16), e_all.astype(BF16), d_row, a_r, a_i


def _postmix_kernel(x_ref, y_ref, gpost_ref, gpre_ref, mod_ref, x1_ref, h_ref):
    mod = mod_ref[0]
    x1 = x_ref[...] + mod[2:3] * _rms(y_ref[...].astype(F32), gpost_ref[...])
    x1_ref[...] = x1
    h_ref[...] = (_rms(x1, gpre_ref[...]) * (1.0 + mod[4:5]) + mod[3:4]).astype(h_ref.dtype)


def _postmix(x, y, g_post, g_pre, mod):
    n, d = x.shape
    tm = _tile(n // mod.shape[0], 256)
    row = pl.BlockSpec((tm, d), lambda i: (i, 0))
    gain = pl.BlockSpec((1, d), lambda i: (0, 0))
    return pl.pallas_call(
        _postmix_kernel,
        grid=(n // tm,),
        in_specs=[row, row, gain, gain, _mod_spec(mod, n, tm)],
        out_specs=[row, row],
        out_shape=[jax.ShapeDtypeStruct((n, d), F32), jax.ShapeDtypeStruct((n, d), BF16)],
        compiler_params=_params("parallel"),
        name="post_mix",
    )(x, y, g_post, g_pre, mod)


def _postmlp_kernel(x_ref, y_ref, g_ref, mod_ref, o_ref):
    o_ref[...] = x_ref[...] + mod_ref[0][5:6] * _rms(y_ref[...].astype(F32), g_ref[...])


def _postmlp(x, y, g_post, mod):
    n, d = x.shape
    tm = _tile(n // mod.shape[0], 256)
    row = pl.BlockSpec((tm, d), lambda i: (i, 0))
    return pl.pallas_call(
        _postmlp_kernel,
        grid=(n // tm,),
        in_specs=[row, row, pl.BlockSpec((1, d), lambda i: (0, 0)), _mod_spec(mod, n, tm)],
        out_specs=row,
        out_shape=jax.ShapeDtypeStruct((n, d), F32),
        compiler_params=_params("parallel"),
        name="post_mlp",
    )(x, y, g_post, mod)


def _axial_rope(n_tokens):
    rows = n_tokens // GRID_W
    row = jnp.repeat(jnp.arange(rows, dtype=F32), GRID_W)
    col = jnp.tile(jnp.arange(GRID_W, dtype=F32), rows)
    n_freq = QK_ROPE // 4
    inv_freq = ROPE_THETA ** (-jnp.arange(n_freq, dtype=F32) / n_freq)
    ang = jnp.concatenate([row[:, None] * inv_freq, col[:, None] * inv_freq], axis=-1)
    ang = jnp.concatenate([ang, ang], axis=-1)
    return jnp.cos(ang), jnp.sin(ang)


def _rotate_half_cols(w):
    half = w.shape[-1] // 2
    return jnp.concatenate([-w[..., half:], w[..., :half]], axis=-1)


def _repack_w_in_kernel(w_ref, q_ref, ck_ref, u_ref, g_ref, *, o1, o2, o3, o4):
    rows = w_ref.shape[1]
    kv, half = o2 - o1, QK_ROPE // 2
    zeros = jnp.zeros((rows, LANES - QK_ROPE), BF16)
    q_ref[...] = w_ref[0, :, :o1].astype(BF16)
    ck_ref[:, :kv] = w_ref[0, :, o1:o2].astype(BF16)
    k_rope = w_ref[0, :, o2:o3]
    ck_ref[:, kv:kv + QK_ROPE] = k_rope.astype(BF16)
    ck_ref[:, kv + QK_ROPE:kv + LANES] = zeros
    ck_ref[:, kv + LANES:kv + LANES + half] = (-k_rope[:, half:]).astype(BF16)
    ck_ref[:, kv + LANES + half:kv + LANES + QK_ROPE] = k_rope[:, :half].astype(BF16)
    ck_ref[:, kv + LANES + QK_ROPE:] = zeros
    u_ref[...] = w_ref[0, :, o3:o4].astype(BF16)
    g_ref[...] = w_ref[0, :, o4:].astype(BF16)


def _repack_w_in(w_in, layer, q_lora, kv_lora, s5_width):
    _, d, cols = w_in.shape
    o1, o2 = q_lora, q_lora + kv_lora
    o3 = o2 + QK_ROPE
    o4 = o3 + s5_width
    rows = _tile(d, 128, 16)
    widths = (o1, kv_lora + 2 * LANES, s5_width, cols - o4)
    return pl.pallas_call(
        functools.partial(_repack_w_in_kernel, o1=o1, o2=o2, o3=o3, o4=o4),
        grid=(d // rows,),
        in_specs=[pl.BlockSpec((1, rows, cols), lambda i: (layer, i, 0))],
        out_specs=[pl.BlockSpec((rows, w), lambda i: (i, 0)) for w in widths],
        out_shape=[jax.ShapeDtypeStruct((d, w), BF16) for w in widths],
        compiler_params=_params("parallel"),
        name="repack_w_in",
    )(w_in)


def _prepare_weights(w_in, layer, w_uq, w_ukv, w_glu, w_out, w_ff1, w_ff2, q_lora, kv_lora, s5_width):
    w_q, w_ck, w_u, w_g = _repack_w_in(w_in, layer, q_lora, kv_lora, s5_width)
    uq = w_uq.reshape(q_lora, N_HEADS, QK_NOPE + QK_ROPE)
    uq_plain = uq.transpose(1, 0, 2)
    uq_rope = jnp.concatenate([uq, _rotate_half_cols(uq[..., QK_NOPE:])], axis=-1).transpose(1, 0, 2)
    ukv = w_ukv.reshape(kv_lora, N_HEADS, QK_NOPE + V_HEAD).transpose(1, 0, 2)
    cast = lambda w: w.astype(BF16)
    return dict(w_q=w_q, w_ck=w_ck, w_u=w_u, w_g=w_g,
                uq_plain=cast(uq_plain), uq_rope=cast(uq_rope), ukv=cast(ukv), w_glu=cast(w_glu),
                w_out=w_out, w_ff1=w_ff1, w_ff2=cast(w_ff2))


def _slab_permutation(j_dim):
    gb = LANES // j_dim
    src = np.arange(gb * LANES)
    i8, g8, j = src // LANES, (src % LANES) // j_dim, src % j_dim
    p = np.zeros((gb * LANES, gb * LANES), np.float32)
    p[src, g8 * LANES + i8 * j_dim + j] = 1.0
    return jnp.asarray(p, BF16)


def _to_rows_kernel(u_ref, p_ref, o_ref, *, chunk):
    gb, rows, _ = o_ref.shape
    tiles = [u_ref[pl.ds(i, rows, stride=chunk), :].astype(BF16) for i in range(chunk)]
    for k in range(chunk // gb):
        z = jnp.concatenate(tiles[gb * k:gb * (k + 1)], axis=1)
        uk = jnp.dot(z, p_ref[...], preferred_element_type=F32).astype(o_ref.dtype)
        for g in range(gb):
            o_ref[g, :, k * LANES:(k + 1) * LANES] = uk[:, g * LANES:(g + 1) * LANES]


def _to_group_rows(u, perm, groups, j_dim, chunk):
    n = u.shape[0]
    gb = LANES // j_dim
    rows = _tile(n // chunk, 256, 16)
    return pl.pallas_call(
        functools.partial(_to_rows_kernel, chunk=chunk),
        grid=(groups // gb, n // (rows * chunk)),
        in_specs=[pl.BlockSpec((rows * chunk, LANES), lambda g, r: (r, g)),
                  pl.BlockSpec(perm.shape, lambda g, r: (0, 0))],
        out_specs=pl.BlockSpec((gb, rows, chunk * j_dim), lambda g, r: (g, r, 0)),
        out_shape=jax.ShapeDtypeStruct((groups, n // chunk, chunk * j_dim), BF16),
        compiler_params=_params("parallel", "parallel"),
        name="to_group_rows",
    )(u, perm)


def _from_rows_kernel(y_ref, p_ref, o_ref, *, chunk):
    gb, rows, _ = y_ref.shape
    for k in range(chunk // gb):
        yk = jnp.concatenate([y_ref[g, :, k * LANES:(k + 1) * LANES] for g in range(gb)], axis=1)
        z = jnp.dot(yk, p_ref[...], preferred_element_type=F32)
        for i8 in range(gb):
            o_ref[pl.ds(gb * k + i8, rows, stride=chunk), :] = z[:, i8 * LANES:(i8 + 1) * LANES]


def _from_group_rows(y, perm_t, groups, j_dim, chunk):
    n_rows = y.shape[1]
    gb = LANES // j_dim
    rows = _tile(n_rows, 256, 16)
    return pl.pallas_call(
        functools.partial(_from_rows_kernel, chunk=chunk),
        grid=(groups // gb, n_rows // rows),
        in_specs=[pl.BlockSpec((gb, rows, chunk * j_dim), lambda g, r: (g, r, 0)),
                  pl.BlockSpec(perm_t.shape, lambda g, r: (0, 0))],
        out_specs=pl.BlockSpec((rows * chunk, LANES), lambda g, r: (r, g)),
        out_shape=jax.ShapeDtypeStruct((n_rows * chunk, groups * j_dim), F32),
        compiler_params=_params("parallel", "parallel"),
        name="from_group_rows",
    )(y, perm_t)


def _trunk(x, mod, wts, gains, s5_ops, s5_dims, ctx):
    batch, length, d_model = x.shape
    n = batch * length
    groups, p_dim, j_dim, chunk = s5_dims
    x2 = x.reshape(n, d_model)
    rope = None if ctx is None else ctx[3]

    h = _modnorm(x2, gains['g_pre_mix'], mod, 0, 1)
    q_lat = _mm(h, wts['w_q'], F32)
    ck = _mm(h, wts['w_ck'], F32)
    u = _mm(h, wts['w_u'], F32)
    gate_sig = _mm(h, wts['w_g'], BF16, epilogue="sigmoid")

    qn = _rmsnorm(q_lat, gains['g_q_lat'])
    q = _qproj(qn, wts['uq_plain'] if rope is None else wts['uq_rope'], batch, length, rope)
    ckv_n, k_rope = _kvprep(ck, gains['g_kv_lat'], length, rope)
    if ctx is None:
        ckv_all, kr_all, lk = ckv_n, k_rope, length
    else:
        lk = ctx[0].shape[1] + length
        ckv_all = jnp.concatenate([ctx[0], ckv_n.reshape(batch, length, -1)], axis=1).reshape(batch * lk, -1)
        kr_all = jnp.concatenate([ctx[1], k_rope.reshape(batch, length, -1)], axis=1).reshape(batch * lk, -1)
    k, v = _kvdec(ckv_all, kr_all, wts['ukv'], batch, lk)
    oa_gated = _attention(q, k, v, gate_sig.reshape(batch, length, -1)).reshape(n, -1)

    if ctx is None:
        h0 = jnp.zeros((groups, batch, 4 * p_dim), F32)
    else:
        h0 = ctx[2].transpose(3, 0, 1, 2, 4).reshape(groups, batch, 4 * p_dim)
    perm = _slab_permutation(j_dim)
    u_rows = _to_group_rows(u, perm, groups, j_dim, chunk)
    y_rows, s5_fin = _s5(u_rows, s5_ops, h0, length // chunk)
    y = _from_group_rows(y_rows, perm.T, groups, j_dim, chunk)
    m = _glu_merge(y, wts['w_glu'], gate_sig, oa_gated)
    mixed = _mm(m, wts['w_out'], BF16)
    x1, h2 = _postmix(x2, mixed, gains['g_post_mix'], gains['g_pre_mlp'], mod)
    a1 = _mm(h2, wts['w_ff1'], BF16, epilogue="relu2")
    a2 = _mm(a1, wts['w_ff2'], BF16)
    y_out = _postmlp(x1, a2, gains['g_post_mlp'], mod).reshape(batch, length, d_model)
    s5_fin = s5_fin.reshape(groups, batch, 2, 2, p_dim).transpose(1, 2, 3, 0, 4)
    return y_out, ckv_n.reshape(batch, length, -1), k_rope.reshape(batch, length, -1), s5_fin


def kernel(x_prompt, x_sample, c, cache_ckv, cache_krope, state_s5, c_ctx, w_mod, b_mod, g_pre_mix, w_in, g_q_lat, g_kv_lat, w_uq, w_ukv, s5_lam_re, s5_lam_im, s5_log_dt, s5_b_re, s5_b_im, s5_c_re, s5_c_im, s5_d, w_glu, w_out, g_post_mix, g_pre_mlp, w_ff1, w_ff2, g_post_mlp):
    depth = w_mod.shape[0]
    d_model = x_prompt.shape[-1]
    dec_batch, dec_seq = x_sample.shape[:2]
    q_lora, kv_lora = g_q_lat.shape[1], g_kv_lat.shape[1]
    groups, p_dim, j_dim = s5_b_re.shape[2:]
    s5_width = groups * j_dim
    chunk = min(S5_CHUNK, x_prompt.shape[1] // 8)
    rope = _axial_rope(dec_seq)

    cond = jnp.concatenate([c_ctx[None, :], c], axis=0)
    cond = jnp.pad(cond, ((0, -cond.shape[0] % 8), (0, 0)))

    xp, xs = x_prompt, x_sample
    new_ckv, new_krope, new_s5 = [], [], []
    for l in range(depth):
        mods = _adaln(cond, w_mod[l], b_mod[l][None, :])
        mod_p = mods[0:1].reshape(1, -1, d_model)
        mod_s = mods[1:1 + dec_batch].reshape(dec_batch, -1, d_model)
        wts = _prepare_weights(w_in, l, w_uq[l], w_ukv[l], w_glu[l], w_out[l], w_ff1[l], w_ff2[l],
                               q_lora, kv_lora, s5_width)
        gains = dict(g_pre_mix=g_pre_mix[l][None], g_q_lat=g_q_lat[l][None], g_kv_lat=g_kv_lat[l][None],
                     g_post_mix=g_post_mix[l][None], g_pre_mlp=g_pre_mlp[l][None], g_post_mlp=g_post_mlp[l][None])
        max_chunks = max(x_prompt.shape[1], dec_seq) // chunk
        s5_ops = _s5_operators(s5_lam_re[l], s5_lam_im[l], s5_log_dt[l], s5_b_re[l], s5_b_im[l],
                               s5_c_re[l], s5_c_im[l], s5_d[l], chunk, max_chunks)
        s5_dims = (groups, p_dim, j_dim, chunk)

        xp, ckv_n, k_rope_p, s5_last = _trunk(xp, mod_p, wts, gains, s5_ops, s5_dims, None)
        new_ckv.append(ckv_n)
        new_krope.append(k_rope_p)
        new_s5.append(s5_last)
        ctx = (cache_ckv[:, l], cache_krope[:, l], state_s5[:, l], rope)
        xs, _, _, _ = _trunk(xs, mod_s, wts, gains, s5_ops, s5_dims, ctx)
    return (xp, xs, jnp.stack(new_ckv, axis=1), jnp.stack(new_krope, axis=1),
            jnp.stack(new_s5, axis=1).astype(x_prompt.dtype))
```

```python
import functools
import math

import jax
import jax.numpy as jnp
import numpy as np
from jax import lax
from jax.experimental import pallas as pl
from jax.experimental.pallas import tpu as pltpu

N_HEADS = 32
QK_NOPE = 128
QK_ROPE = 64
V_HEAD = 128
GRID_W = 64
ROPE_THETA = 10000.0
EPS = 1e-6
S5_CHUNK = 32
LANES = 128
VMEM_LIMIT_BYTES = 48 * 1024 * 1024

F32 = jnp.float32
BF16 = jnp.bfloat16


VMEM_LIMIT_BIG_BYTES = 56 * 1024 * 1024


def _params(*semantics, vmem=VMEM_LIMIT_BYTES):
    return pltpu.CompilerParams(dimension_semantics=semantics, vmem_limit_bytes=vmem)


def _tile(n, target, align=8):
    for d in range(min(n, target), 0, -1):
        if n % d == 0 and d % align == 0:
            return d
    return n


def _adaln_kernel(c_ref, w_ref, b_ref, o_ref):
    c = c_ref[...]
    s = (c * jax.nn.sigmoid(c)).astype(BF16)
    o_ref[...] = jnp.dot(s, w_ref[...].astype(BF16), preferred_element_type=F32) + b_ref[...]


def _adaln(cond, w_mod, b_mod):
    rows, d = cond.shape
    n = w_mod.shape[1]
    tn = _tile(n, 512, 128)
    return pl.pallas_call(
        _adaln_kernel,
        grid=(n // tn,),
        in_specs=[pl.BlockSpec((rows, d), lambda j: (0, 0)),
                  pl.BlockSpec((d, tn), lambda j: (0, j)),
                  pl.BlockSpec((1, tn), lambda j: (0, j))],
        out_specs=pl.BlockSpec((rows, tn), lambda j: (0, j)),
        out_shape=jax.ShapeDtypeStruct((rows, n), F32),
        compiler_params=_params("parallel"),
        name="adaln",
    )(cond, w_mod, b_mod)


def _rms(x, g):
    return x * lax.rsqrt(jnp.mean(x * x, axis=-1, keepdims=True) + EPS) * g


def _modnorm_kernel(x_ref, g_ref, mod_ref, o_ref, *, shift_idx, scale_idx):
    mod = mod_ref[0]
    y = _rms(x_ref[...], g_ref[...])
    o_ref[...] = (y * (1.0 + mod[scale_idx:scale_idx + 1]) + mod[shift_idx:shift_idx + 1]).astype(o_ref.dtype)


def _mod_spec(mod, n_rows, tm):
    rows_per_mod = n_rows // mod.shape[0]
    return pl.BlockSpec((1,) + mod.shape[1:], lambda i: ((i * tm) // rows_per_mod, 0, 0))


def _modnorm(x, g, mod, shift_idx, scale_idx):
    n, d = x.shape
    tm = _tile(n // mod.shape[0], 256)
    return pl.pallas_call(
        functools.partial(_modnorm_kernel, shift_idx=shift_idx, scale_idx=scale_idx),
        grid=(n // tm,),
        in_specs=[pl.BlockSpec((tm, d), lambda i: (i, 0)),
                  pl.BlockSpec((1, d), lambda i: (0, 0)),
                  _mod_spec(mod, n, tm)],
        out_specs=pl.BlockSpec((tm, d), lambda i: (i, 0)),
        out_shape=jax.ShapeDtypeStruct((n, d), BF16),
        compiler_params=_params("parallel"),
        name="modnorm",
    )(x, g, mod)


def _rmsnorm_kernel(x_ref, g_ref, o_ref):
    o_ref[...] = _rms(x_ref[...], g_ref[...]).astype(o_ref.dtype)


def _rmsnorm(x, g):
    n, d = x.shape
    tm = _tile(n, 1024)
    return pl.pallas_call(
        _rmsnorm_kernel,
        grid=(n // tm,),
        in_specs=[pl.BlockSpec((tm, d), lambda i: (i, 0)),
                  pl.BlockSpec((1, d), lambda i: (0, 0))],
        out_specs=pl.BlockSpec((tm, d), lambda i: (i, 0)),
        out_shape=jax.ShapeDtypeStruct((n, d), BF16),
        compiler_params=_params("parallel"),
        name="rmsnorm",
    )(x, g)


def _epilogue(r, kind):
    if kind == "relu2":
        return jnp.square(jnp.maximum(r, 0.0))
    if kind == "sigmoid":
        return jax.nn.sigmoid(r)
    assert kind is None
    return r


def _bf16_panel(w_ref, scratch_ref):
    if scratch_ref is None:
        return w_ref[...]

    @pl.when(pl.program_id(1) == 0)
    def _():
        scratch_ref[...] = w_ref[...].astype(BF16)

    return scratch_ref[...]


def _mm_full_k_kernel(a_ref, w_ref, o_ref, wb_ref=None, *, epilogue):
    w = _bf16_panel(w_ref, wb_ref)
    r = jnp.dot(a_ref[...], w, preferred_element_type=F32)
    o_ref[...] = _epilogue(r, epilogue).astype(o_ref.dtype)


def _dot_nt(a, wt):
    return lax.dot_general(a, wt, (((1,), (1,)), ((), ())), preferred_element_type=F32)


def _mm_wt_kernel(a_ref, wt_ref, o_ref, wb_ref, *, epilogue):
    wt = _bf16_panel(wt_ref, wb_ref)
    o_ref[...] = _epilogue(_dot_nt(a_ref[...], wt), epilogue).astype(o_ref.dtype)


def _mm_wt(a, wt, row0, n, out_dtype, epilogue=None):
    m, kdim = a.shape
    tm = _tile(m, 512)
    tn = _tile(n, 1024 if jnp.dtype(out_dtype).itemsize == 2 else 512, 128)
    return pl.pallas_call(
        functools.partial(_mm_wt_kernel, epilogue=epilogue),
        grid=(n // tn, m // tm),
        in_specs=[pl.BlockSpec((tm, kdim), lambda j, i: (i, 0)),
                  pl.BlockSpec((pl.Element(tn), pl.Element(kdim)),
                               lambda j, i: (pl.multiple_of(row0 + j * tn, 8), 0))],
        out_specs=pl.BlockSpec((tm, tn), lambda j, i: (i, j)),
        out_shape=jax.ShapeDtypeStruct((m, n), out_dtype),
        scratch_shapes=[pltpu.VMEM((tn, kdim), BF16)],
        compiler_params=_params("parallel", "arbitrary", vmem=VMEM_LIMIT_BIG_BYTES),
        name="matmul_wt",
    )(a, wt)


def _mm_ck_kernel(a_ref, wt_ref, o_ref, wb_ref, *, kv_lora):
    half = QK_ROPE // 2

    @pl.when(pl.program_id(0) == 0)
    def _():
        kdim = wt_ref.shape[1]
        zeros = jnp.zeros((LANES - QK_ROPE, kdim), BF16)
        wb_ref[:kv_lora + QK_ROPE, :] = wt_ref[...].astype(BF16)
        wb_ref[kv_lora + QK_ROPE:kv_lora + LANES, :] = zeros
        wb_ref[kv_lora + LANES:kv_lora + LANES + half, :] = (-wt_ref[kv_lora + half:, :]).astype(BF16)
        wb_ref[kv_lora + LANES + half:kv_lora + LANES + QK_ROPE, :] = wt_ref[kv_lora:kv_lora + half, :].astype(BF16)
        wb_ref[kv_lora + LANES + QK_ROPE:, :] = zeros

    o_ref[...] = _dot_nt(a_ref[...], wb_ref[...])


def _mm_ck(a, wt, row0, kv_lora):
    m, kdim = a.shape
    tm = _tile(m, 512)
    rows, cols = kv_lora + QK_ROPE, kv_lora + 2 * LANES
    return pl.pallas_call(
        functools.partial(_mm_ck_kernel, kv_lora=kv_lora),
        grid=(m // tm,),
        in_specs=[pl.BlockSpec((tm, kdim), lambda i: (i, 0)),
                  pl.BlockSpec((pl.Element(rows), pl.Element(kdim)), lambda i: (row0, 0))],
        out_specs=pl.BlockSpec((tm, cols), lambda i: (i, 0)),
        out_shape=jax.ShapeDtypeStruct((m, cols), F32),
        scratch_shapes=[pltpu.VMEM((cols, kdim), BF16)],
        compiler_params=_params("arbitrary"),
        name="matmul_ck",
    )(a, wt)


def _mm_kernel(a_ref, w_ref, o_ref, acc_ref, *, epilogue):
    k = pl.program_id(2)

    @pl.when(k == 0)
    def _():
        acc_ref[...] = jnp.zeros_like(acc_ref)

    acc_ref[...] += jnp.dot(a_ref[...], w_ref[...], preferred_element_type=F32)

    @pl.when(k == pl.num_programs(2) - 1)
    def _():
        o_ref[...] = _epilogue(acc_ref[...], epilogue).astype(o_ref.dtype)


MM_MAX_FULL_K = 4096


def _mm(a, w, out_dtype, epilogue=None):
    m, kdim = a.shape
    n = w.shape[1]
    if kdim <= MM_MAX_FULL_K:
        cast_w = w.dtype != BF16
        tm, tn = _tile(m, 512 if kdim > 2048 else 1024), _tile(n, 1024, 128)
        return pl.pallas_call(
            functools.partial(_mm_full_k_kernel, epilogue=epilogue),
            grid=(n // tn, m // tm),
            in_specs=[pl.BlockSpec((tm, kdim), lambda j, i: (i, 0)),
                      pl.BlockSpec((kdim, tn), lambda j, i: (0, j))],
            out_specs=pl.BlockSpec((tm, tn), lambda j, i: (i, j)),
            out_shape=jax.ShapeDtypeStruct((m, n), out_dtype),
            scratch_shapes=[pltpu.VMEM((kdim, tn), BF16)] if cast_w else [],
            compiler_params=(_params("parallel", "arbitrary", vmem=VMEM_LIMIT_BIG_BYTES) if cast_w
                             else _params("parallel", "parallel")),
            name="matmul",
        )(a, w)
    assert w.dtype == BF16
    tm, tn, tk = _tile(m, 1024), _tile(n, 1024, 128), _tile(kdim, MM_MAX_FULL_K, 128)
    return pl.pallas_call(
        functools.partial(_mm_kernel, epilogue=epilogue),
        grid=(m // tm, n // tn, kdim // tk),
        in_specs=[pl.BlockSpec((tm, tk), lambda i, j, k: (i, k)),
                  pl.BlockSpec((tk, tn), lambda i, j, k: (k, j))],
        out_specs=pl.BlockSpec((tm, tn), lambda i, j, k: (i, j)),
        out_shape=jax.ShapeDtypeStruct((m, n), out_dtype),
        scratch_shapes=[pltpu.VMEM((tm, tn), F32)],
        compiler_params=_params("parallel", "parallel", "arbitrary"),
        name="matmul_k",
    )(a, w)


def _glu_merge_kernel(a_ref, wv_ref, wg_ref, sb_ref, oa_ref, o_ref):
    a = a_ref[...].astype(BF16)
    value = jnp.dot(a, wv_ref[...], preferred_element_type=F32)
    gate = jnp.dot(a, wg_ref[...], preferred_element_type=F32)
    o_b = value * jax.nn.sigmoid(gate)
    o_ref[...] = (oa_ref[...].astype(F32) + sb_ref[...].astype(F32) * o_b).astype(o_ref.dtype)


def _glu_merge(y, w_glu, gate_sig, oa_gated):
    m, kdim = y.shape
    d = w_glu.shape[1] // 2
    tm, tn = _tile(m, 1024), _tile(d, 512, 128)
    nj = d // tn
    return pl.pallas_call(
        _glu_merge_kernel,
        grid=(nj, m // tm),
        in_specs=[pl.BlockSpec((tm, kdim), lambda j, i: (i, 0)),
                  pl.BlockSpec((kdim, tn), lambda j, i: (0, j)),
                  pl.BlockSpec((kdim, tn), lambda j, i: (0, j + nj)),
                  pl.BlockSpec((tm, tn), lambda j, i: (i, j + nj)),
                  pl.BlockSpec((tm, tn), lambda j, i: (i, j))],
        out_specs=pl.BlockSpec((tm, tn), lambda j, i: (i, j)),
        out_shape=jax.ShapeDtypeStruct((m, d), BF16),
        compiler_params=_params("parallel", "parallel"),
        name="glu_merge",
    )(y, w_glu, w_glu, gate_sig, oa_gated)


def _qproj_kernel(a_ref, w_ref, *rest, scale, use_rope):
    if use_rope:
        cos_ref, sin_ref, o_ref = rest
    else:
        (o_ref,) = rest
    nb, hb, tl, _ = o_ref.shape
    a = a_ref[...]
    for h in range(hb):
        acc = jnp.dot(a, w_ref[h], preferred_element_type=F32)
        o_ref[:, h, :, :QK_NOPE] = (acc[:, :QK_NOPE] * scale).reshape(nb, tl, QK_NOPE).astype(o_ref.dtype)
        rope = acc[:, QK_NOPE:QK_NOPE + QK_ROPE]
        if use_rope:
            rope = rope * cos_ref[...] + acc[:, QK_NOPE + QK_ROPE:QK_NOPE + 2 * QK_ROPE] * sin_ref[...]
        o_ref[:, h, :, QK_NOPE:] = (rope * scale).reshape(nb, tl, QK_ROPE).astype(o_ref.dtype)


HEADS_PER_STEP = 4


def _row_blocking(batch, length, target):
    if length >= target:
        return 1, _tile(length, target)
    return _tile(batch, max(1, target // length), 1), length


def _head_out_spec(nb, hb, tl, length, width):
    per_batch = length // tl
    return pl.BlockSpec((nb, hb, tl, width), lambda i, h: (i // per_batch, h, i % per_batch, 0))


def _qproj(qn, w_heads, batch, length, rope_tables):
    n, kdim = qn.shape
    heads, _, cols = w_heads.shape
    use_rope = rope_tables is not None
    nb, tl = (1, _tile(length, 1024)) if use_rope else _row_blocking(batch, length, 1024)
    tm = nb * tl
    hb = _tile(heads, HEADS_PER_STEP, 1)
    in_specs = [pl.BlockSpec((tm, kdim), lambda i, h: (i, 0)),
                pl.BlockSpec((hb, kdim, cols), lambda i, h: (h, 0, 0))]
    args = [qn, w_heads]
    if use_rope:
        per_batch = length // tl
        spec = pl.BlockSpec((tl, QK_ROPE), lambda i, h: (i % per_batch, 0))
        in_specs += [spec, spec]
        args += list(rope_tables)
    width = QK_NOPE + QK_ROPE
    return pl.pallas_call(
        functools.partial(_qproj_kernel, scale=width ** -0.5, use_rope=use_rope),
        grid=(n // tm, heads // hb),
        in_specs=in_specs,
        out_specs=_head_out_spec(nb, hb, tl, length, width),
        out_shape=jax.ShapeDtypeStruct((batch, heads, length, width), BF16),
        compiler_params=_params("parallel", "parallel"),
        name="q_proj",
    )(*args)


def _kvprep_kernel(ck_ref, g_ref, *rest, kv_lora, use_rope):
    if use_rope:
        cos_ref, sin_ref, ckv_ref, kr_ref = rest
    else:
        ckv_ref, kr_ref = rest
    ckv_ref[...] = _rms(ck_ref[:, :kv_lora], g_ref[...])
    kr = ck_ref[:, kv_lora:kv_lora + QK_ROPE]
    if use_rope:
        kr = kr * cos_ref[...] + ck_ref[:, kv_lora + 128:kv_lora + 128 + QK_ROPE] * sin_ref[...]
    kr_ref[...] = kr


def _kvprep(ck, g_kv, length, rope_tables):
    n, cols = ck.shape
    kv_lora = cols - 256
    tm = _tile(length, 512)
    use_rope = rope_tables is not None
    in_specs = [pl.BlockSpec((tm, cols), lambda i: (i, 0)),
                pl.BlockSpec((1, kv_lora), lambda i: (0, 0))]
    args = [ck, g_kv]
    if use_rope:
        per_batch = length // tm
        spec = pl.BlockSpec((tm, QK_ROPE), lambda i: (i % per_batch, 0))
        in_specs += [spec, spec]
        args += list(rope_tables)
    return pl.pallas_call(
        functools.partial(_kvprep_kernel, kv_lora=kv_lora, use_rope=use_rope),
        grid=(n // tm,),
        in_specs=in_specs,
        out_specs=[pl.BlockSpec((tm, kv_lora), lambda i: (i, 0)),
                   pl.BlockSpec((tm, QK_ROPE), lambda i: (i, 0))],
        out_shape=[jax.ShapeDtypeStruct((n, kv_lora), F32),
                   jax.ShapeDtypeStruct((n, QK_ROPE), F32)],
        compiler_params=_params("parallel"),
        name="kv_prep",
    )(*args)


def _kvdec_kernel(c_ref, kr_ref, w_ref, k_ref, v_ref):
    nb, hb, tl, _ = k_ref.shape
    c = c_ref[...].astype(BF16)
    k_rope = kr_ref[...].reshape(nb, tl, QK_ROPE).astype(k_ref.dtype)
    for h in range(hb):
        acc = jnp.dot(c, w_ref[h], preferred_element_type=F32)
        k_ref[:, h, :, :QK_NOPE] = acc[:, :QK_NOPE].reshape(nb, tl, QK_NOPE).astype(k_ref.dtype)
        k_ref[:, h, :, QK_NOPE:] = k_rope
        v_ref[:, h] = acc[:, QK_NOPE:].reshape(nb, tl, V_HEAD).astype(v_ref.dtype)


def _kvdec(ckv, krope, w_heads, batch, length):
    n, kdim = ckv.shape
    heads, _, cols = w_heads.shape
    nb, tl = _row_blocking(batch, length, 1536)
    tm = nb * tl
    hb = _tile(heads, HEADS_PER_STEP, 1)
    return pl.pallas_call(
        _kvdec_kernel,
        grid=(n // tm, heads // hb),
        in_specs=[pl.BlockSpec((tm, kdim), lambda i, h: (i, 0)),
                  pl.BlockSpec((tm, QK_ROPE), lambda i, h: (i, 0)),
                  pl.BlockSpec((hb, kdim, cols), lambda i, h: (h, 0, 0))],
        out_specs=[_head_out_spec(nb, hb, tl, length, QK_NOPE + QK_ROPE),
                   _head_out_spec(nb, hb, tl, length, V_HEAD)],
        out_shape=[jax.ShapeDtypeStruct((batch, heads, length, QK_NOPE + QK_ROPE), BF16),
                   jax.ShapeDtypeStruct((batch, heads, length, V_HEAD), BF16)],
        compiler_params=_params("parallel", "parallel"),
        name="kv_decompress",
    )(ckv, krope, w_heads)


def _qk(q, k):
    return lax.dot_general(q, k, (((1,), (1,)), ((), ())), preferred_element_type=F32)


def _attn_kernel(q_ref, k_ref, v_ref, gate_ref, o_ref, *stat_refs, tk, nk, sub):
    hb, tq, dv = q_ref.shape[1], q_ref.shape[2], v_ref.shape[3]
    if nk > 1:
        m_ref, l_ref, acc_ref = stat_refs
    lanes = dv

    def write_out(h, rows, acc, l):
        cols = slice(h * dv, (h + 1) * dv)
        o_ref[0, rows, cols] = (acc / l * gate_ref[0, rows, cols].astype(F32)).astype(o_ref.dtype)

    for h in range(hb):
        for r in range(tq // sub):
            rows = pl.ds(r * sub, sub)
            q = q_ref[0, h, rows, :]
            s = _qk(q, k_ref[0, h, :tk, :])
            m = jnp.max(s, axis=-1, keepdims=True)
            p = jnp.exp(s - m)
            l = jnp.sum(p, axis=-1, keepdims=True)
            acc = jnp.dot(p.astype(BF16), v_ref[0, h, :tk, :], preferred_element_type=F32)
            if nk > 1:
                m_ref[h, rows, :] = jnp.broadcast_to(m, (sub, lanes))
                l_ref[h, rows, :] = jnp.broadcast_to(l, (sub, lanes))
                acc_ref[h, rows, :] = acc
            else:
                write_out(h, rows, acc, l)
        for j in range(1, nk):
            for r in range(tq // sub):
                rows = pl.ds(r * sub, sub)
                q = q_ref[0, h, rows, :]
                s = _qk(q, k_ref[0, h, j * tk:(j + 1) * tk, :])
                m_old = m_ref[h, rows, :]
                m_new = jnp.maximum(m_old, jnp.max(s, axis=-1, keepdims=True))
                alpha = jnp.exp(m_old - m_new)
                p = jnp.exp(s - m_new[:, :1])
                l = alpha * l_ref[h, rows, :] + jnp.sum(p, axis=-1, keepdims=True)
                acc = alpha * acc_ref[h, rows, :] + jnp.dot(
                    p.astype(BF16), v_ref[0, h, j * tk:(j + 1) * tk, :], preferred_element_type=F32)
                if j < nk - 1:
                    m_ref[h, rows, :] = m_new
                    l_ref[h, rows, :] = l
                    acc_ref[h, rows, :] = acc
                else:
                    write_out(h, rows, acc, l)


ATTN_MAX_KV_CHUNK = 2304
ATTN_STEP_ROWS = 2048


def _attention(q, k, v, gate):
    batch, heads, lq, dqk = q.shape
    lk, dv = k.shape[2], v.shape[3]
    tq = _tile(lq, 2048)
    sub = _tile(tq, 512)
    hb = _tile(heads, max(1, ATTN_STEP_ROWS // lq), 1) if tq == lq else 1
    tk = _tile(lk, ATTN_MAX_KV_CHUNK, 128) if lk > ATTN_MAX_KV_CHUNK else lk
    nk = lk // tk
    stats = [pltpu.VMEM((hb, tq, dv), F32)] * 3 if nk > 1 else []
    return pl.pallas_call(
        functools.partial(_attn_kernel, tk=tk, nk=nk, sub=sub),
        grid=(batch, heads // hb, lq // tq),
        in_specs=[pl.BlockSpec((1, hb, tq, dqk), lambda b, h, i: (b, h, i, 0)),
                  pl.BlockSpec((1, hb, lk, dqk), lambda b, h, i: (b, h, 0, 0)),
                  pl.BlockSpec((1, hb, lk, dv), lambda b, h, i: (b, h, 0, 0)),
                  pl.BlockSpec((1, tq, hb * dv), lambda b, h, i: (b, i, h))],
        out_specs=pl.BlockSpec((1, tq, hb * dv), lambda b, h, i: (b, i, h)),
        out_shape=jax.ShapeDtypeStruct((batch, lq, heads * dv), BF16),
        scratch_shapes=stats,
        compiler_params=_params("parallel", "parallel", "arbitrary"),
        name="attention",
    )(q, k, v, gate)


def _gelu_tanh(x):
    return 0.5 * x * (1.0 + jnp.tanh(math.sqrt(2.0 / math.pi) * (x + 0.044715 * (x * x * x))))


def _s5_kernel(u_ref, w_ref, g_ref, e_ref, d_ref, ar_ref, ai_ref, h0_ref, y_ref, fin_ref, *, n_chunks):
    u = u_ref[0]
    rows = u.shape[0]
    batch = rows // n_chunks
    half = g_ref.shape[2] // 2
    g = jnp.dot(u, g_ref[0], preferred_element_type=F32)
    h0 = jnp.broadcast_to(h0_ref[0][:, None, :], (batch, n_chunks, 2 * half)).reshape(rows, 2 * half)
    chunk = lax.broadcasted_iota(jnp.int32, (rows, half), 0) % n_chunks
    ar = ar_ref[0]
    ai = ai_ref[0]

    def cmul(x, step, lo):
        swapped = pltpu.roll(x, half // 2, axis=1)
        return x * ar[step:step + 1, lo:lo + half] + swapped * ai[step:step + 1, lo:lo + half]

    def scan(gd, h0d, lo, backward):
        if backward:
            z = jnp.where(chunk == n_chunks - 1, h0d, pltpu.roll(gd, rows - 1, axis=0))
        else:
            z = jnp.where(chunk == 0, h0d, pltpu.roll(gd, 1, axis=0))
        step, dist = 0, 1
        while dist < n_chunks:
            if backward:
                moved = jnp.where(chunk < n_chunks - dist, pltpu.roll(z, rows - dist, axis=0), 0.0)
            else:
                moved = jnp.where(chunk >= dist, pltpu.roll(z, dist, axis=0), 0.0)
            z = z + cmul(moved, step, lo)
            step, dist = step + 1, dist * 2
        return z

    gf, gb = g[:, :half], g[:, half:]
    pf = scan(gf, h0[:, :half], 0, False)
    pb = scan(gb, h0[:, half:], half, True)
    p = jnp.concatenate([pf, pb], axis=1).astype(BF16)
    y = (jnp.dot(u, w_ref[0], preferred_element_type=F32)
         + jnp.dot(p, e_ref[0], preferred_element_type=F32)
         + u.astype(F32) * d_ref[0])
    y_ref[0] = _gelu_tanh(y).astype(y_ref.dtype)
    def pick(x, which):
        return jnp.sum(jnp.where(chunk == which, x, 0.0).reshape(batch, n_chunks, half), axis=1)

    ff = pick(cmul(pf, 0, 0) + gf, n_chunks - 1)
    fb = pick(cmul(pb, 0, half) + gb, 0)
    fin_ref[0] = jnp.concatenate([ff, fb], axis=1)


def _s5(u_rows, ops, h0, n_chunks):
    groups, rows, width = u_rows.shape
    batch = rows // n_chunks
    wt, gt, et, dvec, ar, ai = ops
    lanes = gt.shape[2]

    def per_group(*shape):
        return pl.BlockSpec((1,) + shape, lambda g: (g,) + (0,) * len(shape))

    return pl.pallas_call(
        functools.partial(_s5_kernel, n_chunks=n_chunks),
        grid=(groups,),
        in_specs=[per_group(rows, width), per_group(width, width), per_group(width, lanes),
                  per_group(lanes, width), per_group(1, width), per_group(*ar.shape[1:]),
                  per_group(*ai.shape[1:]), per_group(batch, lanes)],
        out_specs=[per_group(rows, width), per_group(batch, lanes)],
        out_shape=[jax.ShapeDtypeStruct((groups, rows, width), BF16),
                   jax.ShapeDtypeStruct((groups, batch, lanes), F32)],
        compiler_params=_params("parallel"),
        name="s5",
    )(u_rows, wt, gt, et, dvec, ar, ai, h0)


def _s5_operators(lam_re, lam_im, log_dt, b_re, b_im, c_re, c_im, d_skip, chunk, max_chunks):
    hi = lax.Precision.HIGHEST
    groups, p_dim, j_dim = b_re.shape[1:]
    t = chunk
    lam = lax.complex(lam_re, lam_im)
    z = lam * jnp.exp(log_dt)[..., None]
    a = jnp.exp(z)
    b_bar = ((a - 1.0) / lam)[..., None] * lax.complex(b_re, b_im)
    c_mat = lax.complex(c_re, c_im)
    k_idx = jnp.arange(t + 1, dtype=F32)
    apow = jnp.exp(z[None] * k_idx[:, None, None, None])

    cb = c_mat[..., None] * b_bar[:, :, None]
    ap = jnp.moveaxis(apow[:t], 1, 0)
    kern = (jnp.einsum('dtgp,dgopi->dtgoi', ap.real, cb.real, precision=hi)
            - jnp.einsum('dtgp,dgopi->dtgoi', ap.imag, cb.imag, precision=hi))
    zeros = jnp.zeros((t - 1,) + kern.shape[2:], F32)
    k_lag = jnp.concatenate([zeros, kern[0]], axis=0) + jnp.concatenate([kern[1][::-1], zeros], axis=0)
    k_lag = k_lag.transpose(1, 3, 0, 2).reshape(groups, j_dim, (2 * t - 1) * j_dim).astype(BF16)
    w_all = jnp.stack([k_lag[:, :, (t - 1 - j) * j_dim:(2 * t - 1 - j) * j_dim] for j in range(t)], axis=1)
    w_all = w_all.reshape(groups, t * j_dim, t * j_dim)

    apow_tp = jnp.moveaxis(apow, 0, 2)
    b_t = jnp.swapaxes(b_bar, -1, -2)
    inj_f = apow_tp[0][:, t - 1::-1][:, :, None, :] * b_t[0][:, None]
    inj_b = apow_tp[1][:, :t][:, :, None, :] * b_t[1][:, None]
    g_all = jnp.concatenate([inj_f.real, inj_f.imag, inj_b.real, inj_b.imag], axis=-1)
    g_all = g_all.reshape(groups, t * j_dim, 4 * p_dim)

    apow_pt = jnp.moveaxis(apow, 0, -1)
    c_t = jnp.swapaxes(c_mat, -1, -2)
    out_f = c_t[0][:, :, None, :] * apow_pt[0][:, :, 1:t + 1][..., None]
    out_b = c_t[1][:, :, None, :] * apow_pt[1][:, :, t:0:-1][..., None]
    e_all = jnp.concatenate([out_f.real, -out_f.imag, out_b.real, -out_b.imag], axis=1)
    e_all = e_all.reshape(groups, 4 * p_dim, t * j_dim)

    d_row = jnp.tile(d_skip.reshape(groups, 1, j_dim), (1, t, 1)).reshape(groups, 1, t * j_dim)

    n_steps = max(1, (max_chunks - 1).bit_length())
    steps = (t * 2 ** jnp.arange(n_steps)).astype(F32)
    tp = jnp.exp(z[None] * steps[:, None, None, None])
    a_r = jnp.concatenate([tp[:, 0].real, tp[:, 0].real, tp[:, 1].real, tp[:, 1].real], axis=-1)
    a_i = jnp.concatenate([-tp[:, 0].imag, tp[:, 0].imag, -tp[:, 1].imag, tp[:, 1].imag], axis=-1)
    a_r = a_r.transpose(1, 0, 2)
    a_i = a_i.transpose(1, 0, 2)
    return w_all, g_all.astype(BF16), e_all.astype(BF16), d_row, a_r, a_i


def _postmix_kernel(x_ref, y_ref, gpost_ref, gpre_ref, mod_ref, x1_ref, h_ref):
    mod = mod_ref[0]
    x1 = x_ref[...] + mod[2:3] * _rms(y_ref[...].astype(F32), gpost_ref[...])
    x1_ref[...] = x1
    h_ref[...] = (_rms(x1, gpre_ref[...]) * (1.0 + mod[4:5]) + mod[3:4]).astype(h_ref.dtype)


def _postmix(x, y, g_post, g_pre, mod):
    n, d = x.shape
    tm = _tile(n // mod.shape[0], 256)
    row = pl.BlockSpec((tm, d), lambda i: (i, 0))
    gain = pl.BlockSpec((1, d), lambda i: (0, 0))
    return pl.pallas_call(
        _postmix_kernel,
        grid=(n // tm,),
        in_specs=[row, row, gain, gain, _mod_spec(mod, n, tm)],
        out_specs=[row, row],
        out_shape=[jax.ShapeDtypeStruct((n, d), F32), jax.ShapeDtypeStruct((n, d), BF16)],
        compiler_params=_params("parallel"),
        name="post_mix",
    )(x, y, g_post, g_pre, mod)


def _postmlp_kernel(x_ref, y_ref, g_ref, mod_ref, o_ref):
    o_ref[...] = x_ref[...] + mod_ref[0][5:6] * _rms(y_ref[...].astype(F32), g_ref[...])


def _postmlp(x, y, g_post, mod):
    n, d = x.shape
    tm = _tile(n // mod.shape[0], 256)
    row = pl.BlockSpec((tm, d), lambda i: (i, 0))
    return pl.pallas_call(
        _postmlp_kernel,
        grid=(n // tm,),
        in_specs=[row, row, pl.BlockSpec((1, d), lambda i: (0, 0)), _mod_spec(mod, n, tm)],
        out_specs=row,
        out_shape=jax.ShapeDtypeStruct((n, d), F32),
        compiler_params=_params("parallel"),
        name="post_mlp",
    )(x, y, g_post, mod)


def _axial_rope(n_tokens):
    rows = n_tokens // GRID_W
    row = jnp.repeat(jnp.arange(rows, dtype=F32), GRID_W)
    col = jnp.tile(jnp.arange(GRID_W, dtype=F32), rows)
    n_freq = QK_ROPE // 4
    inv_freq = ROPE_THETA ** (-jnp.arange(n_freq, dtype=F32) / n_freq)
    ang = jnp.concatenate([row[:, None] * inv_freq, col[:, None] * inv_freq], axis=-1)
    ang = jnp.concatenate([ang, ang], axis=-1)
    return jnp.cos(ang), jnp.sin(ang)


def _rotate_half_cols(w):
    half = w.shape[-1] // 2
    return jnp.concatenate([-w[..., half:], w[..., :half]], axis=-1)


def _prepare_weights(w_in, w_uq, w_ukv, w_glu, w_out, w_ff1, w_ff2, q_lora, kv_lora):
    uq = w_uq.reshape(q_lora, N_HEADS, QK_NOPE + QK_ROPE)
    uq_plain = uq.transpose(1, 0, 2)
    uq_rope = jnp.concatenate([uq, _rotate_half_cols(uq[..., QK_NOPE:])], axis=-1).transpose(1, 0, 2)
    ukv = w_ukv.reshape(kv_lora, N_HEADS, QK_NOPE + V_HEAD).transpose(1, 0, 2)
    cast = lambda w: w.astype(BF16)
    return dict(w_in_t=jnp.transpose(w_in),
                uq_plain=cast(uq_plain), uq_rope=cast(uq_rope), ukv=cast(ukv), w_glu=cast(w_glu),
                w_out=w_out, w_ff1=w_ff1, w_ff2=cast(w_ff2))


def _slab_permutation(j_dim):
    gb = LANES // j_dim
    src = np.arange(gb * LANES)
    i8, g8, j = src // LANES, (src % LANES) // j_dim, src % j_dim
    p = np.zeros((gb * LANES, gb * LANES), np.float32)
    p[src, g8 * LANES + i8 * j_dim + j] = 1.0
    return jnp.asarray(p, BF16)


def _to_rows_kernel(u_ref, p_ref, o_ref, *, chunk):
    gb, rows, _ = o_ref.shape
    tiles = [u_ref[pl.ds(i, rows, stride=chunk), :].astype(BF16) for i in range(chunk)]
    for k in range(chunk // gb):
        z = jnp.concatenate(tiles[gb * k:gb * (k + 1)], axis=1)
        uk = jnp.dot(z, p_ref[...], preferred_element_type=F32).astype(o_ref.dtype)
        for g in range(gb):
            o_ref[g, :, k * LANES:(k + 1) * LANES] = uk[:, g * LANES:(g + 1) * LANES]


def _to_group_rows(u, perm, groups, j_dim, chunk):
    n = u.shape[0]
    gb = LANES // j_dim
    rows = _tile(n // chunk, 256, 16)
    return pl.pallas_call(
        functools.partial(_to_rows_kernel, chunk=chunk),
        grid=(groups // gb, n // (rows * chunk)),
        in_specs=[pl.BlockSpec((rows * chunk, LANES), lambda g, r: (r, g)),
                  pl.BlockSpec(perm.shape, lambda g, r: (0, 0))],
        out_specs=pl.BlockSpec((gb, rows, chunk * j_dim), lambda g, r: (g, r, 0)),
        out_shape=jax.ShapeDtypeStruct((groups, n // chunk, chunk * j_dim), BF16),
        compiler_params=_params("parallel", "parallel"),
        name="to_group_rows",
    )(u, perm)


def _from_rows_kernel(y_ref, p_ref, o_ref, *, chunk):
    gb, rows, _ = y_ref.shape
    for k in range(chunk // gb):
        yk = jnp.concatenate([y_ref[g, :, k * LANES:(k + 1) * LANES] for g in range(gb)], axis=1)
        z = jnp.dot(yk, p_ref[...], preferred_element_type=F32)
        for i8 in range(gb):
            o_ref[pl.ds(gb * k + i8, rows, stride=chunk), :] = z[:, i8 * LANES:(i8 + 1) * LANES]


def _from_group_rows(y, perm_t, groups, j_dim, chunk):
    n_rows = y.shape[1]
    gb = LANES // j_dim
    rows = _tile(n_rows, 256, 16)
    return pl.pallas_call(
        functools.partial(_from_rows_kernel, chunk=chunk),
        grid=(groups // gb, n_rows // rows),
        in_specs=[pl.BlockSpec((gb, rows, chunk * j_dim), lambda g, r: (g, r, 0)),
                  pl.BlockSpec(perm_t.shape, lambda g, r: (0, 0))],
        out_specs=pl.BlockSpec((rows * chunk, LANES), lambda g, r: (r, g)),
        out_shape=jax.ShapeDtypeStruct((n_rows * chunk, groups * j_dim), F32),
        compiler_params=_params("parallel", "parallel"),
        name="from_group_rows",
    )(y, perm_t)


def _trunk(x, mod, wts, gains, s5_ops, s5_dims, ctx):
    batch, length, d_model = x.shape
    n = batch * length
    groups, p_dim, j_dim, chunk = s5_dims
    x2 = x.reshape(n, d_model)
    rope = None if ctx is None else ctx[3]

    h = _modnorm(x2, gains['g_pre_mix'], mod, 0, 1)
    q_lora, kv_lora = gains['g_q_lat'].shape[1], gains['g_kv_lat'].shape[1]
    s5_width = groups * j_dim
    wt = wts['w_in_t']
    q_lat = _mm_wt(h, wt, 0, q_lora, F32)
    ck = _mm_ck(h, wt, q_lora, kv_lora)
    u = _mm_wt(h, wt, q_lora + kv_lora + QK_ROPE, s5_width, F32)
    gate_sig = _mm_wt(h, wt, q_lora + kv_lora + QK_ROPE + s5_width, 2 * d_model, BF16, epilogue="sigmoid")

    qn = _rmsnorm(q_lat, gains['g_q_lat'])
    q = _qproj(qn, wts['uq_plain'] if rope is None else wts['uq_rope'], batch, length, rope)
    ckv_n, k_rope = _kvprep(ck, gains['g_kv_lat'], length, rope)
    if ctx is None:
        ckv_all, kr_all, lk = ckv_n, k_rope, length
    else:
        lk = ctx[0].shape[1] + length
        ckv_all = jnp.concatenate([ctx[0], ckv_n.reshape(batch, length, -1)], axis=1).reshape(batch * lk, -1)
        kr_all = jnp.concatenate([ctx[1], k_rope.reshape(batch, length, -1)], axis=1).reshape(batch * lk, -1)
    k, v = _kvdec(ckv_all, kr_all, wts['ukv'], batch, lk)
    oa_gated = _attention(q, k, v, gate_sig.reshape(batch, length, -1)).reshape(n, -1)

    if ctx is None:
        h0 = jnp.zeros((groups, batch, 4 * p_dim), F32)
    else:
        h0 = ctx[2].transpose(3, 0, 1, 2, 4).reshape(groups, batch, 4 * p_dim)
    perm = _slab_permutation(j_dim)
    u_rows = _to_group_rows(u, perm, groups, j_dim, chunk)
    y_rows, s5_fin = _s5(u_rows, s5_ops, h0, length // chunk)
    y = _from_group_rows(y_rows, perm.T, groups, j_dim, chunk)
    m = _glu_merge(y, wts['w_glu'], gate_sig, oa_gated)
    mixed = _mm(m, wts['w_out'], BF16)
    x1, h2 = _postmix(x2, mixed, gains['g_post_mix'], gains['g_pre_mlp'], mod)
    a1 = _mm(h2, wts['w_ff1'], BF16, epilogue="relu2")
    a2 = _mm(a1, wts['w_ff2'], BF16)
    y_out = _postmlp(x1, a2, gains['g_post_mlp'], mod).reshape(batch, length, d_model)
    s5_fin = s5_fin.reshape(groups, batch, 2, 2, p_dim).transpose(1, 2, 3, 0, 4)
    return y_out, ckv_n.reshape(batch, length, -1), k_rope.reshape(batch, length, -1), s5_fin


def kernel(x_prompt, x_sample, c, cache_ckv, cache_krope, state_s5, c_ctx, w_mod, b_mod, g_pre_mix, w_in, g_q_lat, g_kv_lat, w_uq, w_ukv, s5_lam_re, s5_lam_im, s5_log_dt, s5_b_re, s5_b_im, s5_c_re, s5_c_im, s5_d, w_glu, w_out, g_post_mix, g_pre_mlp, w_ff1, w_ff2, g_post_mlp):
    depth = w_mod.shape[0]
    d_model = x_prompt.shape[-1]
    dec_batch, dec_seq = x_sample.shape[:2]
    q_lora, kv_lora = g_q_lat.shape[1], g_kv_lat.shape[1]
    groups, p_dim, j_dim = s5_b_re.shape[2:]
    chunk = min(S5_CHUNK, x_prompt.shape[1] // 8)
    rope = _axial_rope(dec_seq)

    cond = jnp.concatenate([c_ctx[None, :], c], axis=0)
    cond = jnp.pad(cond, ((0, -cond.shape[0] % 8), (0, 0)))

    xp, xs = x_prompt, x_sample
    new_ckv, new_krope, new_s5 = [], [], []
    for l in range(depth):
        mods = _adaln(cond, w_mod[l], b_mod[l][None, :])
        mod_p = mods[0:1].reshape(1, -1, d_model)
        mod_s = mods[1:1 + dec_batch].reshape(dec_batch, -1, d_model)
        wts = _prepare_weights(w_in[l], w_uq[l], w_ukv[l], w_glu[l], w_out[l], w_ff1[l], w_ff2[l],
                               q_lora, kv_lora)
        gains = dict(g_pre_mix=g_pre_mix[l][None], g_q_lat=g_q_lat[l][None], g_kv_lat=g_kv_lat[l][None],
                     g_post_mix=g_post_mix[l][None], g_pre_mlp=g_pre_mlp[l][None], g_post_mlp=g_post_mlp[l][None])
        max_chunks = max(x_prompt.shape[1], dec_seq) // chunk
        s5_ops = _s5_operators(s5_lam_re[l], s5_lam_im[l], s5_log_dt[l], s5_b_re[l], s5_b_im[l],
                               s5_c_re[l], s5_c_im[l], s5_d[l], chunk, max_chunks)
        s5_dims = (groups, p_dim, j_dim, chunk)

        xp, ckv_n, k_rope_p, s5_last = _trunk(xp, mod_p, wts, gains, s5_ops, s5_dims, None)
        new_ckv.append(ckv_n)
        new_krope.append(k_rope_p)
        new_s5.append(s5_last)
        ctx = (cache_ckv[:, l], cache_krope[:, l], state_s5[:, l], rope)
        xs, _, _, _ = _trunk(xs, mod_s, wts, gains, s5_ops, s5_dims, ctx)
    return (xp, xs, jnp.stack(new_ckv, axis=1), jnp.stack(new_krope, axis=1),
            jnp.stack(new_s5, axis=1).astype(x_prompt.dtype))
```

```python
import functools
import math

import jax
import jax.numpy as jnp
import numpy as np
from jax import lax
from jax.experimental import pallas as pl
from jax.experimental.pallas import tpu as pltpu

N_HEADS = 32
QK_NOPE = 128
QK_ROPE = 64
V_HEAD = 128
GRID_W = 64
ROPE_THETA = 10000.0
EPS = 1e-6
S5_CHUNK = 32
LANES = 128
VMEM_LIMIT_BYTES = 48 * 1024 * 1024

F32 = jnp.float32
BF16 = jnp.bfloat16


VMEM_LIMIT_BIG_BYTES = 56 * 1024 * 1024


def _params(*semantics, vmem=VMEM_LIMIT_BYTES):
    return pltpu.CompilerParams(dimension_semantics=semantics, vmem_limit_bytes=vmem)


def _tile(n, target, align=8):
    for d in range(min(n, target), 0, -1):
        if n % d == 0 and d % align == 0:
            return d
    return n


def _adaln_kernel(c_ref, w_ref, b_ref, o_ref):
    c = c_ref[...]
    s = (c * jax.nn.sigmoid(c)).astype(BF16)
    o_ref[...] = jnp.dot(s, w_ref[...].astype(BF16), preferred_element_type=F32) + b_ref[...]


def _adaln(cond, w_mod, b_mod):
    rows, d = cond.shape
    n = w_mod.shape[1]
    tn = _tile(n, 512, 128)
    return pl.pallas_call(
        _adaln_kernel,
        grid=(n // tn,),
        in_specs=[pl.BlockSpec((rows, d), lambda j: (0, 0)),
                  pl.BlockSpec((d, tn), lambda j: (0, j)),
                  pl.BlockSpec((1, tn), lambda j: (0, j))],
        out_specs=pl.BlockSpec((rows, tn), lambda j: (0, j)),
        out_shape=jax.ShapeDtypeStruct((rows, n), F32),
        compiler_params=_params("parallel"),
        name="adaln",
    )(cond, w_mod, b_mod)


def _rms(x, g):
    return x * lax.rsqrt(jnp.mean(x * x, axis=-1, keepdims=True) + EPS) * g


def _modnorm_kernel(x_ref, g_ref, mod_ref, o_ref, *, shift_idx, scale_idx):
    mod = mod_ref[0]
    y = _rms(x_ref[...], g_ref[...])
    o_ref[...] = (y * (1.0 + mod[scale_idx:scale_idx + 1]) + mod[shift_idx:shift_idx + 1]).astype(o_ref.dtype)


def _mod_spec(mod, n_rows, tm):
    rows_per_mod = n_rows // mod.shape[0]
    return pl.BlockSpec((1,) + mod.shape[1:], lambda i: ((i * tm) // rows_per_mod, 0, 0))


def _modnorm(x, g, mod, shift_idx, scale_idx):
    n, d = x.shape
    tm = _tile(n // mod.shape[0], 256)
    return pl.pallas_call(
        functools.partial(_modnorm_kernel, shift_idx=shift_idx, scale_idx=scale_idx),
        grid=(n // tm,),
        in_specs=[pl.BlockSpec((tm, d), lambda i: (i, 0)),
                  pl.BlockSpec((1, d), lambda i: (0, 0)),
                  _mod_spec(mod, n, tm)],
        out_specs=pl.BlockSpec((tm, d), lambda i: (i, 0)),
        out_shape=jax.ShapeDtypeStruct((n, d), BF16),
        compiler_params=_params("parallel"),
        name="modnorm",
    )(x, g, mod)


def _epilogue(r, kind):
    if kind == "relu2":
        return jnp.square(jnp.maximum(r, 0.0))
    if kind == "sigmoid":
        return jax.nn.sigmoid(r)
    assert kind is None
    return r


def _bf16_panel(w_ref, scratch_ref):
    if scratch_ref is None:
        return w_ref[...]

    @pl.when(pl.program_id(1) == 0)
    def _():
        scratch_ref[...] = w_ref[...].astype(BF16)

    return scratch_ref[...]


def _mm_full_k_kernel(a_ref, w_ref, o_ref, wb_ref=None, *, epilogue):
    w = _bf16_panel(w_ref, wb_ref)
    r = jnp.dot(a_ref[...], w, preferred_element_type=F32)
    o_ref[...] = _epilogue(r, epilogue).astype(o_ref.dtype)


def _dot_nt(a, wt):
    return lax.dot_general(a, wt, (((1,), (1,)), ((), ())), preferred_element_type=F32)


def _mm_wt_kernel(a_ref, wt_ref, *rest, epilogue):
    r = _dot_nt(a_ref[...], wt_ref[...])
    if epilogue == "rmsnorm":
        g_ref, o_ref = rest
        r = _rms(r, g_ref[...])
    else:
        (o_ref,) = rest
        r = _epilogue(r, epilogue)
    o_ref[...] = r.astype(o_ref.dtype)


def _mm_wt(a, wt, row0, n, out_dtype, epilogue=None, gain=None):
    m, kdim = a.shape
    tm = _tile(m, 512)
    tn = n if epilogue == "rmsnorm" else _tile(n, 1024, 128)
    in_specs = [pl.BlockSpec((tm, kdim), lambda j, i: (i, 0)),
                pl.BlockSpec((pl.Element(tn), pl.Element(kdim)),
                             lambda j, i: (pl.multiple_of(row0 + j * tn, 16), 0))]
    args = [a, wt]
    if epilogue == "rmsnorm":
        in_specs.append(pl.BlockSpec((1, n), lambda j, i: (0, 0)))
        args.append(gain)
    return pl.pallas_call(
        functools.partial(_mm_wt_kernel, epilogue=epilogue),
        grid=(n // tn, m // tm),
        in_specs=in_specs,
        out_specs=pl.BlockSpec((tm, tn), lambda j, i: (i, j)),
        out_shape=jax.ShapeDtypeStruct((m, n), out_dtype),
        compiler_params=_params("parallel", "parallel"),
        name="matmul_wt",
    )(*args)


def _mm_ck_kernel(a_ref, wt_ref, g_ref, *rest, kv_lora, use_rope):
    if use_rope:
        cos_ref, sin_ref, ckv_ref, kr_ref, wb_ref = rest
    else:
        ckv_ref, kr_ref, wb_ref = rest
    half = QK_ROPE // 2

    @pl.when(pl.program_id(0) == 0)
    def _():
        wb_ref[:kv_lora + QK_ROPE, :] = wt_ref[...]
        wb_ref[kv_lora + QK_ROPE:kv_lora + QK_ROPE + half, :] = -wt_ref[kv_lora + half:, :]
        wb_ref[kv_lora + QK_ROPE + half:, :] = wt_ref[kv_lora:kv_lora + half, :]

    acc = _dot_nt(a_ref[...], wb_ref[...])
    ckv_ref[...] = _rms(acc[:, :kv_lora], g_ref[...])
    k_rope = acc[:, kv_lora:kv_lora + QK_ROPE]
    if use_rope:
        k_rope = k_rope * cos_ref[...] + acc[:, kv_lora + QK_ROPE:] * sin_ref[...]
    kr_ref[...] = k_rope


def _mm_ck(a, wt, row0, g_kv, length, rope_tables):
    m, kdim = a.shape
    kv_lora = g_kv.shape[1]
    use_rope = rope_tables is not None
    tm = _tile(length, 512) if use_rope else _tile(m, 512)
    rows = kv_lora + QK_ROPE
    in_specs = [pl.BlockSpec((tm, kdim), lambda i: (i, 0)),
                pl.BlockSpec((pl.Element(rows), pl.Element(kdim)), lambda i: (row0, 0)),
                pl.BlockSpec((1, kv_lora), lambda i: (0, 0))]
    args = [a, wt, g_kv]
    if use_rope:
        per_batch = length // tm
        spec = pl.BlockSpec((tm, QK_ROPE), lambda i: (i % per_batch, 0))
        in_specs += [spec, spec]
        args += list(rope_tables)
    return pl.pallas_call(
        functools.partial(_mm_ck_kernel, kv_lora=kv_lora, use_rope=use_rope),
        grid=(m // tm,),
        in_specs=in_specs,
        out_specs=[pl.BlockSpec((tm, kv_lora), lambda i: (i, 0)),
                   pl.BlockSpec((tm, QK_ROPE), lambda i: (i, 0))],
        out_shape=[jax.ShapeDtypeStruct((m, kv_lora), F32),
                   jax.ShapeDtypeStruct((m, QK_ROPE), F32)],
        scratch_shapes=[pltpu.VMEM((rows + QK_ROPE, kdim), BF16)],
        compiler_params=_params("arbitrary"),
        name="matmul_ck",
    )(*args)


def _mm_kernel(a_ref, w_ref, o_ref, acc_ref, *, epilogue):
    k = pl.program_id(2)

    @pl.when(k == 0)
    def _():
        acc_ref[...] = jnp.zeros_like(acc_ref)

    acc_ref[...] += jnp.dot(a_ref[...], w_ref[...], preferred_element_type=F32)

    @pl.when(k == pl.num_programs(2) - 1)
    def _():
        o_ref[...] = _epilogue(acc_ref[...], epilogue).astype(o_ref.dtype)


MM_MAX_FULL_K = 4096


def _mm(a, w, out_dtype, epilogue=None):
    m, kdim = a.shape
    n = w.shape[1]
    if kdim <= MM_MAX_FULL_K:
        cast_w = w.dtype != BF16
        tm, tn = _tile(m, 512 if kdim > 2048 else 1024), _tile(n, 1024, 128)
        return pl.pallas_call(
            functools.partial(_mm_full_k_kernel, epilogue=epilogue),
            grid=(n // tn, m // tm),
            in_specs=[pl.BlockSpec((tm, kdim), lambda j, i: (i, 0)),
                      pl.BlockSpec((kdim, tn), lambda j, i: (0, j))],
            out_specs=pl.BlockSpec((tm, tn), lambda j, i: (i, j)),
            out_shape=jax.ShapeDtypeStruct((m, n), out_dtype),
            scratch_shapes=[pltpu.VMEM((kdim, tn), BF16)] if cast_w else [],
            compiler_params=(_params("parallel", "arbitrary", vmem=VMEM_LIMIT_BIG_BYTES) if cast_w
                             else _params("parallel", "parallel")),
            name="matmul",
        )(a, w)
    assert w.dtype == BF16
    tm, tn, tk = _tile(m, 1024), _tile(n, 1024, 128), _tile(kdim, MM_MAX_FULL_K, 128)
    return pl.pallas_call(
        functools.partial(_mm_kernel, epilogue=epilogue),
        grid=(m // tm, n // tn, kdim // tk),
        in_specs=[pl.BlockSpec((tm, tk), lambda i, j, k: (i, k)),
                  pl.BlockSpec((tk, tn), lambda i, j, k: (k, j))],
        out_specs=pl.BlockSpec((tm, tn), lambda i, j, k: (i, j)),
        out_shape=jax.ShapeDtypeStruct((m, n), out_dtype),
        scratch_shapes=[pltpu.VMEM((tm, tn), F32)],
        compiler_params=_params("parallel", "parallel", "arbitrary"),
        name="matmul_k",
    )(a, w)


def _glu_merge_kernel(a_ref, wv_ref, wg_ref, sb_ref, oa_ref, o_ref):
    a = a_ref[...].astype(BF16)
    value = jnp.dot(a, wv_ref[...], preferred_element_type=F32)
    gate = jnp.dot(a, wg_ref[...], preferred_element_type=F32)
    o_b = value * jax.nn.sigmoid(gate)
    o_ref[...] = (oa_ref[...].astype(F32) + sb_ref[...].astype(F32) * o_b).astype(o_ref.dtype)


def _glu_merge(y, w_glu, gate_sig, oa_gated):
    m, kdim = y.shape
    d = w_glu.shape[1] // 2
    tm, tn = _tile(m, 1024), _tile(d, 512, 128)
    nj = d // tn
    return pl.pallas_call(
        _glu_merge_kernel,
        grid=(nj, m // tm),
        in_specs=[pl.BlockSpec((tm, kdim), lambda j, i: (i, 0)),
                  pl.BlockSpec((kdim, tn), lambda j, i: (0, j)),
                  pl.BlockSpec((kdim, tn), lambda j, i: (0, j + nj)),
                  pl.BlockSpec((tm, tn), lambda j, i: (i, j + nj)),
                  pl.BlockSpec((tm, tn), lambda j, i: (i, j))],
        out_specs=pl.BlockSpec((tm, tn), lambda j, i: (i, j)),
        out_shape=jax.ShapeDtypeStruct((m, d), BF16),
        compiler_params=_params("parallel", "parallel"),
        name="glu_merge",
    )(y, w_glu, w_glu, gate_sig, oa_gated)


def _qproj_kernel(a_ref, w_ref, *rest, scale, use_rope):
    if use_rope:
        cos_ref, sin_ref, o_ref = rest
    else:
        (o_ref,) = rest
    nb, hb, tl, _ = o_ref.shape
    a = a_ref[...]
    for h in range(hb):
        acc = jnp.dot(a, w_ref[h], preferred_element_type=F32)
        o_ref[:, h, :, :QK_NOPE] = (acc[:, :QK_NOPE] * scale).reshape(nb, tl, QK_NOPE).astype(o_ref.dtype)
        rope = acc[:, QK_NOPE:QK_NOPE + QK_ROPE]
        if use_rope:
            rope = rope * cos_ref[...] + acc[:, QK_NOPE + QK_ROPE:QK_NOPE + 2 * QK_ROPE] * sin_ref[...]
        o_ref[:, h, :, QK_NOPE:] = (rope * scale).reshape(nb, tl, QK_ROPE).astype(o_ref.dtype)


HEADS_PER_STEP = 4


def _row_blocking(batch, length, target):
    if length >= target:
        return 1, _tile(length, target)
    return _tile(batch, max(1, target // length), 1), length


def _head_out_spec(nb, hb, tl, length, width):
    per_batch = length // tl
    return pl.BlockSpec((nb, hb, tl, width), lambda i, h: (i // per_batch, h, i % per_batch, 0))


def _qproj(qn, w_heads, batch, length, rope_tables):
    n, kdim = qn.shape
    heads, _, cols = w_heads.shape
    use_rope = rope_tables is not None
    nb, tl = (1, _tile(length, 1024)) if use_rope else _row_blocking(batch, length, 1024)
    tm = nb * tl
    hb = _tile(heads, HEADS_PER_STEP, 1)
    in_specs = [pl.BlockSpec((tm, kdim), lambda i, h: (i, 0)),
                pl.BlockSpec((hb, kdim, cols), lambda i, h: (h, 0, 0))]
    args = [qn, w_heads]
    if use_rope:
        per_batch = length // tl
        spec = pl.BlockSpec((tl, QK_ROPE), lambda i, h: (i % per_batch, 0))
        in_specs += [spec, spec]
        args += list(rope_tables)
    width = QK_NOPE + QK_ROPE
    return pl.pallas_call(
        functools.partial(_qproj_kernel, scale=width ** -0.5, use_rope=use_rope),
        grid=(n // tm, heads // hb),
        in_specs=in_specs,
        out_specs=_head_out_spec(nb, hb, tl, length, width),
        out_shape=jax.ShapeDtypeStruct((batch, heads, length, width), BF16),
        compiler_params=_params("parallel", "parallel"),
        name="q_proj",
    )(*args)


def _kvdec_kernel(c_ref, kr_ref, w_ref, k_ref, v_ref):
    nb, hb, tl, _ = k_ref.shape
    c = c_ref[...].astype(BF16)
    k_rope = kr_ref[...].reshape(nb, tl, QK_ROPE).astype(k_ref.dtype)
    for h in range(hb):
        acc = jnp.dot(c, w_ref[h], preferred_element_type=F32)
        k_ref[:, h, :, :QK_NOPE] = acc[:, :QK_NOPE].reshape(nb, tl, QK_NOPE).astype(k_ref.dtype)
        k_ref[:, h, :, QK_NOPE:] = k_rope
        v_ref[:, h] = acc[:, QK_NOPE:].reshape(nb, tl, V_HEAD).astype(v_ref.dtype)


def _kvdec(ckv, krope, w_heads, batch, length):
    n, kdim = ckv.shape
    heads, _, cols = w_heads.shape
    nb, tl = _row_blocking(batch, length, 1536)
    tm = nb * tl
    hb = _tile(heads, HEADS_PER_STEP, 1)
    return pl.pallas_call(
        _kvdec_kernel,
        grid=(n // tm, heads // hb),
        in_specs=[pl.BlockSpec((tm, kdim), lambda i, h: (i, 0)),
                  pl.BlockSpec((tm, QK_ROPE), lambda i, h: (i, 0)),
                  pl.BlockSpec((hb, kdim, cols), lambda i, h: (h, 0, 0))],
        out_specs=[_head_out_spec(nb, hb, tl, length, QK_NOPE + QK_ROPE),
                   _head_out_spec(nb, hb, tl, length, V_HEAD)],
        out_shape=[jax.ShapeDtypeStruct((batch, heads, length, QK_NOPE + QK_ROPE), BF16),
                   jax.ShapeDtypeStruct((batch, heads, length, V_HEAD), BF16)],
        compiler_params=_params("parallel", "parallel"),
        name="kv_decompress",
    )(ckv, krope, w_heads)


def _qk(q, k):
    return lax.dot_general(q, k, (((1,), (1,)), ((), ())), preferred_element_type=F32)


def _attn_kernel(q_ref, k_ref, v_ref, gate_ref, o_ref, *stat_refs, tk, nk, sub):
    hb, tq, dv = q_ref.shape[1], q_ref.shape[2], v_ref.shape[3]
    if nk > 1:
        m_ref, l_ref, acc_ref = stat_refs
    lanes = dv

    def write_out(h, rows, acc, l):
        cols = slice(h * dv, (h + 1) * dv)
        o_ref[0, rows, cols] = (acc / l * gate_ref[0, rows, cols].astype(F32)).astype(o_ref.dtype)

    for h in range(hb):
        for r in range(tq // sub):
            rows = pl.ds(r * sub, sub)
            q = q_ref[0, h, rows, :]
            s = _qk(q, k_ref[0, h, :tk, :])
            m = jnp.max(s, axis=-1, keepdims=True)
            p = jnp.exp(s - m)
            l = jnp.sum(p, axis=-1, keepdims=True)
            acc = jnp.dot(p.astype(BF16), v_ref[0, h, :tk, :], preferred_element_type=F32)
            if nk > 1:
                m_ref[h, rows, :] = jnp.broadcast_to(m, (sub, lanes))
                l_ref[h, rows, :] = jnp.broadcast_to(l, (sub, lanes))
                acc_ref[h, rows, :] = acc
            else:
                write_out(h, rows, acc, l)
        for j in range(1, nk):
            for r in range(tq // sub):
                rows = pl.ds(r * sub, sub)
                q = q_ref[0, h, rows, :]
                s = _qk(q, k_ref[0, h, j * tk:(j + 1) * tk, :])
                m_old = m_ref[h, rows, :]
                m_new = jnp.maximum(m_old, jnp.max(s, axis=-1, keepdims=True))
                alpha = jnp.exp(m_old - m_new)
                p = jnp.exp(s - m_new[:, :1])
                l = alpha * l_ref[h, rows, :] + jnp.sum(p, axis=-1, keepdims=True)
                acc = alpha * acc_ref[h, rows, :] + jnp.dot(
                    p.astype(BF16), v_ref[0, h, j * tk:(j + 1) * tk, :], preferred_element_type=F32)
                if j < nk - 1:
                    m_ref[h, rows, :] = m_new
                    l_ref[h, rows, :] = l
                    acc_ref[h, rows, :] = acc
                else:
                    write_out(h, rows, acc, l)


ATTN_MAX_KV_CHUNK = 2304
ATTN_STEP_ROWS = 2048


def _attention(q, k, v, gate):
    batch, heads, lq, dqk = q.shape
    lk, dv = k.shape[2], v.shape[3]
    tq = _tile(lq, 2048)
    sub = _tile(tq, 512)
    hb = _tile(heads, max(1, ATTN_STEP_ROWS // lq), 1) if tq == lq else 1
    tk = _tile(lk, ATTN_MAX_KV_CHUNK, 128) if lk > ATTN_MAX_KV_CHUNK else lk
    nk = lk // tk
    stats = [pltpu.VMEM((hb, tq, dv), F32)] * 3 if nk > 1 else []
    return pl.pallas_call(
        functools.partial(_attn_kernel, tk=tk, nk=nk, sub=sub),
        grid=(batch, heads // hb, lq // tq),
        in_specs=[pl.BlockSpec((1, hb, tq, dqk), lambda b, h, i: (b, h, i, 0)),
                  pl.BlockSpec((1, hb, lk, dqk), lambda b, h, i: (b, h, 0, 0)),
                  pl.BlockSpec((1, hb, lk, dv), lambda b, h, i: (b, h, 0, 0)),
                  pl.BlockSpec((1, tq, hb * dv), lambda b, h, i: (b, i, h))],
        out_specs=pl.BlockSpec((1, tq, hb * dv), lambda b, h, i: (b, i, h)),
        out_shape=jax.ShapeDtypeStruct((batch, lq, heads * dv), BF16),
        scratch_shapes=stats,
        compiler_params=_params("parallel", "parallel", "arbitrary"),
        name="attention",
    )(q, k, v, gate)


def _gelu_tanh(x):
    return 0.5 * x * (1.0 + jnp.tanh(math.sqrt(2.0 / math.pi) * (x + 0.044715 * (x * x * x))))


def _s5_kernel(u_ref, w_ref, g_ref, e_ref, d_ref, ar_ref, ai_ref, h0_ref, y_ref, fin_ref, *, n_chunks):
    u = u_ref[0]
    rows = u.shape[0]
    batch = rows // n_chunks
    half = g_ref.shape[2] // 2
    g = jnp.dot(u, g_ref[0], preferred_element_type=F32)
    h0 = jnp.broadcast_to(h0_ref[0][:, None, :], (batch, n_chunks, 2 * half)).reshape(rows, 2 * half)
    chunk = lax.broadcasted_iota(jnp.int32, (rows, half), 0) % n_chunks
    ar = ar_ref[0]
    ai = ai_ref[0]

    def cmul(x, step, lo):
        swapped = pltpu.roll(x, half // 2, axis=1)
        return x * ar[step:step + 1, lo:lo + half] + swapped * ai[step:step + 1, lo:lo + half]

    def scan(gd, h0d, lo, backward):
        if backward:
            z = jnp.where(chunk == n_chunks - 1, h0d, pltpu.roll(gd, rows - 1, axis=0))
        else:
            z = jnp.where(chunk == 0, h0d, pltpu.roll(gd, 1, axis=0))
        step, dist = 0, 1
        while dist < n_chunks:
            if backward:
                moved = jnp.where(chunk < n_chunks - dist, pltpu.roll(z, rows - dist, axis=0), 0.0)
            else:
                moved = jnp.where(chunk >= dist, pltpu.roll(z, dist, axis=0), 0.0)
            z = z + cmul(moved, step, lo)
            step, dist = step + 1, dist * 2
        return z

    gf, gb = g[:, :half], g[:, half:]
    pf = scan(gf, h0[:, :half], 0, False)
    pb = scan(gb, h0[:, half:], half, True)
    p = jnp.concatenate([pf, pb], axis=1).astype(BF16)
    y = (jnp.dot(u, w_ref[0], preferred_element_type=F32)
         + jnp.dot(p, e_ref[0], preferred_element_type=F32)
         + u.astype(F32) * d_ref[0])
    y_ref[0] = _gelu_tanh(y).astype(y_ref.dtype)
    def pick(x, which):
        return jnp.sum(jnp.where(chunk == which, x, 0.0).reshape(batch, n_chunks, half), axis=1)

    ff = pick(cmul(pf, 0, 0) + gf, n_chunks - 1)
    fb = pick(cmul(pb, 0, half) + gb, 0)
    fin_ref[0] = jnp.concatenate([ff, fb], axis=1)


def _s5(u_rows, ops, h0, n_chunks):
    groups, rows, width = u_rows.shape
    batch = rows // n_chunks
    wt, gt, et, dvec, ar, ai = ops
    lanes = gt.shape[2]

    def per_group(*shape):
        return pl.BlockSpec((1,) + shape, lambda g: (g,) + (0,) * len(shape))

    return pl.pallas_call(
        functools.partial(_s5_kernel, n_chunks=n_chunks),
        grid=(groups,),
        in_specs=[per_group(rows, width), per_group(width, width), per_group(width, lanes),
                  per_group(lanes, width), per_group(1, width), per_group(*ar.shape[1:]),
                  per_group(*ai.shape[1:]), per_group(batch, lanes)],
        out_specs=[per_group(rows, width), per_group(batch, lanes)],
        out_shape=[jax.ShapeDtypeStruct((groups, rows, width), BF16),
                   jax.ShapeDtypeStruct((groups, batch, lanes), F32)],
        compiler_params=_params("parallel"),
        name="s5",
    )(u_rows, wt, gt, et, dvec, ar, ai, h0)


def _s5_operators(lam_re, lam_im, log_dt, b_re, b_im, c_re, c_im, d_skip, chunk, max_chunks):
    hi = lax.Precision.HIGHEST
    groups, p_dim, j_dim = b_re.shape[1:]
    t = chunk
    lam = lax.complex(lam_re, lam_im)
    z = lam * jnp.exp(log_dt)[..., None]
    a = jnp.exp(z)
    b_bar = ((a - 1.0) / lam)[..., None] * lax.complex(b_re, b_im)
    c_mat = lax.complex(c_re, c_im)
    k_idx = jnp.arange(t + 1, dtype=F32)
    apow = jnp.exp(z[None] * k_idx[:, None, None, None])

    cb = c_mat[..., None] * b_bar[:, :, None]
    ap = jnp.moveaxis(apow[:t], 1, 0)
    kern = (jnp.einsum('dtgp,dgopi->dtgoi', ap.real, cb.real, precision=hi)
            - jnp.einsum('dtgp,dgopi->dtgoi', ap.imag, cb.imag, precision=hi))
    zeros = jnp.zeros((t - 1,) + kern.shape[2:], F32)
    k_lag = jnp.concatenate([zeros, kern[0]], axis=0) + jnp.concatenate([kern[1][::-1], zeros], axis=0)
    k_lag = k_lag.transpose(1, 3, 0, 2).reshape(groups, j_dim, (2 * t - 1) * j_dim).astype(BF16)
    w_all = jnp.stack([k_lag[:, :, (t - 1 - j) * j_dim:(2 * t - 1 - j) * j_dim] for j in range(t)], axis=1)
    w_all = w_all.reshape(groups, t * j_dim, t * j_dim)

    apow_tp = jnp.moveaxis(apow, 0, 2)
    b_t = jnp.swapaxes(b_bar, -1, -2)
    inj_f = apow_tp[0][:, t - 1::-1][:, :, None, :] * b_t[0][:, None]
    inj_b = apow_tp[1][:, :t][:, :, None, :] * b_t[1][:, None]
    g_all = jnp.concatenate([inj_f.real, inj_f.imag, inj_b.real, inj_b.imag], axis=-1)
    g_all = g_all.reshape(groups, t * j_dim, 4 * p_dim)

    apow_pt = jnp.moveaxis(apow, 0, -1)
    c_t = jnp.swapaxes(c_mat, -1, -2)
    out_f = c_t[0][:, :, None, :] * apow_pt[0][:, :, 1:t + 1][..., None]
    out_b = c_t[1][:, :, None, :] * apow_pt[1][:, :, t:0:-1][..., None]
    e_all = jnp.concatenate([out_f.real, -out_f.imag, out_b.real, -out_b.imag], axis=1)
    e_all = e_all.reshape(groups, 4 * p_dim, t * j_dim)

    d_row = jnp.tile(d_skip.reshape(groups, 1, j_dim), (1, t, 1)).reshape(groups, 1, t * j_dim)

    n_steps = max(1, (max_chunks - 1).bit_length())
    steps = (t * 2 ** jnp.arange(n_steps)).astype(F32)
    tp = jnp.exp(z[None] * steps[:, None, None, None])
    a_r = jnp.concatenate([tp[:, 0].real, tp[:, 0].real, tp[:, 1].real, tp[:, 1].real], axis=-1)
    a_i = jnp.concatenate([-tp[:, 0].imag, tp[:, 0].imag, -tp[:, 1].imag, tp[:, 1].imag], axis=-1)
    a_r = a_r.transpose(1, 0, 2)
    a_i = a_i.transpose(1, 0, 2)
    return w_all, g_all.astype(BF16), e_all.astype(BF16), d_row, a_r, a_i


def _postmix_kernel(x_ref, y_ref, gpost_ref, gpre_ref, mod_ref, x1_ref, h_ref):
    mod = mod_ref[0]
    x1 = x_ref[...] + mod[2:3] * _rms(y_ref[...].astype(F32), gpost_ref[...])
    x1_ref[...] = x1
    h_ref[...] = (_rms(x1, gpre_ref[...]) * (1.0 + mod[4:5]) + mod[3:4]).astype(h_ref.dtype)


def _postmix(x, y, g_post, g_pre, mod):
    n, d = x.shape
    tm = _tile(n // mod.shape[0], 256)
    row = pl.BlockSpec((tm, d), lambda i: (i, 0))
    gain = pl.BlockSpec((1, d), lambda i: (0, 0))
    return pl.pallas_call(
        _postmix_kernel,
        grid=(n // tm,),
        in_specs=[row, row, gain, gain, _mod_spec(mod, n, tm)],
        out_specs=[row, row],
        out_shape=[jax.ShapeDtypeStruct((n, d), F32), jax.ShapeDtypeStruct((n, d), BF16)],
        compiler_params=_params("parallel"),
        name="post_mix",
    )(x, y, g_post, g_pre, mod)


def _postmlp_kernel(x_ref, y_ref, g_ref, mod_ref, o_ref):
    o_ref[...] = x_ref[...] + mod_ref[0][5:6] * _rms(y_ref[...].astype(F32), g_ref[...])


def _postmlp(x, y, g_post, mod):
    n, d = x.shape
    tm = _tile(n // mod.shape[0], 256)
    row = pl.BlockSpec((tm, d), lambda i: (i, 0))
    return pl.pallas_call(
        _postmlp_kernel,
        grid=(n // tm,),
        in_specs=[row, row, pl.BlockSpec((1, d), lambda i: (0, 0)), _mod_spec(mod, n, tm)],
        out_specs=row,
        out_shape=jax.ShapeDtypeStruct((n, d), F32),
        compiler_params=_params("parallel"),
        name="post_mlp",
    )(x, y, g_post, mod)


def _axial_rope(n_tokens):
    rows = n_tokens // GRID_W
    row = jnp.repeat(jnp.arange(rows, dtype=F32), GRID_W)
    col = jnp.tile(jnp.arange(GRID_W, dtype=F32), rows)
    n_freq = QK_ROPE // 4
    inv_freq = ROPE_THETA ** (-jnp.arange(n_freq, dtype=F32) / n_freq)
    ang = jnp.concatenate([row[:, None] * inv_freq, col[:, None] * inv_freq], axis=-1)
    ang = jnp.concatenate([ang, ang], axis=-1)
    return jnp.cos(ang), jnp.sin(ang)


def _rotate_half_cols(w):
    half = w.shape[-1] // 2
    return jnp.concatenate([-w[..., half:], w[..., :half]], axis=-1)


def _prepare_weights(w_in, w_uq, w_ukv, w_glu, w_out, w_ff1, w_ff2, q_lora, kv_lora):
    uq = w_uq.reshape(q_lora, N_HEADS, QK_NOPE + QK_ROPE)
    uq_plain = uq.transpose(1, 0, 2)
    uq_rope = jnp.concatenate([uq, _rotate_half_cols(uq[..., QK_NOPE:])], axis=-1).transpose(1, 0, 2)
    ukv = w_ukv.reshape(kv_lora, N_HEADS, QK_NOPE + V_HEAD).transpose(1, 0, 2)
    cast = lambda w: w.astype(BF16)
    return dict(w_in_t=cast(jnp.transpose(w_in)),
                uq_plain=cast(uq_plain), uq_rope=cast(uq_rope), ukv=cast(ukv), w_glu=cast(w_glu),
                w_out=w_out, w_ff1=w_ff1, w_ff2=cast(w_ff2))


def _slab_permutation(j_dim):
    gb = LANES // j_dim
    src = np.arange(gb * LANES)
    i8, g8, j = src // LANES, (src % LANES) // j_dim, src % j_dim
    p = np.zeros((gb * LANES, gb * LANES), np.float32)
    p[src, g8 * LANES + i8 * j_dim + j] = 1.0
    return jnp.asarray(p, BF16)


def _to_rows_kernel(u_ref, p_ref, o_ref, *, chunk):
    gb, rows, _ = o_ref.shape
    tiles = [u_ref[pl.ds(i, rows, stride=chunk), :].astype(BF16) for i in range(chunk)]
    for k in range(chunk // gb):
        z = jnp.concatenate(tiles[gb * k:gb * (k + 1)], axis=1)
        uk = jnp.dot(z, p_ref[...], preferred_element_type=F32).astype(o_ref.dtype)
        for g in range(gb):
            o_ref[g, :, k * LANES:(k + 1) * LANES] = uk[:, g * LANES:(g + 1) * LANES]


def _to_group_rows(u, perm, groups, j_dim, chunk):
    n = u.shape[0]
    gb = LANES // j_dim
    rows = _tile(n // chunk, 256, 16)
    return pl.pallas_call(
        functools.partial(_to_rows_kernel, chunk=chunk),
        grid=(groups // gb, n // (rows * chunk)),
        in_specs=[pl.BlockSpec((rows * chunk, LANES), lambda g, r: (r, g)),
                  pl.BlockSpec(perm.shape, lambda g, r: (0, 0))],
        out_specs=pl.BlockSpec((gb, rows, chunk * j_dim), lambda g, r: (g, r, 0)),
        out_shape=jax.ShapeDtypeStruct((groups, n // chunk, chunk * j_dim), BF16),
        compiler_params=_params("parallel", "parallel"),
        name="to_group_rows",
    )(u, perm)


def _from_rows_kernel(y_ref, p_ref, o_ref, *, chunk):
    gb, rows, _ = y_ref.shape
    for k in range(chunk // gb):
        yk = jnp.concatenate([y_ref[g, :, k * LANES:(k + 1) * LANES] for g in range(gb)], axis=1)
        z = jnp.dot(yk, p_ref[...], preferred_element_type=F32)
        for i8 in range(gb):
            o_ref[pl.ds(gb * k + i8, rows, stride=chunk), :] = z[:, i8 * LANES:(i8 + 1) * LANES]


def _from_group_rows(y, perm_t, groups, j_dim, chunk):
    n_rows = y.shape[1]
    gb = LANES // j_dim
    rows = _tile(n_rows, 256, 16)
    return pl.pallas_call(
        functools.partial(_from_rows_kernel, chunk=chunk),
        grid=(groups // gb, n_rows // rows),
        in_specs=[pl.BlockSpec((gb, rows, chunk * j_dim), lambda g, r: (g, r, 0)),
                  pl.BlockSpec(perm_t.shape, lambda g, r: (0, 0))],
        out_specs=pl.BlockSpec((rows * chunk, LANES), lambda g, r: (r, g)),
        out_shape=jax.ShapeDtypeStruct((n_rows * chunk, groups * j_dim), F32),
        compiler_params=_params("parallel", "parallel"),
        name="from_group_rows",
    )(y, perm_t)


def _trunk(x, mod, wts, gains, s5_ops, s5_dims, ctx):
    batch, length, d_model = x.shape
    n = batch * length
    groups, p_dim, j_dim, chunk = s5_dims
    x2 = x.reshape(n, d_model)
    rope = None if ctx is None else ctx[3]

    h = _modnorm(x2, gains['g_pre_mix'], mod, 0, 1)
    q_lora, kv_lora = gains['g_q_lat'].shape[1], gains['g_kv_lat'].shape[1]
    s5_width = groups * j_dim
    wt = wts['w_in_t']
    qn = _mm_wt(h, wt, 0, q_lora, BF16, epilogue="rmsnorm", gain=gains['g_q_lat'])
    ckv_n, k_rope = _mm_ck(h, wt, q_lora, gains['g_kv_lat'], length, rope)
    u = _mm_wt(h, wt, q_lora + kv_lora + QK_ROPE, s5_width, F32)
    gate_sig = _mm_wt(h, wt, q_lora + kv_lora + QK_ROPE + s5_width, 2 * d_model, BF16, epilogue="sigmoid")

    q = _qproj(qn, wts['uq_plain'] if rope is None else wts['uq_rope'], batch, length, rope)
    if ctx is None:
        ckv_all, kr_all, lk = ckv_n, k_rope, length
    else:
        lk = ctx[0].shape[1] + length
        ckv_all = jnp.concatenate([ctx[0], ckv_n.reshape(batch, length, -1)], axis=1).reshape(batch * lk, -1)
        kr_all = jnp.concatenate([ctx[1], k_rope.reshape(batch, length, -1)], axis=1).reshape(batch * lk, -1)
    k, v = _kvdec(ckv_all, kr_all, wts['ukv'], batch, lk)
    oa_gated = _attention(q, k, v, gate_sig.reshape(batch, length, -1)).reshape(n, -1)

    if ctx is None:
        h0 = jnp.zeros((groups, batch, 4 * p_dim), F32)
    else:
        h0 = ctx[2].transpose(3, 0, 1, 2, 4).reshape(groups, batch, 4 * p_dim)
    perm = _slab_permutation(j_dim)
    u_rows = _to_group_rows(u, perm, groups, j_dim, chunk)
    y_rows, s5_fin = _s5(u_rows, s5_ops, h0, length // chunk)
    y = _from_group_rows(y_rows, perm.T, groups, j_dim, chunk)
    m = _glu_merge(y, wts['w_glu'], gate_sig, oa_gated)
    mixed = _mm(m, wts['w_out'], BF16)
    x1, h2 = _postmix(x2, mixed, gains['g_post_mix'], gains['g_pre_mlp'], mod)
    a1 = _mm(h2, wts['w_ff1'], BF16, epilogue="relu2")
    a2 = _mm(a1, wts['w_ff2'], BF16)
    y_out = _postmlp(x1, a2, gains['g_post_mlp'], mod).reshape(batch, length, d_model)
    s5_fin = s5_fin.reshape(groups, batch, 2, 2, p_dim).transpose(1, 2, 3, 0, 4)
    return y_out, ckv_n.reshape(batch, length, -1), k_rope.reshape(batch, length, -1), s5_fin


def kernel(x_prompt, x_sample, c, cache_ckv, cache_krope, state_s5, c_ctx, w_mod, b_mod, g_pre_mix, w_in, g_q_lat, g_kv_lat, w_uq, w_ukv, s5_lam_re, s5_lam_im, s5_log_dt, s5_b_re, s5_b_im, s5_c_re, s5_c_im, s5_d, w_glu, w_out, g_post_mix, g_pre_mlp, w_ff1, w_ff2, g_post_mlp):
    depth = w_mod.shape[0]
    d_model = x_prompt.shape[-1]
    dec_batch, dec_seq = x_sample.shape[:2]
    q_lora, kv_lora = g_q_lat.shape[1], g_kv_lat.shape[1]
    groups, p_dim, j_dim = s5_b_re.shape[2:]
    chunk = min(S5_CHUNK, x_prompt.shape[1] // 8)
    rope = _axial_rope(dec_seq)

    cond = jnp.concatenate([c_ctx[None, :], c], axis=0)
    cond = jnp.pad(cond, ((0, -cond.shape[0] % 8), (0, 0)))

    xp, xs = x_prompt, x_sample
    new_ckv, new_krope, new_s5 = [], [], []
    for l in range(depth):
        mods = _adaln(cond, w_mod[l], b_mod[l][None, :])
        mod_p = mods[0:1].reshape(1, -1, d_model)
        mod_s = mods[1:1 + dec_batch].reshape(dec_batch, -1, d_model)
        wts = _prepare_weights(w_in[l], w_uq[l], w_ukv[l], w_glu[l], w_out[l], w_ff1[l], w_ff2[l],
                               q_lora, kv_lora)
        gains = dict(g_pre_mix=g_pre_mix[l][None], g_q_lat=g_q_lat[l][None], g_kv_lat=g_kv_lat[l][None],
                     g_post_mix=g_post_mix[l][None], g_pre_mlp=g_pre_mlp[l][None], g_post_mlp=g_post_mlp[l][None])
        max_chunks = max(x_prompt.shape[1], dec_seq) // chunk
        s5_ops = _s5_operators(s5_lam_re[l], s5_lam_im[l], s5_log_dt[l], s5_b_re[l], s5_b_im[l],
                               s5_c_re[l], s5_c_im[l], s5_d[l], chunk, max_chunks)
        s5_dims = (groups, p_dim, j_dim, chunk)

        xp, ckv_n, k_rope_p, s5_last = _trunk(xp, mod_p, wts, gains, s5_ops, s5_dims, None)
        new_ckv.append(ckv_n)
        new_krope.append(k_rope_p)
        new_s5.append(s5_last)
        ctx = (cache_ckv[:, l], cache_krope[:, l], state_s5[:, l], rope)
        xs, _, _, _ = _trunk(xs, mod_s, wts, gains, s5_ops, s5_dims, ctx)
    return (xp, xs, jnp.stack(new_ckv, axis=1), jnp.stack(new_krope, axis=1),
            jnp.stack(new_s5, axis=1).astype(x_prompt.dtype))
```

```python
import functools
import math

import jax
import jax.numpy as jnp
import numpy as np
from jax import lax
from jax.experimental import pallas as pl
from jax.experimental.pallas import tpu as pltpu

N_HEADS = 32
QK_NOPE = 128
QK_ROPE = 64
V_HEAD = 128
GRID_W = 64
ROPE_THETA = 10000.0
EPS = 1e-6
S5_CHUNK = 32
LANES = 128
VMEM_LIMIT_BYTES = 48 * 1024 * 1024

F32 = jnp.float32
BF16 = jnp.bfloat16


VMEM_LIMIT_BIG_BYTES = 56 * 1024 * 1024


def _params(*semantics, vmem=VMEM_LIMIT_BYTES):
    return pltpu.CompilerParams(dimension_semantics=semantics, vmem_limit_bytes=vmem)


def _tile(n, target, align=8):
    for d in range(min(n, target), 0, -1):
        if n % d == 0 and d % align == 0:
            return d
    return n


def _adaln_kernel(c_ref, w_ref, b_ref, o_ref):
    c = c_ref[...]
    s = (c * jax.nn.sigmoid(c)).astype(BF16)
    o_ref[...] = jnp.dot(s, w_ref[...].astype(BF16), preferred_element_type=F32) + b_ref[...]


def _adaln(cond, w_mod, b_mod):
    rows, d = cond.shape
    n = w_mod.shape[1]
    tn = _tile(n, 512, 128)
    return pl.pallas_call(
        _adaln_kernel,
        grid=(n // tn,),
        in_specs=[pl.BlockSpec((rows, d), lambda j: (0, 0)),
                  pl.BlockSpec((d, tn), lambda j: (0, j)),
                  pl.BlockSpec((1, tn), lambda j: (0, j))],
        out_specs=pl.BlockSpec((rows, tn), lambda j: (0, j)),
        out_shape=jax.ShapeDtypeStruct((rows, n), F32),
        compiler_params=_params("parallel"),
        name="adaln",
    )(cond, w_mod, b_mod)


def _rms(x, g):
    return x * lax.rsqrt(jnp.mean(x * x, axis=-1, keepdims=True) + EPS) * g


def _modnorm_kernel(x_ref, g_ref, mod_ref, o_ref, *, shift_idx, scale_idx):
    mod = mod_ref[0]
    y = _rms(x_ref[...], g_ref[...])
    o_ref[...] = (y * (1.0 + mod[scale_idx:scale_idx + 1]) + mod[shift_idx:shift_idx + 1]).astype(o_ref.dtype)


def _mod_spec(mod, n_rows, tm):
    rows_per_mod = n_rows // mod.shape[0]
    return pl.BlockSpec((1,) + mod.shape[1:], lambda i: ((i * tm) // rows_per_mod, 0, 0))


def _modnorm(x, g, mod, shift_idx, scale_idx):
    n, d = x.shape
    tm = _tile(n // mod.shape[0], 256)
    return pl.pallas_call(
        functools.partial(_modnorm_kernel, shift_idx=shift_idx, scale_idx=scale_idx),
        grid=(n // tm,),
        in_specs=[pl.BlockSpec((tm, d), lambda i: (i, 0)),
                  pl.BlockSpec((1, d), lambda i: (0, 0)),
                  _mod_spec(mod, n, tm)],
        out_specs=pl.BlockSpec((tm, d), lambda i: (i, 0)),
        out_shape=jax.ShapeDtypeStruct((n, d), BF16),
        compiler_params=_params("parallel"),
        name="modnorm",
    )(x, g, mod)


def _epilogue(r, kind):
    if kind == "relu2":
        return jnp.square(jnp.maximum(r, 0.0))
    if kind == "sigmoid":
        return jax.nn.sigmoid(r)
    assert kind is None
    return r


def _bf16_panel(w_ref, scratch_ref):
    if scratch_ref is None:
        return w_ref[...]

    @pl.when(pl.program_id(1) == 0)
    def _():
        scratch_ref[...] = w_ref[...].astype(BF16)

    return scratch_ref[...]


def _mm_full_k_kernel(a_ref, w_ref, o_ref, wb_ref=None, *, epilogue):
    w = _bf16_panel(w_ref, wb_ref)
    r = jnp.dot(a_ref[...], w, preferred_element_type=F32)
    o_ref[...] = _epilogue(r, epilogue).astype(o_ref.dtype)


def _dot_nt(a, wt):
    return lax.dot_general(a, wt, (((1,), (1,)), ((), ())), preferred_element_type=F32)


def _mm_wt_kernel(a_ref, wt_ref, *rest, epilogue):
    r = _dot_nt(a_ref[...], wt_ref[...])
    if epilogue == "rmsnorm":
        g_ref, o_ref = rest
        r = _rms(r, g_ref[...])
    else:
        (o_ref,) = rest
        r = _epilogue(r, epilogue)
    o_ref[...] = r.astype(o_ref.dtype)


def _mm_wt(a, wt, row0, n, out_dtype, epilogue=None, gain=None):
    m, kdim = a.shape
    tm = _tile(m, 512)
    tn = n if epilogue == "rmsnorm" else _tile(n, 1024, 128)
    in_specs = [pl.BlockSpec((tm, kdim), lambda j, i: (i, 0)),
                pl.BlockSpec((pl.Element(tn), pl.Element(kdim)),
                             lambda j, i: (pl.multiple_of(row0 + j * tn, 16), 0))]
    args = [a, wt]
    if epilogue == "rmsnorm":
        in_specs.append(pl.BlockSpec((1, n), lambda j, i: (0, 0)))
        args.append(gain)
    return pl.pallas_call(
        functools.partial(_mm_wt_kernel, epilogue=epilogue),
        grid=(n // tn, m // tm),
        in_specs=in_specs,
        out_specs=pl.BlockSpec((tm, tn), lambda j, i: (i, j)),
        out_shape=jax.ShapeDtypeStruct((m, n), out_dtype),
        compiler_params=_params("parallel", "parallel"),
        name="matmul_wt",
    )(*args)


def _mm_ck_kernel(a_ref, wt_ref, g_ref, *rest, kv_lora, use_rope):
    if use_rope:
        cos_ref, sin_ref, ckv_ref, kr_ref, wb_ref = rest
    else:
        ckv_ref, kr_ref, wb_ref = rest
    half = QK_ROPE // 2

    @pl.when(pl.program_id(0) == 0)
    def _():
        wb_ref[:kv_lora + QK_ROPE, :] = wt_ref[...]
        wb_ref[kv_lora + QK_ROPE:kv_lora + QK_ROPE + half, :] = -wt_ref[kv_lora + half:, :]
        wb_ref[kv_lora + QK_ROPE + half:, :] = wt_ref[kv_lora:kv_lora + half, :]

    acc = _dot_nt(a_ref[...], wb_ref[...])
    ckv_ref[...] = _rms(acc[:, :kv_lora], g_ref[...])
    k_rope = acc[:, kv_lora:kv_lora + QK_ROPE]
    if use_rope:
        k_rope = k_rope * cos_ref[...] + acc[:, kv_lora + QK_ROPE:] * sin_ref[...]
    kr_ref[...] = k_rope


def _mm_ck(a, wt, row0, g_kv, length, rope_tables):
    m, kdim = a.shape
    kv_lora = g_kv.shape[1]
    use_rope = rope_tables is not None
    tm = _tile(length, 512) if use_rope else _tile(m, 512)
    rows = kv_lora + QK_ROPE
    in_specs = [pl.BlockSpec((tm, kdim), lambda i: (i, 0)),
                pl.BlockSpec((pl.Element(rows), pl.Element(kdim)), lambda i: (row0, 0)),
                pl.BlockSpec((1, kv_lora), lambda i: (0, 0))]
    args = [a, wt, g_kv]
    if use_rope:
        per_batch = length // tm
        spec = pl.BlockSpec((tm, QK_ROPE), lambda i: (i % per_batch, 0))
        in_specs += [spec, spec]
        args += list(rope_tables)
    return pl.pallas_call(
        functools.partial(_mm_ck_kernel, kv_lora=kv_lora, use_rope=use_rope),
        grid=(m // tm,),
        in_specs=in_specs,
        out_specs=[pl.BlockSpec((tm, kv_lora), lambda i: (i, 0)),
                   pl.BlockSpec((tm, QK_ROPE), lambda i: (i, 0))],
        out_shape=[jax.ShapeDtypeStruct((m, kv_lora), F32),
                   jax.ShapeDtypeStruct((m, QK_ROPE), F32)],
        scratch_shapes=[pltpu.VMEM((rows + QK_ROPE, kdim), BF16)],
        compiler_params=_params("arbitrary"),
        name="matmul_ck",
    )(*args)


def _mm_kernel(a_ref, w_ref, o_ref, acc_ref, *, epilogue):
    k = pl.program_id(2)

    @pl.when(k == 0)
    def _():
        acc_ref[...] = jnp.zeros_like(acc_ref)

    acc_ref[...] += jnp.dot(a_ref[...], w_ref[...], preferred_element_type=F32)

    @pl.when(k == pl.num_programs(2) - 1)
    def _():
        o_ref[...] = _epilogue(acc_ref[...], epilogue).astype(o_ref.dtype)


MM_MAX_FULL_K = 4096


def _mm(a, w, out_dtype, epilogue=None):
    m, kdim = a.shape
    n = w.shape[1]
    if kdim <= MM_MAX_FULL_K:
        cast_w = w.dtype != BF16
        tm, tn = _tile(m, 512 if kdim > 2048 else 1024), _tile(n, 1024, 128)
        return pl.pallas_call(
            functools.partial(_mm_full_k_kernel, epilogue=epilogue),
            grid=(n // tn, m // tm),
            in_specs=[pl.BlockSpec((tm, kdim), lambda j, i: (i, 0)),
                      pl.BlockSpec((kdim, tn), lambda j, i: (0, j))],
            out_specs=pl.BlockSpec((tm, tn), lambda j, i: (i, j)),
            out_shape=jax.ShapeDtypeStruct((m, n), out_dtype),
            scratch_shapes=[pltpu.VMEM((kdim, tn), BF16)] if cast_w else [],
            compiler_params=(_params("parallel", "arbitrary", vmem=VMEM_LIMIT_BIG_BYTES) if cast_w
                             else _params("parallel", "parallel")),
            name="matmul",
        )(a, w)
    assert w.dtype == BF16
    tm, tn, tk = _tile(m, 1024), _tile(n, 1024, 128), _tile(kdim, MM_MAX_FULL_K, 128)
    return pl.pallas_call(
        functools.partial(_mm_kernel, epilogue=epilogue),
        grid=(m // tm, n // tn, kdim // tk),
        in_specs=[pl.BlockSpec((tm, tk), lambda i, j, k: (i, k)),
                  pl.BlockSpec((tk, tn), lambda i, j, k: (k, j))],
        out_specs=pl.BlockSpec((tm, tn), lambda i, j, k: (i, j)),
        out_shape=jax.ShapeDtypeStruct((m, n), out_dtype),
        scratch_shapes=[pltpu.VMEM((tm, tn), F32)],
        compiler_params=_params("parallel", "parallel", "arbitrary"),
        name="matmul_k",
    )(a, w)


def _glu_merge_kernel(a_ref, wv_ref, wg_ref, sb_ref, oa_ref, o_ref):
    a = a_ref[...].astype(BF16)
    value = jnp.dot(a, wv_ref[...], preferred_element_type=F32)
    gate = jnp.dot(a, wg_ref[...], preferred_element_type=F32)
    o_b = value * jax.nn.sigmoid(gate)
    o_ref[...] = (oa_ref[...].astype(F32) + sb_ref[...].astype(F32) * o_b).astype(o_ref.dtype)


def _glu_merge(y, w_glu, gate_sig, oa_gated):
    m, kdim = y.shape
    d = w_glu.shape[1] // 2
    tm, tn = _tile(m, 1024), _tile(d, 512, 128)
    nj = d // tn
    return pl.pallas_call(
        _glu_merge_kernel,
        grid=(nj, m // tm),
        in_specs=[pl.BlockSpec((tm, kdim), lambda j, i: (i, 0)),
                  pl.BlockSpec((kdim, tn), lambda j, i: (0, j)),
                  pl.BlockSpec((kdim, tn), lambda j, i: (0, j + nj)),
                  pl.BlockSpec((tm, tn), lambda j, i: (i, j + nj)),
                  pl.BlockSpec((tm, tn), lambda j, i: (i, j))],
        out_specs=pl.BlockSpec((tm, tn), lambda j, i: (i, j)),
        out_shape=jax.ShapeDtypeStruct((m, d), BF16),
        compiler_params=_params("parallel", "parallel"),
        name="glu_merge",
    )(y, w_glu, w_glu, gate_sig, oa_gated)


def _qproj_kernel(a_ref, w_ref, *rest, scale, use_rope):
    if use_rope:
        cos_ref, sin_ref, o_ref = rest
    else:
        (o_ref,) = rest
    nb, hb, tl, _ = o_ref.shape
    cols = w_ref.shape[1] // hb
    acc_all = jnp.dot(a_ref[...], w_ref[...], preferred_element_type=F32)
    for h in range(hb):
        acc = acc_all[:, h * cols:(h + 1) * cols]
        o_ref[:, h, :, :QK_NOPE] = (acc[:, :QK_NOPE] * scale).reshape(nb, tl, QK_NOPE).astype(o_ref.dtype)
        rope = acc[:, QK_NOPE:QK_NOPE + QK_ROPE]
        if use_rope:
            rope = rope * cos_ref[...] + acc[:, QK_NOPE + QK_ROPE:QK_NOPE + 2 * QK_ROPE] * sin_ref[...]
        o_ref[:, h, :, QK_NOPE:] = (rope * scale).reshape(nb, tl, QK_ROPE).astype(o_ref.dtype)


HEADS_PER_STEP = 4


def _row_blocking(batch, length, target):
    if length >= target:
        return 1, _tile(length, target)
    return _tile(batch, max(1, target // length), 1), length


def _head_out_spec(nb, hb, tl, length, width):
    per_batch = length // tl
    return pl.BlockSpec((nb, hb, tl, width), lambda i, h: (i // per_batch, h, i % per_batch, 0))


def _qproj(qn, w_heads, batch, length, rope_tables):
    n, kdim = qn.shape
    heads = N_HEADS
    cols = w_heads.shape[1] // heads
    use_rope = rope_tables is not None
    nb, tl = (1, _tile(length, 1024)) if use_rope else _row_blocking(batch, length, 1024)
    tm = nb * tl
    hb = _tile(heads, HEADS_PER_STEP, 1)
    in_specs = [pl.BlockSpec((tm, kdim), lambda i, h: (i, 0)),
                pl.BlockSpec((kdim, hb * cols), lambda i, h: (0, h))]
    args = [qn, w_heads]
    if use_rope:
        per_batch = length // tl
        spec = pl.BlockSpec((tl, QK_ROPE), lambda i, h: (i % per_batch, 0))
        in_specs += [spec, spec]
        args += list(rope_tables)
    width = QK_NOPE + QK_ROPE
    return pl.pallas_call(
        functools.partial(_qproj_kernel, scale=width ** -0.5, use_rope=use_rope),
        grid=(n // tm, heads // hb),
        in_specs=in_specs,
        out_specs=_head_out_spec(nb, hb, tl, length, width),
        out_shape=jax.ShapeDtypeStruct((batch, heads, length, width), BF16),
        compiler_params=_params("parallel", "parallel"),
        name="q_proj",
    )(*args)


def _kvdec_kernel(c_ref, kr_ref, w_ref, k_ref, v_ref):
    nb, hb, tl, _ = k_ref.shape
    cols = QK_NOPE + V_HEAD
    acc_all = jnp.dot(c_ref[...].astype(BF16), w_ref[...], preferred_element_type=F32)
    k_rope = kr_ref[...].reshape(nb, tl, QK_ROPE).astype(k_ref.dtype)
    for h in range(hb):
        acc = acc_all[:, h * cols:(h + 1) * cols]
        k_ref[:, h, :, :QK_NOPE] = acc[:, :QK_NOPE].reshape(nb, tl, QK_NOPE).astype(k_ref.dtype)
        k_ref[:, h, :, QK_NOPE:] = k_rope
        v_ref[:, h] = acc[:, QK_NOPE:].reshape(nb, tl, V_HEAD).astype(v_ref.dtype)


def _kvdec(ckv, krope, w_heads, batch, length):
    n, kdim = ckv.shape
    heads, cols = N_HEADS, QK_NOPE + V_HEAD
    nb, tl = _row_blocking(batch, length, 1536)
    tm = nb * tl
    hb = _tile(heads, HEADS_PER_STEP, 1)
    return pl.pallas_call(
        _kvdec_kernel,
        grid=(n // tm, heads // hb),
        in_specs=[pl.BlockSpec((tm, kdim), lambda i, h: (i, 0)),
                  pl.BlockSpec((tm, QK_ROPE), lambda i, h: (i, 0)),
                  pl.BlockSpec((kdim, hb * cols), lambda i, h: (0, h))],
        out_specs=[_head_out_spec(nb, hb, tl, length, QK_NOPE + QK_ROPE),
                   _head_out_spec(nb, hb, tl, length, V_HEAD)],
        out_shape=[jax.ShapeDtypeStruct((batch, heads, length, QK_NOPE + QK_ROPE), BF16),
                   jax.ShapeDtypeStruct((batch, heads, length, V_HEAD), BF16)],
        compiler_params=_params("parallel", "parallel"),
        name="kv_decompress",
    )(ckv, krope, w_heads)


def _qk(q, k):
    return lax.dot_general(q, k, (((1,), (1,)), ((), ())), preferred_element_type=F32)


def _attn_kernel(q_ref, k_ref, v_ref, gate_ref, o_ref, *stat_refs, tk, nk, sub):
    hb, tq, dv = q_ref.shape[1], q_ref.shape[2], v_ref.shape[3]
    if nk > 1:
        m_ref, l_ref, acc_ref = stat_refs
    lanes = dv
    units = [(h, j, r) for h in range(hb) for j in range(nk) for r in range(tq // sub)]

    def scores(unit):
        h, j, r = unit
        return _qk(q_ref[0, h, pl.ds(r * sub, sub), :], k_ref[0, h, j * tk:(j + 1) * tk, :])

    def write_out(h, rows, acc, l):
        cols = slice(h * dv, (h + 1) * dv)
        o_ref[0, rows, cols] = (acc / l * gate_ref[0, rows, cols].astype(F32)).astype(o_ref.dtype)

    s_next = scores(units[0])
    for idx, (h, j, r) in enumerate(units):
        s = s_next
        if idx + 1 < len(units):
            s_next = scores(units[idx + 1])
        rows = pl.ds(r * sub, sub)
        keys = slice(j * tk, (j + 1) * tk)
        if j == 0:
            m = jnp.max(s, axis=-1, keepdims=True)
            p = jnp.exp(s - m)
            l = jnp.sum(p, axis=-1, keepdims=True)
            acc = jnp.dot(p.astype(BF16), v_ref[0, h, keys, :], preferred_element_type=F32)
            if nk > 1:
                m = jnp.broadcast_to(m, (sub, lanes))
                l = jnp.broadcast_to(l, (sub, lanes))
        else:
            m_old = m_ref[h, rows, :]
            m = jnp.maximum(m_old, jnp.max(s, axis=-1, keepdims=True))
            alpha = jnp.exp(m_old - m)
            p = jnp.exp(s - m[:, :1])
            l = alpha * l_ref[h, rows, :] + jnp.sum(p, axis=-1, keepdims=True)
            acc = alpha * acc_ref[h, rows, :] + jnp.dot(
                p.astype(BF16), v_ref[0, h, keys, :], preferred_element_type=F32)
        if j < nk - 1:
            m_ref[h, rows, :] = m
            l_ref[h, rows, :] = l
            acc_ref[h, rows, :] = acc
        else:
            write_out(h, rows, acc, l)


ATTN_MAX_KV_CHUNK = 2304
ATTN_STEP_ROWS = 2048


def _attention(q, k, v, gate):
    batch, heads, lq, dqk = q.shape
    lk, dv = k.shape[2], v.shape[3]
    tq = _tile(lq, 2048)
    sub = _tile(tq, 512)
    hb = _tile(heads, max(1, ATTN_STEP_ROWS // lq), 1) if tq == lq else 1
    tk = _tile(lk, ATTN_MAX_KV_CHUNK, 128) if lk > ATTN_MAX_KV_CHUNK else lk
    nk = lk // tk
    stats = [pltpu.VMEM((hb, tq, dv), F32)] * 3 if nk > 1 else []
    return pl.pallas_call(
        functools.partial(_attn_kernel, tk=tk, nk=nk, sub=sub),
        grid=(batch, heads // hb, lq // tq),
        in_specs=[pl.BlockSpec((1, hb, tq, dqk), lambda b, h, i: (b, h, i, 0)),
                  pl.BlockSpec((1, hb, lk, dqk), lambda b, h, i: (b, h, 0, 0)),
                  pl.BlockSpec((1, hb, lk, dv), lambda b, h, i: (b, h, 0, 0)),
                  pl.BlockSpec((1, tq, hb * dv), lambda b, h, i: (b, i, h))],
        out_specs=pl.BlockSpec((1, tq, hb * dv), lambda b, h, i: (b, i, h)),
        out_shape=jax.ShapeDtypeStruct((batch, lq, heads * dv), BF16),
        scratch_shapes=stats,
        compiler_params=_params("parallel", "parallel", "arbitrary"),
        name="attention",
    )(q, k, v, gate)


def _gelu_tanh(x):
    return 0.5 * x * (1.0 + jnp.tanh(math.sqrt(2.0 / math.pi) * (x + 0.044715 * (x * x * x))))


def _s5_kernel(u_ref, w_ref, g_ref, e_ref, d_ref, ar_ref, ai_ref, h0_ref, y_ref, fin_ref, *, n_chunks):
    u = u_ref[0]
    rows = u.shape[0]
    batch = rows // n_chunks
    half = g_ref.shape[2] // 2
    g = jnp.dot(u, g_ref[0], preferred_element_type=F32)
    h0 = jnp.broadcast_to(h0_ref[0][:, None, :], (batch, n_chunks, 2 * half)).reshape(rows, 2 * half)
    chunk = lax.broadcasted_iota(jnp.int32, (rows, half), 0) % n_chunks
    ar = ar_ref[0]
    ai = ai_ref[0]

    def cmul(x, step, lo):
        swapped = pltpu.roll(x, half // 2, axis=1)
        return x * ar[step:step + 1, lo:lo + half] + swapped * ai[step:step + 1, lo:lo + half]

    def scan(gd, h0d, lo, backward):
        if backward:
            z = jnp.where(chunk == n_chunks - 1, h0d, pltpu.roll(gd, rows - 1, axis=0))
        else:
            z = jnp.where(chunk == 0, h0d, pltpu.roll(gd, 1, axis=0))
        step, dist = 0, 1
        while dist < n_chunks:
            if backward:
                moved = jnp.where(chunk < n_chunks - dist, pltpu.roll(z, rows - dist, axis=0), 0.0)
            else:
                moved = jnp.where(chunk >= dist, pltpu.roll(z, dist, axis=0), 0.0)
            z = z + cmul(moved, step, lo)
            step, dist = step + 1, dist * 2
        return z

    gf, gb = g[:, :half], g[:, half:]
    pf = scan(gf, h0[:, :half], 0, False)
    pb = scan(gb, h0[:, half:], half, True)
    p = jnp.concatenate([pf, pb], axis=1).astype(BF16)
    y = (jnp.dot(u, w_ref[0], preferred_element_type=F32)
         + jnp.dot(p, e_ref[0], preferred_element_type=F32)
         + u.astype(F32) * d_ref[0])
    y_ref[0] = _gelu_tanh(y).astype(y_ref.dtype)
    def pick(x, which):
        return jnp.sum(jnp.where(chunk == which, x, 0.0).reshape(batch, n_chunks, half), axis=1)

    ff = pick(cmul(pf, 0, 0) + gf, n_chunks - 1)
    fb = pick(cmul(pb, 0, half) + gb, 0)
    fin_ref[0] = jnp.concatenate([ff, fb], axis=1)


def _s5(u_rows, ops, h0, n_chunks):
    groups, rows, width = u_rows.shape
    batch = rows // n_chunks
    wt, gt, et, dvec, ar, ai = ops
    lanes = gt.shape[2]

    def per_group(*shape):
        return pl.BlockSpec((1,) + shape, lambda g: (g,) + (0,) * len(shape))

    return pl.pallas_call(
        functools.partial(_s5_kernel, n_chunks=n_chunks),
        grid=(groups,),
        in_specs=[per_group(rows, width), per_group(width, width), per_group(width, lanes),
                  per_group(lanes, width), per_group(1, width), per_group(*ar.shape[1:]),
                  per_group(*ai.shape[1:]), per_group(batch, lanes)],
        out_specs=[per_group(rows, width), per_group(batch, lanes)],
        out_shape=[jax.ShapeDtypeStruct((groups, rows, width), BF16),
                   jax.ShapeDtypeStruct((groups, batch, lanes), F32)],
        compiler_params=_params("parallel"),
        name="s5",
    )(u_rows, wt, gt, et, dvec, ar, ai, h0)


def _s5_operators(lam_re, lam_im, log_dt, b_re, b_im, c_re, c_im, d_skip, chunk, max_chunks):
    hi = lax.Precision.HIGHEST
    groups, p_dim, j_dim = b_re.shape[1:]
    t = chunk
    lam = lax.complex(lam_re, lam_im)
    z = lam * jnp.exp(log_dt)[..., None]
    a = jnp.exp(z)
    b_bar = ((a - 1.0) / lam)[..., None] * lax.complex(b_re, b_im)
    c_mat = lax.complex(c_re, c_im)
    k_idx = jnp.arange(t + 1, dtype=F32)
    apow = jnp.exp(z[None] * k_idx[:, None, None, None])

    cb = c_mat[..., None] * b_bar[:, :, None]
    ap = jnp.moveaxis(apow[:t], 1, 0)
    kern = (jnp.einsum('dtgp,dgopi->dtgoi', ap.real, cb.real, precision=hi)
            - jnp.einsum('dtgp,dgopi->dtgoi', ap.imag, cb.imag, precision=hi))
    zeros = jnp.zeros((t - 1,) + kern.shape[2:], F32)
    k_lag = jnp.concatenate([zeros, kern[0]], axis=0) + jnp.concatenate([kern[1][::-1], zeros], axis=0)
    k_lag = k_lag.transpose(1, 3, 0, 2).reshape(groups, j_dim, (2 * t - 1) * j_dim).astype(BF16)
    w_all = jnp.stack([k_lag[:, :, (t - 1 - j) * j_dim:(2 * t - 1 - j) * j_dim] for j in range(t)], axis=1)
    w_all = w_all.reshape(groups, t * j_dim, t * j_dim)

    apow_tp = jnp.moveaxis(apow, 0, 2)
    b_t = jnp.swapaxes(b_bar, -1, -2)
    inj_f = apow_tp[0][:, t - 1::-1][:, :, None, :] * b_t[0][:, None]
    inj_b = apow_tp[1][:, :t][:, :, None, :] * b_t[1][:, None]
    g_all = jnp.concatenate([inj_f.real, inj_f.imag, inj_b.real, inj_b.imag], axis=-1)
    g_all = g_all.reshape(groups, t * j_dim, 4 * p_dim)

    apow_pt = jnp.moveaxis(apow, 0, -1)
    c_t = jnp.swapaxes(c_mat, -1, -2)
    out_f = c_t[0][:, :, None, :] * apow_pt[0][:, :, 1:t + 1][..., None]
    out_b = c_t[1][:, :, None, :] * apow_pt[1][:, :, t:0:-1][..., None]
    e_all = jnp.concatenate([out_f.real, -out_f.imag, out_b.real, -out_b.imag], axis=1)
    e_all = e_all.reshape(groups, 4 * p_dim, t * j_dim)

    d_row = jnp.tile(d_skip.reshape(groups, 1, j_dim), (1, t, 1)).reshape(groups, 1, t * j_dim)

    n_steps = max(1, (max_chunks - 1).bit_length())
    steps = (t * 2 ** jnp.arange(n_steps)).astype(F32)
    tp = jnp.exp(z[None] * steps[:, None, None, None])
    a_r = jnp.concatenate([tp[:, 0].real, tp[:, 0].real, tp[:, 1].real, tp[:, 1].real], axis=-1)
    a_i = jnp.concatenate([-tp[:, 0].imag, tp[:, 0].imag, -tp[:, 1].imag, tp[:, 1].imag], axis=-1)
    a_r = a_r.transpose(1, 0, 2)
    a_i = a_i.transpose(1, 0, 2)
    return w_all, g_all.astype(BF16), e_all.astype(BF16), d_row, a_r, a_i


def _postmix_kernel(x_ref, y_ref, gpost_ref, gpre_ref, mod_ref, x1_ref, h_ref):
    mod = mod_ref[0]
    x1 = x_ref[...] + mod[2:3] * _rms(y_ref[...].astype(F32), gpost_ref[...])
    x1_ref[...] = x1
    h_ref[...] = (_rms(x1, gpre_ref[...]) * (1.0 + mod[4:5]) + mod[3:4]).astype(h_ref.dtype)


def _postmix(x, y, g_post, g_pre, mod):
    n, d = x.shape
    tm = _tile(n // mod.shape[0], 256)
    row = pl.BlockSpec((tm, d), lambda i: (i, 0))
    gain = pl.BlockSpec((1, d), lambda i: (0, 0))
    return pl.pallas_call(
        _postmix_kernel,
        grid=(n // tm,),
        in_specs=[row, row, gain, gain, _mod_spec(mod, n, tm)],
        out_specs=[row, row],
        out_shape=[jax.ShapeDtypeStruct((n, d), F32), jax.ShapeDtypeStruct((n, d), BF16)],
        compiler_params=_params("parallel"),
        name="post_mix",
    )(x, y, g_post, g_pre, mod)


def _postmlp_kernel(x_ref, y_ref, g_ref, mod_ref, o_ref):
    o_ref[...] = x_ref[...] + mod_ref[0][5:6] * _rms(y_ref[...].astype(F32), g_ref[...])


def _postmlp(x, y, g_post, mod):
    n, d = x.shape
    tm = _tile(n // mod.shape[0], 256)
    row = pl.BlockSpec((tm, d), lambda i: (i, 0))
    return pl.pallas_call(
        _postmlp_kernel,
        grid=(n // tm,),
        in_specs=[row, row, pl.BlockSpec((1, d), lambda i: (0, 0)), _mod_spec(mod, n, tm)],
        out_specs=row,
        out_shape=jax.ShapeDtypeStruct((n, d), F32),
        compiler_params=_params("parallel"),
        name="post_mlp",
    )(x, y, g_post, mod)


def _axial_rope(n_tokens):
    rows = n_tokens // GRID_W
    row = jnp.repeat(jnp.arange(rows, dtype=F32), GRID_W)
    col = jnp.tile(jnp.arange(GRID_W, dtype=F32), rows)
    n_freq = QK_ROPE // 4
    inv_freq = ROPE_THETA ** (-jnp.arange(n_freq, dtype=F32) / n_freq)
    ang = jnp.concatenate([row[:, None] * inv_freq, col[:, None] * inv_freq], axis=-1)
    ang = jnp.concatenate([ang, ang], axis=-1)
    return jnp.cos(ang), jnp.sin(ang)


def _rotate_half_cols(w):
    half = w.shape[-1] // 2
    return jnp.concatenate([-w[..., half:], w[..., :half]], axis=-1)


def _prepare_weights(w_in, w_uq, w_ukv, w_glu, w_out, w_ff1, w_ff2, q_lora, kv_lora):
    uq = w_uq.reshape(q_lora, N_HEADS, QK_NOPE + QK_ROPE)
    uq_rope = jnp.concatenate([uq, _rotate_half_cols(uq[..., QK_NOPE:])], axis=-1).reshape(q_lora, -1)
    cast = lambda w: w.astype(BF16)
    return dict(w_in_t=cast(jnp.transpose(w_in)),
                uq_plain=cast(w_uq), uq_rope=cast(uq_rope), ukv=cast(w_ukv), w_glu=cast(w_glu),
                w_out=w_out, w_ff1=w_ff1, w_ff2=cast(w_ff2))


def _slab_permutation(j_dim):
    gb = LANES // j_dim
    src = np.arange(gb * LANES)
    i8, g8, j = src // LANES, (src % LANES) // j_dim, src % j_dim
    p = np.zeros((gb * LANES, gb * LANES), np.float32)
    p[src, g8 * LANES + i8 * j_dim + j] = 1.0
    return jnp.asarray(p, BF16)


def _to_rows_kernel(u_ref, p_ref, o_ref, *, chunk):
    gb, rows, _ = o_ref.shape
    tiles = [u_ref[pl.ds(i, rows, stride=chunk), :].astype(BF16) for i in range(chunk)]
    for k in range(chunk // gb):
        z = jnp.concatenate(tiles[gb * k:gb * (k + 1)], axis=1)
        uk = jnp.dot(z, p_ref[...], preferred_element_type=F32).astype(o_ref.dtype)
        for g in range(gb):
            o_ref[g, :, k * LANES:(k + 1) * LANES] = uk[:, g * LANES:(g + 1) * LANES]


def _to_group_rows(u, perm, groups, j_dim, chunk):
    n = u.shape[0]
    gb = LANES // j_dim
    rows = _tile(n // chunk, 256, 16)
    return pl.pallas_call(
        functools.partial(_to_rows_kernel, chunk=chunk),
        grid=(groups // gb, n // (rows * chunk)),
        in_specs=[pl.BlockSpec((rows * chunk, LANES), lambda g, r: (r, g)),
                  pl.BlockSpec(perm.shape, lambda g, r: (0, 0))],
        out_specs=pl.BlockSpec((gb, rows, chunk * j_dim), lambda g, r: (g, r, 0)),
        out_shape=jax.ShapeDtypeStruct((groups, n // chunk, chunk * j_dim), BF16),
        compiler_params=_params("parallel", "parallel"),
        name="to_group_rows",
    )(u, perm)


def _from_rows_kernel(y_ref, p_ref, o_ref, *, chunk):
    gb, rows, _ = y_ref.shape
    for k in range(chunk // gb):
        yk = jnp.concatenate([y_ref[g, :, k * LANES:(k + 1) * LANES] for g in range(gb)], axis=1)
        z = jnp.dot(yk, p_ref[...], preferred_element_type=F32)
        for i8 in range(gb):
            o_ref[pl.ds(gb * k + i8, rows, stride=chunk), :] = z[:, i8 * LANES:(i8 + 1) * LANES]


def _from_group_rows(y, perm_t, groups, j_dim, chunk):
    n_rows = y.shape[1]
    gb = LANES // j_dim
    rows = _tile(n_rows, 256, 16)
    return pl.pallas_call(
        functools.partial(_from_rows_kernel, chunk=chunk),
        grid=(groups // gb, n_rows // rows),
        in_specs=[pl.BlockSpec((gb, rows, chunk * j_dim), lambda g, r: (g, r, 0)),
                  pl.BlockSpec(perm_t.shape, lambda g, r: (0, 0))],
        out_specs=pl.BlockSpec((rows * chunk, LANES), lambda g, r: (r, g)),
        out_shape=jax.ShapeDtypeStruct((n_rows * chunk, groups * j_dim), F32),
        compiler_params=_params("parallel", "parallel"),
        name="from_group_rows",
    )(y, perm_t)


def _trunk(x, mod, wts, gains, s5_ops, s5_dims, ctx):
    batch, length, d_model = x.shape
    n = batch * length
    groups, p_dim, j_dim, chunk = s5_dims
    x2 = x.reshape(n, d_model)
    rope = None if ctx is None else ctx[3]

    h = _modnorm(x2, gains['g_pre_mix'], mod, 0, 1)
    q_lora, kv_lora = gains['g_q_lat'].shape[1], gains['g_kv_lat'].shape[1]
    s5_width = groups * j_dim
    wt = wts['w_in_t']
    qn = _mm_wt(h, wt, 0, q_lora, BF16, epilogue="rmsnorm", gain=gains['g_q_lat'])
    ckv_n, k_rope = _mm_ck(h, wt, q_lora, gains['g_kv_lat'], length, rope)
    u = _mm_wt(h, wt, q_lora + kv_lora + QK_ROPE, s5_width, F32)
    gate_sig = _mm_wt(h, wt, q_lora + kv_lora + QK_ROPE + s5_width, 2 * d_model, BF16, epilogue="sigmoid")

    q = _qproj(qn, wts['uq_plain'] if rope is None else wts['uq_rope'], batch, length, rope)
    if ctx is None:
        ckv_all, kr_all, lk = ckv_n, k_rope, length
    else:
        lk = ctx[0].shape[1] + length
        ckv_all = jnp.concatenate([ctx[0], ckv_n.reshape(batch, length, -1)], axis=1).reshape(batch * lk, -1)
        kr_all = jnp.concatenate([ctx[1], k_rope.reshape(batch, length, -1)], axis=1).reshape(batch * lk, -1)
    k, v = _kvdec(ckv_all, kr_all, wts['ukv'], batch, lk)
    oa_gated = _attention(q, k, v, gate_sig.reshape(batch, length, -1)).reshape(n, -1)

    if ctx is None:
        h0 = jnp.zeros((groups, batch, 4 * p_dim), F32)
    else:
        h0 = ctx[2].transpose(3, 0, 1, 2, 4).reshape(groups, batch, 4 * p_dim)
    perm = _slab_permutation(j_dim)
    u_rows = _to_group_rows(u, perm, groups, j_dim, chunk)
    y_rows, s5_fin = _s5(u_rows, s5_ops, h0, length // chunk)
    y = _from_group_rows(y_rows, perm.T, groups, j_dim, chunk)
    m = _glu_merge(y, wts['w_glu'], gate_sig, oa_gated)
    mixed = _mm(m, wts['w_out'], BF16)
    x1, h2 = _postmix(x2, mixed, gains['g_post_mix'], gains['g_pre_mlp'], mod)
    a1 = _mm(h2, wts['w_ff1'], BF16, epilogue="relu2")
    a2 = _mm(a1, wts['w_ff2'], BF16)
    y_out = _postmlp(x1, a2, gains['g_post_mlp'], mod).reshape(batch, length, d_model)
    s5_fin = s5_fin.reshape(groups, batch, 2, 2, p_dim).transpose(1, 2, 3, 0, 4)
    return y_out, ckv_n.reshape(batch, length, -1), k_rope.reshape(batch, length, -1), s5_fin


def kernel(x_prompt, x_sample, c, cache_ckv, cache_krope, state_s5, c_ctx, w_mod, b_mod, g_pre_mix, w_in, g_q_lat, g_kv_lat, w_uq, w_ukv, s5_lam_re, s5_lam_im, s5_log_dt, s5_b_re, s5_b_im, s5_c_re, s5_c_im, s5_d, w_glu, w_out, g_post_mix, g_pre_mlp, w_ff1, w_ff2, g_post_mlp):
    depth = w_mod.shape[0]
    d_model = x_prompt.shape[-1]
    dec_batch, dec_seq = x_sample.shape[:2]
    q_lora, kv_lora = g_q_lat.shape[1], g_kv_lat.shape[1]
    groups, p_dim, j_dim = s5_b_re.shape[2:]
    chunk = min(S5_CHUNK, x_prompt.shape[1] // 8)
    rope = _axial_rope(dec_seq)

    cond = jnp.concatenate([c_ctx[None, :], c], axis=0)
    cond = jnp.pad(cond, ((0, -cond.shape[0] % 8), (0, 0)))

    xp, xs = x_prompt, x_sample
    new_ckv, new_krope, new_s5 = [], [], []
    for l in range(depth):
        mods = _adaln(cond, w_mod[l], b_mod[l][None, :])
        mod_p = mods[0:1].reshape(1, -1, d_model)
        mod_s = mods[1:1 + dec_batch].reshape(dec_batch, -1, d_model)
        wts = _prepare_weights(w_in[l], w_uq[l], w_ukv[l], w_glu[l], w_out[l], w_ff1[l], w_ff2[l],
                               q_lora, kv_lora)
        gains = dict(g_pre_mix=g_pre_mix[l][None], g_q_lat=g_q_lat[l][None], g_kv_lat=g_kv_lat[l][None],
                     g_post_mix=g_post_mix[l][None], g_pre_mlp=g_pre_mlp[l][None], g_post_mlp=g_post_mlp[l][None])
        max_chunks = max(x_prompt.shape[1], dec_seq) // chunk
        s5_ops = _s5_operators(s5_lam_re[l], s5_lam_im[l], s5_log_dt[l], s5_b_re[l], s5_b_im[l],
                               s5_c_re[l], s5_c_im[l], s5_d[l], chunk, max_chunks)
        s5_dims = (groups, p_dim, j_dim, chunk)

        xp, ckv_n, k_rope_p, s5_last = _trunk(xp, mod_p, wts, gains, s5_ops, s5_dims, None)
        new_ckv.append(ckv_n)
        new_krope.append(k_rope_p)
        new_s5.append(s5_last)
        ctx = (cache_ckv[:, l], cache_krope[:, l], state_s5[:, l], rope)
        xs, _, _, _ = _trunk(xs, mod_s, wts, gains, s5_ops, s5_dims, ctx)
    return (xp, xs, jnp.stack(new_ckv, axis=1), jnp.stack(new_krope, axis=1),
            jnp.stack(new_s5, axis=1).astype(x_prompt.dtype))
```

```python
import functools
import math

import jax
import jax.numpy as jnp
import numpy as np
from jax import lax
from jax.experimental import pallas as pl
from jax.experimental.pallas import tpu as pltpu

N_HEADS = 32
QK_NOPE = 128
QK_ROPE = 64
V_HEAD = 128
GRID_W = 64
ROPE_THETA = 10000.0
EPS = 1e-6
S5_CHUNK = 32
LANES = 128
VMEM_LIMIT_BYTES = 48 * 1024 * 1024

F32 = jnp.float32
BF16 = jnp.bfloat16


VMEM_LIMIT_BIG_BYTES = 56 * 1024 * 1024


def _params(*semantics, vmem=VMEM_LIMIT_BYTES):
    return pltpu.CompilerParams(dimension_semantics=semantics, vmem_limit_bytes=vmem)


def _tile(n, target, align=8):
    for d in range(min(n, target), 0, -1):
        if n % d == 0 and d % align == 0:
            return d
    return n


def _sigmoid(x):
    return 0.5 * jnp.tanh(0.5 * x) + 0.5


def _adaln_kernel(c_ref, w_ref, b_ref, o_ref):
    c = c_ref[...]
    s = (c * _sigmoid(c)).astype(BF16)
    o_ref[...] = jnp.dot(s, w_ref[...].astype(BF16), preferred_element_type=F32) + b_ref[...]


def _adaln(cond, w_mod, b_mod):
    rows, d = cond.shape
    n = w_mod.shape[1]
    tn = _tile(n, 512, 128)
    return pl.pallas_call(
        _adaln_kernel,
        grid=(n // tn,),
        in_specs=[pl.BlockSpec((rows, d), lambda j: (0, 0)),
                  pl.BlockSpec((d, tn), lambda j: (0, j)),
                  pl.BlockSpec((1, tn), lambda j: (0, j))],
        out_specs=pl.BlockSpec((rows, tn), lambda j: (0, j)),
        out_shape=jax.ShapeDtypeStruct((rows, n), F32),
        compiler_params=_params("parallel"),
        name="adaln",
    )(cond, w_mod, b_mod)


def _rms(x, g):
    return x * lax.rsqrt(jnp.mean(x * x, axis=-1, keepdims=True) + EPS) * g


def _modnorm_kernel(x_ref, g_ref, mod_ref, o_ref, *, shift_idx, scale_idx):
    mod = mod_ref[0]
    y = _rms(x_ref[...], g_ref[...])
    o_ref[...] = (y * (1.0 + mod[scale_idx:scale_idx + 1]) + mod[shift_idx:shift_idx + 1]).astype(o_ref.dtype)


def _mod_spec(mod, n_rows, tm):
    rows_per_mod = n_rows // mod.shape[0]
    return pl.BlockSpec((1,) + mod.shape[1:], lambda i: ((i * tm) // rows_per_mod, 0, 0))


def _modnorm(x, g, mod, shift_idx, scale_idx):
    n, d = x.shape
    tm = _tile(n // mod.shape[0], 256)
    return pl.pallas_call(
        functools.partial(_modnorm_kernel, shift_idx=shift_idx, scale_idx=scale_idx),
        grid=(n // tm,),
        in_specs=[pl.BlockSpec((tm, d), lambda i: (i, 0)),
                  pl.BlockSpec((1, d), lambda i: (0, 0)),
                  _mod_spec(mod, n, tm)],
        out_specs=pl.BlockSpec((tm, d), lambda i: (i, 0)),
        out_shape=jax.ShapeDtypeStruct((n, d), BF16),
        compiler_params=_params("parallel"),
        name="modnorm",
    )(x, g, mod)


def _epilogue(r, kind):
    if kind == "relu2":
        return jnp.square(jnp.maximum(r, 0.0))
    if kind == "sigmoid":
        return _sigmoid(r)
    assert kind is None
    return r


def _bf16_panel(w_ref, scratch_ref):
    if scratch_ref is None:
        return w_ref[...]

    @pl.when(pl.program_id(1) == 0)
    def _():
        scratch_ref[...] = w_ref[...].astype(BF16)

    return scratch_ref[...]


def _mm_full_k_kernel(a_ref, w_ref, o_ref, wb_ref=None, *, epilogue):
    w = _bf16_panel(w_ref, wb_ref)
    r = jnp.dot(a_ref[...], w, preferred_element_type=F32)
    o_ref[...] = _epilogue(r, epilogue).astype(o_ref.dtype)


def _dot_nt(a, wt):
    return lax.dot_general(a, wt, (((1,), (1,)), ((), ())), preferred_element_type=F32)


def _mm_wt_kernel(a_ref, wt_ref, *rest, epilogue):
    r = _dot_nt(a_ref[...], wt_ref[...])
    if epilogue == "rmsnorm":
        g_ref, o_ref = rest
        r = _rms(r, g_ref[...])
    else:
        (o_ref,) = rest
        r = _epilogue(r, epilogue)
    o_ref[...] = r.astype(o_ref.dtype)


def _mm_wt(a, wt, row0, n, out_dtype, epilogue=None, gain=None):
    m, kdim = a.shape
    tm = _tile(m, 512)
    tn = n if epilogue == "rmsnorm" else _tile(n, 1024, 128)
    in_specs = [pl.BlockSpec((tm, kdim), lambda j, i: (i, 0)),
                pl.BlockSpec((pl.Element(tn), pl.Element(kdim)),
                             lambda j, i: (pl.multiple_of(row0 + j * tn, 16), 0))]
    args = [a, wt]
    if epilogue == "rmsnorm":
        in_specs.append(pl.BlockSpec((1, n), lambda j, i: (0, 0)))
        args.append(gain)
    return pl.pallas_call(
        functools.partial(_mm_wt_kernel, epilogue=epilogue),
        grid=(n // tn, m // tm),
        in_specs=in_specs,
        out_specs=pl.BlockSpec((tm, tn), lambda j, i: (i, j)),
        out_shape=jax.ShapeDtypeStruct((m, n), out_dtype),
        compiler_params=_params("parallel", "parallel"),
        name="matmul_wt",
    )(*args)


def _mm_ck_kernel(a_ref, wt_ref, g_ref, *rest, kv_lora, use_rope):
    if use_rope:
        cos_ref, sin_ref, ckv_ref, kr_ref, wb_ref = rest
    else:
        ckv_ref, kr_ref, wb_ref = rest
    half = QK_ROPE // 2

    @pl.when(pl.program_id(0) == 0)
    def _():
        wb_ref[:kv_lora + QK_ROPE, :] = wt_ref[...]
        wb_ref[kv_lora + QK_ROPE:kv_lora + QK_ROPE + half, :] = -wt_ref[kv_lora + half:, :]
        wb_ref[kv_lora + QK_ROPE + half:, :] = wt_ref[kv_lora:kv_lora + half, :]

    acc = _dot_nt(a_ref[...], wb_ref[...])
    ckv_ref[...] = _rms(acc[:, :kv_lora], g_ref[...])
    k_rope = acc[:, kv_lora:kv_lora + QK_ROPE]
    if use_rope:
        k_rope = k_rope * cos_ref[...] + acc[:, kv_lora + QK_ROPE:] * sin_ref[...]
    kr_ref[...] = k_rope


def _mm_ck(a, wt, row0, g_kv, length, rope_tables):
    m, kdim = a.shape
    kv_lora = g_kv.shape[1]
    use_rope = rope_tables is not None
    tm = _tile(length, 512) if use_rope else _tile(m, 512)
    rows = kv_lora + QK_ROPE
    in_specs = [pl.BlockSpec((tm, kdim), lambda i: (i, 0)),
                pl.BlockSpec((pl.Element(rows), pl.Element(kdim)), lambda i: (row0, 0)),
                pl.BlockSpec((1, kv_lora), lambda i: (0, 0))]
    args = [a, wt, g_kv]
    if use_rope:
        per_batch = length // tm
        spec = pl.BlockSpec((tm, QK_ROPE), lambda i: (i % per_batch, 0))
        in_specs += [spec, spec]
        args += list(rope_tables)
    return pl.pallas_call(
        functools.partial(_mm_ck_kernel, kv_lora=kv_lora, use_rope=use_rope),
        grid=(m // tm,),
        in_specs=in_specs,
        out_specs=[pl.BlockSpec((tm, kv_lora), lambda i: (i, 0)),
                   pl.BlockSpec((tm, QK_ROPE), lambda i: (i, 0))],
        out_shape=[jax.ShapeDtypeStruct((m, kv_lora), F32),
                   jax.ShapeDtypeStruct((m, QK_ROPE), F32)],
        scratch_shapes=[pltpu.VMEM((rows + QK_ROPE, kdim), BF16)],
        compiler_params=_params("arbitrary"),
        name="matmul_ck",
    )(*args)


def _mm_kernel(a_ref, w_ref, o_ref, acc_ref, *, epilogue):
    k = pl.program_id(2)

    @pl.when(k == 0)
    def _():
        acc_ref[...] = jnp.zeros_like(acc_ref)

    acc_ref[...] += jnp.dot(a_ref[...], w_ref[...], preferred_element_type=F32)

    @pl.when(k == pl.num_programs(2) - 1)
    def _():
        o_ref[...] = _epilogue(acc_ref[...], epilogue).astype(o_ref.dtype)


MM_MAX_FULL_K = 4096


def _mm(a, w, out_dtype, epilogue=None):
    m, kdim = a.shape
    n = w.shape[1]
    if kdim <= MM_MAX_FULL_K:
        cast_w = w.dtype != BF16
        tm, tn = _tile(m, 512 if kdim > 2048 else 1024), _tile(n, 1024, 128)
        return pl.pallas_call(
            functools.partial(_mm_full_k_kernel, epilogue=epilogue),
            grid=(n // tn, m // tm),
            in_specs=[pl.BlockSpec((tm, kdim), lambda j, i: (i, 0)),
                      pl.BlockSpec((kdim, tn), lambda j, i: (0, j))],
            out_specs=pl.BlockSpec((tm, tn), lambda j, i: (i, j)),
            out_shape=jax.ShapeDtypeStruct((m, n), out_dtype),
            scratch_shapes=[pltpu.VMEM((kdim, tn), BF16)] if cast_w else [],
            compiler_params=(_params("parallel", "arbitrary", vmem=VMEM_LIMIT_BIG_BYTES) if cast_w
                             else _params("parallel", "parallel")),
            name="matmul",
        )(a, w)
    assert w.dtype == BF16
    tm, tn, tk = _tile(m, 1024), _tile(n, 1024, 128), _tile(kdim, MM_MAX_FULL_K, 128)
    return pl.pallas_call(
        functools.partial(_mm_kernel, epilogue=epilogue),
        grid=(m // tm, n // tn, kdim // tk),
        in_specs=[pl.BlockSpec((tm, tk), lambda i, j, k: (i, k)),
                  pl.BlockSpec((tk, tn), lambda i, j, k: (k, j))],
        out_specs=pl.BlockSpec((tm, tn), lambda i, j, k: (i, j)),
        out_shape=jax.ShapeDtypeStruct((m, n), out_dtype),
        scratch_shapes=[pltpu.VMEM((tm, tn), F32)],
        compiler_params=_params("parallel", "parallel", "arbitrary"),
        name="matmul_k",
    )(a, w)


def _glu_merge_kernel(a_ref, wv_ref, wg_ref, sb_ref, oa_ref, o_ref):
    a = a_ref[...].astype(BF16)
    value = jnp.dot(a, wv_ref[...], preferred_element_type=F32)
    gate = jnp.dot(a, wg_ref[...], preferred_element_type=F32)
    o_b = value * _sigmoid(gate)
    o_ref[...] = (oa_ref[...].astype(F32) + sb_ref[...].astype(F32) * o_b).astype(o_ref.dtype)


def _glu_merge(y, w_glu, gate_sig, oa_gated):
    m, kdim = y.shape
    d = w_glu.shape[1] // 2
    tm, tn = _tile(m, 1024), _tile(d, 512, 128)
    nj = d // tn
    return pl.pallas_call(
        _glu_merge_kernel,
        grid=(nj, m // tm),
        in_specs=[pl.BlockSpec((tm, kdim), lambda j, i: (i, 0)),
                  pl.BlockSpec((kdim, tn), lambda j, i: (0, j)),
                  pl.BlockSpec((kdim, tn), lambda j, i: (0, j + nj)),
                  pl.BlockSpec((tm, tn), lambda j, i: (i, j + nj)),
                  pl.BlockSpec((tm, tn), lambda j, i: (i, j))],
        out_specs=pl.BlockSpec((tm, tn), lambda j, i: (i, j)),
        out_shape=jax.ShapeDtypeStruct((m, d), BF16),
        compiler_params=_params("parallel", "parallel"),
        name="glu_merge",
    )(y, w_glu, w_glu, gate_sig, oa_gated)


def _qproj_kernel(a_ref, w_ref, *rest, scale, use_rope):
    if use_rope:
        cos_ref, sin_ref, o_ref = rest
    else:
        (o_ref,) = rest
    nb, hb, tl, _ = o_ref.shape
    cols = w_ref.shape[1] // hb
    acc_all = jnp.dot(a_ref[...], w_ref[...], preferred_element_type=F32)
    for h in range(hb):
        acc = acc_all[:, h * cols:(h + 1) * cols]
        o_ref[:, h, :, :QK_NOPE] = (acc[:, :QK_NOPE] * scale).reshape(nb, tl, QK_NOPE).astype(o_ref.dtype)
        rope = acc[:, QK_NOPE:QK_NOPE + QK_ROPE]
        if use_rope:
            rope = rope * cos_ref[...] + acc[:, QK_NOPE + QK_ROPE:QK_NOPE + 2 * QK_ROPE] * sin_ref[...]
        o_ref[:, h, :, QK_NOPE:] = (rope * scale).reshape(nb, tl, QK_ROPE).astype(o_ref.dtype)


HEADS_PER_STEP = 4


def _row_blocking(batch, length, target):
    if length >= target:
        return 1, _tile(length, target)
    return _tile(batch, max(1, target // length), 1), length


def _head_out_spec(nb, hb, tl, length, width):
    per_batch = length // tl
    return pl.BlockSpec((nb, hb, tl, width), lambda i, h: (i // per_batch, h, i % per_batch, 0))


def _qproj(qn, w_heads, batch, length, rope_tables):
    n, kdim = qn.shape
    heads = N_HEADS
    cols = w_heads.shape[1] // heads
    use_rope = rope_tables is not None
    nb, tl = (1, _tile(length, 1024)) if use_rope else _row_blocking(batch, length, 1024)
    tm = nb * tl
    hb = _tile(heads, HEADS_PER_STEP, 1)
    in_specs = [pl.BlockSpec((tm, kdim), lambda i, h: (i, 0)),
                pl.BlockSpec((kdim, hb * cols), lambda i, h: (0, h))]
    args = [qn, w_heads]
    if use_rope:
        per_batch = length // tl
        spec = pl.BlockSpec((tl, QK_ROPE), lambda i, h: (i % per_batch, 0))
        in_specs += [spec, spec]
        args += list(rope_tables)
    width = QK_NOPE + QK_ROPE
    return pl.pallas_call(
        functools.partial(_qproj_kernel, scale=width ** -0.5, use_rope=use_rope),
        grid=(n // tm, heads // hb),
        in_specs=in_specs,
        out_specs=_head_out_spec(nb, hb, tl, length, width),
        out_shape=jax.ShapeDtypeStruct((batch, heads, length, width), BF16),
        compiler_params=_params("parallel", "parallel"),
        name="q_proj",
    )(*args)


def _kvdec_kernel(c_ref, kr_ref, w_ref, k_ref, v_ref):
    nb, hb, tl, _ = k_ref.shape
    cols = QK_NOPE + V_HEAD
    acc_all = jnp.dot(c_ref[...].astype(BF16), w_ref[...], preferred_element_type=F32)
    k_rope = kr_ref[...].reshape(nb, tl, QK_ROPE).astype(k_ref.dtype)
    for h in range(hb):
        acc = acc_all[:, h * cols:(h + 1) * cols]
        k_ref[:, h, :, :QK_NOPE] = acc[:, :QK_NOPE].reshape(nb, tl, QK_NOPE).astype(k_ref.dtype)
        k_ref[:, h, :, QK_NOPE:] = k_rope
        v_ref[:, h] = acc[:, QK_NOPE:].reshape(nb, tl, V_HEAD).astype(v_ref.dtype)


def _kvdec(ckv, krope, w_heads, batch, length):
    n, kdim = ckv.shape
    heads, cols = N_HEADS, QK_NOPE + V_HEAD
    nb, tl = _row_blocking(batch, length, 1536)
    tm = nb * tl
    hb = _tile(heads, HEADS_PER_STEP, 1)
    return pl.pallas_call(
        _kvdec_kernel,
        grid=(n // tm, heads // hb),
        in_specs=[pl.BlockSpec((tm, kdim), lambda i, h: (i, 0)),
                  pl.BlockSpec((tm, QK_ROPE), lambda i, h: (i, 0)),
                  pl.BlockSpec((kdim, hb * cols), lambda i, h: (0, h))],
        out_specs=[_head_out_spec(nb, hb, tl, length, QK_NOPE + QK_ROPE),
                   _head_out_spec(nb, hb, tl, length, V_HEAD)],
        out_shape=[jax.ShapeDtypeStruct((batch, heads, length, QK_NOPE + QK_ROPE), BF16),
                   jax.ShapeDtypeStruct((batch, heads, length, V_HEAD), BF16)],
        compiler_params=_params("parallel", "parallel"),
        name="kv_decompress",
    )(ckv, krope, w_heads)


def _qk(q, k):
    return lax.dot_general(q, k, (((1,), (1,)), ((), ())), preferred_element_type=F32)


def _attn_kernel(q_ref, k_ref, v_ref, gate_ref, o_ref, *stat_refs, tk, nk, sub):
    hb, tq, dv = q_ref.shape[1], q_ref.shape[2], v_ref.shape[3]
    if nk > 1:
        m_ref, l_ref, acc_ref = stat_refs
    lanes = dv
    units = [(h, j, r) for h in range(hb) for j in range(nk) for r in range(tq // sub)]

    def scores(unit):
        h, j, r = unit
        return _qk(q_ref[0, h, pl.ds(r * sub, sub), :], k_ref[0, h, j * tk:(j + 1) * tk, :])

    def write_out(h, rows, acc, l):
        cols = slice(h * dv, (h + 1) * dv)
        o_ref[0, rows, cols] = (acc / l * gate_ref[0, rows, cols].astype(F32)).astype(o_ref.dtype)

    s_next = scores(units[0])
    for idx, (h, j, r) in enumerate(units):
        s = s_next
        if idx + 1 < len(units):
            s_next = scores(units[idx + 1])
        rows = pl.ds(r * sub, sub)
        keys = slice(j * tk, (j + 1) * tk)
        if j == 0:
            m = jnp.max(s, axis=-1, keepdims=True)
            p = jnp.exp(s - m)
            l = jnp.sum(p, axis=-1, keepdims=True)
            acc = jnp.dot(p.astype(BF16), v_ref[0, h, keys, :], preferred_element_type=F32)
            if nk > 1:
                m = jnp.broadcast_to(m, (sub, lanes))
                l = jnp.broadcast_to(l, (sub, lanes))
        else:
            m_old = m_ref[h, rows, :]
            m = jnp.maximum(m_old, jnp.max(s, axis=-1, keepdims=True))
            alpha = jnp.exp(m_old - m)
            p = jnp.exp(s - m[:, :1])
            l = alpha * l_ref[h, rows, :] + jnp.sum(p, axis=-1, keepdims=True)
            acc = alpha * acc_ref[h, rows, :] + jnp.dot(
                p.astype(BF16), v_ref[0, h, keys, :], preferred_element_type=F32)
        if j < nk - 1:
            m_ref[h, rows, :] = m
            l_ref[h, rows, :] = l
            acc_ref[h, rows, :] = acc
        else:
            write_out(h, rows, acc, l)


ATTN_MAX_KV_CHUNK = 2304
ATTN_STEP_ROWS = 2048


def _attention(q, k, v, gate):
    batch, heads, lq, dqk = q.shape
    lk, dv = k.shape[2], v.shape[3]
    tq = _tile(lq, 2048)
    sub = _tile(tq, 512)
    hb = _tile(heads, max(1, ATTN_STEP_ROWS // lq), 1) if tq == lq else 1
    tk = _tile(lk, ATTN_MAX_KV_CHUNK, 128) if lk > ATTN_MAX_KV_CHUNK else lk
    nk = lk // tk
    stats = [pltpu.VMEM((hb, tq, dv), F32)] * 3 if nk > 1 else []
    return pl.pallas_call(
        functools.partial(_attn_kernel, tk=tk, nk=nk, sub=sub),
        grid=(batch, heads // hb, lq // tq),
        in_specs=[pl.BlockSpec((1, hb, tq, dqk), lambda b, h, i: (b, h, i, 0)),
                  pl.BlockSpec((1, hb, lk, dqk), lambda b, h, i: (b, h, 0, 0)),
                  pl.BlockSpec((1, hb, lk, dv), lambda b, h, i: (b, h, 0, 0)),
                  pl.BlockSpec((1, tq, hb * dv), lambda b, h, i: (b, i, h))],
        out_specs=pl.BlockSpec((1, tq, hb * dv), lambda b, h, i: (b, i, h)),
        out_shape=jax.ShapeDtypeStruct((batch, lq, heads * dv), BF16),
        scratch_shapes=stats,
        compiler_params=_params("parallel", "parallel", "arbitrary"),
        name="attention",
    )(q, k, v, gate)


def _gelu_tanh(x):
    return 0.5 * x * (1.0 + jnp.tanh(math.sqrt(2.0 / math.pi) * (x + 0.044715 * (x * x * x))))


def _s5_kernel(u_ref, w_ref, g_ref, e_ref, d_ref, ar_ref, ai_ref, h0_ref, y_ref, fin_ref, *, n_chunks):
    u = u_ref[0]
    rows = u.shape[0]
    batch = rows // n_chunks
    half = g_ref.shape[2] // 2
    g = jnp.dot(u, g_ref[0], preferred_element_type=F32)
    h0 = jnp.broadcast_to(h0_ref[0][:, None, :], (batch, n_chunks, 2 * half)).reshape(rows, 2 * half)
    chunk = lax.broadcasted_iota(jnp.int32, (rows, half), 0) % n_chunks
    ar = ar_ref[0]
    ai = ai_ref[0]

    def cmul(x, step, lo):
        swapped = pltpu.roll(x, half // 2, axis=1)
        return x * ar[step:step + 1, lo:lo + half] + swapped * ai[step:step + 1, lo:lo + half]

    def scan(gd, h0d, lo, backward):
        if backward:
            z = jnp.where(chunk == n_chunks - 1, h0d, pltpu.roll(gd, rows - 1, axis=0))
        else:
            z = jnp.where(chunk == 0, h0d, pltpu.roll(gd, 1, axis=0))
        step, dist = 0, 1
        while dist < n_chunks:
            if backward:
                moved = jnp.where(chunk < n_chunks - dist, pltpu.roll(z, rows - dist, axis=0), 0.0)
            else:
                moved = jnp.where(chunk >= dist, pltpu.roll(z, dist, axis=0), 0.0)
            z = z + cmul(moved, step, lo)
            step, dist = step + 1, dist * 2
        return z

    gf, gb = g[:, :half], g[:, half:]
    pf = scan(gf, h0[:, :half], 0, False)
    pb = scan(gb, h0[:, half:], half, True)
    p = jnp.concatenate([pf, pb], axis=1).astype(BF16)
    y = (jnp.dot(u, w_ref[0], preferred_element_type=F32)
         + jnp.dot(p, e_ref[0], preferred_element_type=F32)
         + u.astype(F32) * d_ref[0])
    y_ref[0] = _gelu_tanh(y).astype(y_ref.dtype)
    def pick(x, which):
        return jnp.sum(jnp.where(chunk == which, x, 0.0).reshape(batch, n_chunks, half), axis=1)

    ff = pick(cmul(pf, 0, 0) + gf, n_chunks - 1)
    fb = pick(cmul(pb, 0, half) + gb, 0)
    fin_ref[0] = jnp.concatenate([ff, fb], axis=1)


def _s5(u_rows, ops, h0, n_chunks):
    groups, rows, width = u_rows.shape
    batch = rows // n_chunks
    wt, gt, et, dvec, ar, ai = ops
    lanes = gt.shape[2]

    def per_group(*shape):
        return pl.BlockSpec((1,) + shape, lambda g: (g,) + (0,) * len(shape))

    return pl.pallas_call(
        functools.partial(_s5_kernel, n_chunks=n_chunks),
        grid=(groups,),
        in_specs=[per_group(rows, width), per_group(width, width), per_group(width, lanes),
                  per_group(lanes, width), per_group(1, width), per_group(*ar.shape[1:]),
                  per_group(*ai.shape[1:]), per_group(batch, lanes)],
        out_specs=[per_group(rows, width), per_group(batch, lanes)],
        out_shape=[jax.ShapeDtypeStruct((groups, rows, width), BF16),
                   jax.ShapeDtypeStruct((groups, batch, lanes), F32)],
        compiler_params=_params("parallel"),
        name="s5",
    )(u_rows, wt, gt, et, dvec, ar, ai, h0)


def _s5_operators(lam_re, lam_im, log_dt, b_re, b_im, c_re, c_im, d_skip, chunk, max_chunks):
    hi = lax.Precision.HIGHEST
    groups, p_dim, j_dim = b_re.shape[1:]
    t = chunk
    lam = lax.complex(lam_re, lam_im)
    z = lam * jnp.exp(log_dt)[..., None]
    a = jnp.exp(z)
    b_bar = ((a - 1.0) / lam)[..., None] * lax.complex(b_re, b_im)
    c_mat = lax.complex(c_re, c_im)
    k_idx = jnp.arange(t + 1, dtype=F32)
    apow = jnp.exp(z[None] * k_idx[:, None, None, None])

    cb = c_mat[..., None] * b_bar[:, :, None]
    ap = jnp.moveaxis(apow[:t], 1, 0)
    kern = (jnp.einsum('dtgp,dgopi->dtgoi', ap.real, cb.real, precision=hi)
            - jnp.einsum('dtgp,dgopi->dtgoi', ap.imag, cb.imag, precision=hi))
    zeros = jnp.zeros((t - 1,) + kern.shape[2:], F32)
    k_lag = jnp.concatenate([zeros, kern[0]], axis=0) + jnp.concatenate([kern[1][::-1], zeros], axis=0)
    k_lag = k_lag.transpose(1, 3, 0, 2).reshape(groups, j_dim, (2 * t - 1) * j_dim).astype(BF16)
    w_all = jnp.stack([k_lag[:, :, (t - 1 - j) * j_dim:(2 * t - 1 - j) * j_dim] for j in range(t)], axis=1)
    w_all = w_all.reshape(groups, t * j_dim, t * j_dim)

    apow_tp = jnp.moveaxis(apow, 0, 2)
    b_t = jnp.swapaxes(b_bar, -1, -2)
    inj_f = apow_tp[0][:, t - 1::-1][:, :, None, :] * b_t[0][:, None]
    inj_b = apow_tp[1][:, :t][:, :, None, :] * b_t[1][:, None]
    g_all = jnp.concatenate([inj_f.real, inj_f.imag, inj_b.real, inj_b.imag], axis=-1)
    g_all = g_all.reshape(groups, t * j_dim, 4 * p_dim)

    apow_pt = jnp.moveaxis(apow, 0, -1)
    c_t = jnp.swapaxes(c_mat, -1, -2)
    out_f = c_t[0][:, :, None, :] * apow_pt[0][:, :, 1:t + 1][..., None]
    out_b = c_t[1][:, :, None, :] * apow_pt[1][:, :, t:0:-1][..., None]
    e_all = jnp.concatenate([out_f.real, -out_f.imag, out_b.real, -out_b.imag], axis=1)
    e_all = e_all.reshape(groups, 4 * p_dim, t * j_dim)

    d_row = jnp.tile(d_skip.reshape(groups, 1, j_dim), (1, t, 1)).reshape(groups, 1, t * j_dim)

    n_steps = max(1, (max_chunks - 1).bit_length())
    steps = (t * 2 ** jnp.arange(n_steps)).astype(F32)
    tp = jnp.exp(z[None] * steps[:, None, None, None])
    a_r = jnp.concatenate([tp[:, 0].real, tp[:, 0].real, tp[:, 1].real, tp[:, 1].real], axis=-1)
    a_i = jnp.concatenate([-tp[:, 0].imag, tp[:, 0].imag, -tp[:, 1].imag, tp[:, 1].imag], axis=-1)
    a_r = a_r.transpose(1, 0, 2)
    a_i = a_i.transpose(1, 0, 2)
    return w_all, g_all.astype(BF16), e_all.astype(BF16), d_row, a_r, a_i


def _postmix_kernel(x_ref, y_ref, gpost_ref, gpre_ref, mod_ref, x1_ref, h_ref):
    mod = mod_ref[0]
    x1 = x_ref[...] + mod[2:3] * _rms(y_ref[...].astype(F32), gpost_ref[...])
    x1_ref[...] = x1
    h_ref[...] = (_rms(x1, gpre_ref[...]) * (1.0 + mod[4:5]) + mod[3:4]).astype(h_ref.dtype)


def _postmix(x, y, g_post, g_pre, mod):
    n, d = x.shape
    tm = _tile(n // mod.shape[0], 256)
    row = pl.BlockSpec((tm, d), lambda i: (i, 0))
    gain = pl.BlockSpec((1, d), lambda i: (0, 0))
    return pl.pallas_call(
        _postmix_kernel,
        grid=(n // tm,),
        in_specs=[row, row, gain, gain, _mod_spec(mod, n, tm)],
        out_specs=[row, row],
        out_shape=[jax.ShapeDtypeStruct((n, d), F32), jax.ShapeDtypeStruct((n, d), BF16)],
        compiler_params=_params("parallel"),
        name="post_mix",
    )(x, y, g_post, g_pre, mod)


def _postmlp_kernel(x_ref, y_ref, g_ref, mod_ref, o_ref):
    o_ref[...] = x_ref[...] + mod_ref[0][5:6] * _rms(y_ref[...].astype(F32), g_ref[...])


def _postmlp(x, y, g_post, mod):
    n, d = x.shape
    tm = _tile(n // mod.shape[0], 256)
    row = pl.BlockSpec((tm, d), lambda i: (i, 0))
    return pl.pallas_call(
        _postmlp_kernel,
        grid=(n // tm,),
        in_specs=[row, row, pl.BlockSpec((1, d), lambda i: (0, 0)), _mod_spec(mod, n, tm)],
        out_specs=row,
        out_shape=jax.ShapeDtypeStruct((n, d), F32),
        compiler_params=_params("parallel"),
        name="post_mlp",
    )(x, y, g_post, mod)


def _axial_rope(n_tokens):
    rows = n_tokens // GRID_W
    row = jnp.repeat(jnp.arange(rows, dtype=F32), GRID_W)
    col = jnp.tile(jnp.arange(GRID_W, dtype=F32), rows)
    n_freq = QK_ROPE // 4
    inv_freq = ROPE_THETA ** (-jnp.arange(n_freq, dtype=F32) / n_freq)
    ang = jnp.concatenate([row[:, None] * inv_freq, col[:, None] * inv_freq], axis=-1)
    ang = jnp.concatenate([ang, ang], axis=-1)
    return jnp.cos(ang), jnp.sin(ang)


def _rotate_half_cols(w):
    half = w.shape[-1] // 2
    return jnp.concatenate([-w[..., half:], w[..., :half]], axis=-1)


def _prepare_weights(w_in, w_uq, w_ukv, w_glu, w_out, w_ff1, w_ff2, q_lora, kv_lora):
    uq = w_uq.reshape(q_lora, N_HEADS, QK_NOPE + QK_ROPE)
    uq_rope = jnp.concatenate([uq, _rotate_half_cols(uq[..., QK_NOPE:])], axis=-1).reshape(q_lora, -1)
    cast = lambda w: w.astype(BF16)
    return dict(w_in_t=cast(jnp.transpose(w_in)),
                uq_plain=cast(w_uq), uq_rope=cast(uq_rope), ukv=cast(w_ukv), w_glu=cast(w_glu),
                w_out=w_out, w_ff1=w_ff1, w_ff2=cast(w_ff2))


def _slab_permutation(j_dim):
    gb = LANES // j_dim
    src = np.arange(gb * LANES)
    i8, g8, j = src // LANES, (src % LANES) // j_dim, src % j_dim
    p = np.zeros((gb * LANES, gb * LANES), np.float32)
    p[src, g8 * LANES + i8 * j_dim + j] = 1.0
    return jnp.asarray(p, BF16)


def _to_rows_kernel(u_ref, p_ref, o_ref, *, chunk):
    gb, rows, _ = o_ref.shape
    tiles = [u_ref[pl.ds(i, rows, stride=chunk), :].astype(BF16) for i in range(chunk)]
    for k in range(chunk // gb):
        z = jnp.concatenate(tiles[gb * k:gb * (k + 1)], axis=1)
        uk = jnp.dot(z, p_ref[...], preferred_element_type=F32).astype(o_ref.dtype)
        for g in range(gb):
            o_ref[g, :, k * LANES:(k + 1) * LANES] = uk[:, g * LANES:(g + 1) * LANES]


def _to_group_rows(u, perm, groups, j_dim, chunk):
    n = u.shape[0]
    gb = LANES // j_dim
    rows = _tile(n // chunk, 256, 16)
    return pl.pallas_call(
        functools.partial(_to_rows_kernel, chunk=chunk),
        grid=(groups // gb, n // (rows * chunk)),
        in_specs=[pl.BlockSpec((rows * chunk, LANES), lambda g, r: (r, g)),
                  pl.BlockSpec(perm.shape, lambda g, r: (0, 0))],
        out_specs=pl.BlockSpec((gb, rows, chunk * j_dim), lambda g, r: (g, r, 0)),
        out_shape=jax.ShapeDtypeStruct((groups, n // chunk, chunk * j_dim), BF16),
        compiler_params=_params("parallel", "parallel"),
        name="to_group_rows",
    )(u, perm)


def _from_rows_kernel(y_ref, p_ref, o_ref, *, chunk):
    gb, rows, _ = y_ref.shape
    for k in range(chunk // gb):
        yk = jnp.concatenate([y_ref[g, :, k * LANES:(k + 1) * LANES] for g in range(gb)], axis=1)
        z = jnp.dot(yk, p_ref[...], preferred_element_type=F32)
        for i8 in range(gb):
            o_ref[pl.ds(gb * k + i8, rows, stride=chunk), :] = z[:, i8 * LANES:(i8 + 1) * LANES]


def _from_group_rows(y, perm_t, groups, j_dim, chunk):
    n_rows = y.shape[1]
    gb = LANES // j_dim
    rows = _tile(n_rows, 256, 16)
    return pl.pallas_call(
        functools.partial(_from_rows_kernel, chunk=chunk),
        grid=(groups // gb, n_rows // rows),
        in_specs=[pl.BlockSpec((gb, rows, chunk * j_dim), lambda g, r: (g, r, 0)),
                  pl.BlockSpec(perm_t.shape, lambda g, r: (0, 0))],
        out_specs=pl.BlockSpec((rows * chunk, LANES), lambda g, r: (r, g)),
        out_shape=jax.ShapeDtypeStruct((n_rows * chunk, groups * j_dim), F32),
        compiler_params=_params("parallel", "parallel"),
        name="from_group_rows",
    )(y, perm_t)


def _trunk(x, mod, wts, gains, s5_ops, s5_dims, ctx):
    batch, length, d_model = x.shape
    n = batch * length
    groups, p_dim, j_dim, chunk = s5_dims
    x2 = x.reshape(n, d_model)
    rope = None if ctx is None else ctx[3]

    h = _modnorm(x2, gains['g_pre_mix'], mod, 0, 1)
    q_lora, kv_lora = gains['g_q_lat'].shape[1], gains['g_kv_lat'].shape[1]
    s5_width = groups * j_dim
    wt = wts['w_in_t']
    qn = _mm_wt(h, wt, 0, q_lora, BF16, epilogue="rmsnorm", gain=gains['g_q_lat'])
    ckv_n, k_rope = _mm_ck(h, wt, q_lora, gains['g_kv_lat'], length, rope)
    u = _mm_wt(h, wt, q_lora + kv_lora + QK_ROPE, s5_width, F32)
    gate_sig = _mm_wt(h, wt, q_lora + kv_lora + QK_ROPE + s5_width, 2 * d_model, BF16, epilogue="sigmoid")

    q = _qproj(qn, wts['uq_plain'] if rope is None else wts['uq_rope'], batch, length, rope)
    if ctx is None:
        ckv_all, kr_all, lk = ckv_n, k_rope, length
    else:
        lk = ctx[0].shape[1] + length
        ckv_all = jnp.concatenate([ctx[0], ckv_n.reshape(batch, length, -1)], axis=1).reshape(batch * lk, -1)
        kr_all = jnp.concatenate([ctx[1], k_rope.reshape(batch, length, -1)], axis=1).reshape(batch * lk, -1)
    k, v = _kvdec(ckv_all, kr_all, wts['ukv'], batch, lk)
    oa_gated = _attention(q, k, v, gate_sig.reshape(batch, length, -1)).reshape(n, -1)

    if ctx is None:
        h0 = jnp.zeros((groups, batch, 4 * p_dim), F32)
    else:
        h0 = ctx[2].transpose(3, 0, 1, 2, 4).reshape(groups, batch, 4 * p_dim)
    perm = _slab_permutation(j_dim)
    u_rows = _to_group_rows(u, perm, groups, j_dim, chunk)
    y_rows, s5_fin = _s5(u_rows, s5_ops, h0, length // chunk)
    y = _from_group_rows(y_rows, perm.T, groups, j_dim, chunk)
    m = _glu_merge(y, wts['w_glu'], gate_sig, oa_gated)
    mixed = _mm(m, wts['w_out'], BF16)
    x1, h2 = _postmix(x2, mixed, gains['g_post_mix'], gains['g_pre_mlp'], mod)
    a1 = _mm(h2, wts['w_ff1'], BF16, epilogue="relu2")
    a2 = _mm(a1, wts['w_ff2'], BF16)
    y_out = _postmlp(x1, a2, gains['g_post_mlp'], mod).reshape(batch, length, d_model)
    s5_fin = s5_fin.reshape(groups, batch, 2, 2, p_dim).transpose(1, 2, 3, 0, 4)
    return y_out, ckv_n.reshape(batch, length, -1), k_rope.reshape(batch, length, -1), s5_fin


def kernel(x_prompt, x_sample, c, cache_ckv, cache_krope, state_s5, c_ctx, w_mod, b_mod, g_pre_mix, w_in, g_q_lat, g_kv_lat, w_uq, w_ukv, s5_lam_re, s5_lam_im, s5_log_dt, s5_b_re, s5_b_im, s5_c_re, s5_c_im, s5_d, w_glu, w_out, g_post_mix, g_pre_mlp, w_ff1, w_ff2, g_post_mlp):
    depth = w_mod.shape[0]
    d_model = x_prompt.shape[-1]
    dec_batch, dec_seq = x_sample.shape[:2]
    q_lora, kv_lora = g_q_lat.shape[1], g_kv_lat.shape[1]
    groups, p_dim, j_dim = s5_b_re.shape[2:]
    chunk = min(S5_CHUNK, x_prompt.shape[1] // 8)
    rope = _axial_rope(dec_seq)

    cond = jnp.concatenate([c_ctx[None, :], c], axis=0)
    cond = jnp.pad(cond, ((0, -cond.shape[0] % 8), (0, 0)))

    xp, xs = x_prompt, x_sample
    new_ckv, new_krope, new_s5 = [], [], []
    for l in range(depth):
        mods = _adaln(cond, w_mod[l], b_mod[l][None, :])
        mod_p = mods[0:1].reshape(1, -1, d_model)
        mod_s = mods[1:1 + dec_batch].reshape(dec_batch, -1, d_model)
        wts = _prepare_weights(w_in[l], w_uq[l], w_ukv[l], w_glu[l], w_out[l], w_ff1[l], w_ff2[l],
                               q_lora, kv_lora)
        gains = dict(g_pre_mix=g_pre_mix[l][None], g_q_lat=g_q_lat[l][None], g_kv_lat=g_kv_lat[l][None],
                     g_post_mix=g_post_mix[l][None], g_pre_mlp=g_pre_mlp[l][None], g_post_mlp=g_post_mlp[l][None])
        max_chunks = max(x_prompt.shape[1], dec_seq) // chunk
        s5_ops = _s5_operators(s5_lam_re[l], s5_lam_im[l], s5_log_dt[l], s5_b_re[l], s5_b_im[l],
                               s5_c_re[l], s5_c_im[l], s5_d[l], chunk, max_chunks)
        s5_dims = (groups, p_dim, j_dim, chunk)

        xp, ckv_n, k_rope_p, s5_last = _trunk(xp, mod_p, wts, gains, s5_ops, s5_dims, None)
        new_ckv.append(ckv_n)
        new_krope.append(k_rope_p)
        new_s5.append(s5_last)
        ctx = (cache_ckv[:, l], cache_krope[:, l], state_s5[:, l], rope)
        xs, _, _, _ = _trunk(xs, mod_s, wts, gains, s5_ops, s5_dims, ctx)
    return (xp, xs, jnp.stack(new_ckv, axis=1), jnp.stack(new_krope, axis=1),
            jnp.stack(new_s5, axis=1).astype(x_prompt.dtype))
```

```python
import functools
import math

import jax
import jax.numpy as jnp
import numpy as np
from jax import lax
from jax.experimental import pallas as pl
from jax.experimental.pallas import tpu as pltpu

N_HEADS = 32
QK_NOPE = 128
QK_ROPE = 64
V_HEAD = 128
GRID_W = 64
ROPE_THETA = 10000.0
EPS = 1e-6
S5_CHUNK = 32
LANES = 128
VMEM_LIMIT_BYTES = 48 * 1024 * 1024

F32 = jnp.float32
BF16 = jnp.bfloat16


VMEM_LIMIT_BIG_BYTES = 56 * 1024 * 1024


def _params(*semantics, vmem=VMEM_LIMIT_BYTES):
    return pltpu.CompilerParams(dimension_semantics=semantics, vmem_limit_bytes=vmem)


def _tile(n, target, align=8):
    for d in range(min(n, target), 0, -1):
        if n % d == 0 and d % align == 0:
            return d
    return n


def _sigmoid(x):
    return 0.5 * jnp.tanh(0.5 * x) + 0.5


def _adaln_kernel(c_ref, w_ref, b_ref, o_ref):
    c = c_ref[...]
    s = (c * _sigmoid(c)).astype(BF16)
    o_ref[...] = jnp.dot(s, w_ref[...].astype(BF16), preferred_element_type=F32) + b_ref[...]


def _adaln(cond, w_mod, b_mod):
    rows, d = cond.shape
    n = w_mod.shape[1]
    tn = _tile(n, 512, 128)
    return pl.pallas_call(
        _adaln_kernel,
        grid=(n // tn,),
        in_specs=[pl.BlockSpec((rows, d), lambda j: (0, 0)),
                  pl.BlockSpec((d, tn), lambda j: (0, j)),
                  pl.BlockSpec((1, tn), lambda j: (0, j))],
        out_specs=pl.BlockSpec((rows, tn), lambda j: (0, j)),
        out_shape=jax.ShapeDtypeStruct((rows, n), F32),
        compiler_params=_params("parallel"),
        name="adaln",
    )(cond, w_mod, b_mod)


def _rms(x, g):
    return x * lax.rsqrt(jnp.mean(x * x, axis=-1, keepdims=True) + EPS) * g


def _modnorm_kernel(x_ref, g_ref, mod_ref, o_ref, *, shift_idx, scale_idx):
    mod = mod_ref[0]
    y = _rms(x_ref[...], g_ref[...])
    o_ref[...] = (y * (1.0 + mod[scale_idx:scale_idx + 1]) + mod[shift_idx:shift_idx + 1]).astype(o_ref.dtype)


def _mod_spec(mod, n_rows, tm):
    rows_per_mod = n_rows // mod.shape[0]
    return pl.BlockSpec((1,) + mod.shape[1:], lambda i: ((i * tm) // rows_per_mod, 0, 0))


def _modnorm(x, g, mod, shift_idx, scale_idx):
    n, d = x.shape
    tm = _tile(n // mod.shape[0], 256)
    return pl.pallas_call(
        functools.partial(_modnorm_kernel, shift_idx=shift_idx, scale_idx=scale_idx),
        grid=(n // tm,),
        in_specs=[pl.BlockSpec((tm, d), lambda i: (i, 0)),
                  pl.BlockSpec((1, d), lambda i: (0, 0)),
                  _mod_spec(mod, n, tm)],
        out_specs=pl.BlockSpec((tm, d), lambda i: (i, 0)),
        out_shape=jax.ShapeDtypeStruct((n, d), BF16),
        compiler_params=_params("parallel"),
        name="modnorm",
    )(x, g, mod)


def _epilogue(r, kind):
    if kind == "relu2":
        return jnp.square(jnp.maximum(r, 0.0))
    if kind == "sigmoid":
        return _sigmoid(r)
    assert kind is None
    return r


def _bf16_panel(w_ref, scratch_ref):
    if scratch_ref is None:
        return w_ref[...]

    @pl.when(pl.program_id(1) == 0)
    def _():
        scratch_ref[...] = w_ref[...].astype(BF16)

    return scratch_ref[...]


def _mm_full_k_kernel(a_ref, w_ref, o_ref, wb_ref=None, *, epilogue):
    w = _bf16_panel(w_ref, wb_ref)
    r = jnp.dot(a_ref[...], w, preferred_element_type=F32)
    o_ref[...] = _epilogue(r, epilogue).astype(o_ref.dtype)


def _dot_nt(a, wt):
    return lax.dot_general(a, wt, (((1,), (1,)), ((), ())), preferred_element_type=F32)


def _mm_wt_kernel(a_ref, wt_ref, *rest, epilogue):
    r = _dot_nt(a_ref[...], wt_ref[...])
    if epilogue == "rmsnorm":
        g_ref, o_ref = rest
        r = _rms(r, g_ref[...])
    else:
        (o_ref,) = rest
        r = _epilogue(r, epilogue)
    o_ref[...] = r.astype(o_ref.dtype)


def _mm_wt(a, wt, row0, n, out_dtype, epilogue=None, gain=None):
    m, kdim = a.shape
    tm = _tile(m, 512)
    tn = n if epilogue == "rmsnorm" else _tile(n, 1024, 128)
    in_specs = [pl.BlockSpec((tm, kdim), lambda j, i: (i, 0)),
                pl.BlockSpec((pl.Element(tn), pl.Element(kdim)),
                             lambda j, i: (pl.multiple_of(row0 + j * tn, 16), 0))]
    args = [a, wt]
    if epilogue == "rmsnorm":
        in_specs.append(pl.BlockSpec((1, n), lambda j, i: (0, 0)))
        args.append(gain)
    return pl.pallas_call(
        functools.partial(_mm_wt_kernel, epilogue=epilogue),
        grid=(n // tn, m // tm),
        in_specs=in_specs,
        out_specs=pl.BlockSpec((tm, tn), lambda j, i: (i, j)),
        out_shape=jax.ShapeDtypeStruct((m, n), out_dtype),
        compiler_params=_params("parallel", "parallel"),
        name="matmul_wt",
    )(*args)


def _mm_ck_kernel(a_ref, wt_ref, g_ref, *rest, kv_lora, use_rope):
    if use_rope:
        cos_ref, sin_ref, ckv_ref, kr_ref, wb_ref = rest
    else:
        ckv_ref, kr_ref, wb_ref = rest
    half = QK_ROPE // 2

    @pl.when(pl.program_id(0) == 0)
    def _():
        wb_ref[:kv_lora + QK_ROPE, :] = wt_ref[...]
        wb_ref[kv_lora + QK_ROPE:kv_lora + QK_ROPE + half, :] = -wt_ref[kv_lora + half:, :]
        wb_ref[kv_lora + QK_ROPE + half:, :] = wt_ref[kv_lora:kv_lora + half, :]

    acc = _dot_nt(a_ref[...], wb_ref[...])
    ckv_ref[...] = _rms(acc[:, :kv_lora], g_ref[...])
    k_rope = acc[:, kv_lora:kv_lora + QK_ROPE]
    if use_rope:
        k_rope = k_rope * cos_ref[...] + acc[:, kv_lora + QK_ROPE:] * sin_ref[...]
    kr_ref[...] = k_rope


def _mm_ck(a, wt, row0, g_kv, length, rope_tables):
    m, kdim = a.shape
    kv_lora = g_kv.shape[1]
    use_rope = rope_tables is not None
    tm = _tile(length, 512) if use_rope else _tile(m, 512)
    rows = kv_lora + QK_ROPE
    in_specs = [pl.BlockSpec((tm, kdim), lambda i: (i, 0)),
                pl.BlockSpec((pl.Element(rows), pl.Element(kdim)), lambda i: (row0, 0)),
                pl.BlockSpec((1, kv_lora), lambda i: (0, 0))]
    args = [a, wt, g_kv]
    if use_rope:
        per_batch = length // tm
        spec = pl.BlockSpec((tm, QK_ROPE), lambda i: (i % per_batch, 0))
        in_specs += [spec, spec]
        args += list(rope_tables)
    return pl.pallas_call(
        functools.partial(_mm_ck_kernel, kv_lora=kv_lora, use_rope=use_rope),
        grid=(m // tm,),
        in_specs=in_specs,
        out_specs=[pl.BlockSpec((tm, kv_lora), lambda i: (i, 0)),
                   pl.BlockSpec((tm, QK_ROPE), lambda i: (i, 0))],
        out_shape=[jax.ShapeDtypeStruct((m, kv_lora), F32),
                   jax.ShapeDtypeStruct((m, QK_ROPE), F32)],
        scratch_shapes=[pltpu.VMEM((rows + QK_ROPE, kdim), BF16)],
        compiler_params=_params("arbitrary"),
        name="matmul_ck",
    )(*args)


def _mm_kernel(a_ref, w_ref, o_ref, acc_ref, *, epilogue):
    k = pl.program_id(2)
    last = pl.num_programs(2) - 1

    def partial_product():
        return jnp.dot(a_ref[...], w_ref[...], preferred_element_type=F32)

    @pl.when(k == 0)
    def _():
        acc_ref[...] = partial_product()

    @pl.when((k > 0) & (k < last))
    def _():
        acc_ref[...] += partial_product()

    @pl.when(k == last)
    def _():
        o_ref[...] = _epilogue(acc_ref[...] + partial_product(), epilogue).astype(o_ref.dtype)


MM_MAX_FULL_K = 4096


def _mm(a, w, out_dtype, epilogue=None):
    m, kdim = a.shape
    n = w.shape[1]
    if kdim <= MM_MAX_FULL_K:
        cast_w = w.dtype != BF16
        tm, tn = _tile(m, 512 if kdim > 2048 else 1024), _tile(n, 1024, 128)
        return pl.pallas_call(
            functools.partial(_mm_full_k_kernel, epilogue=epilogue),
            grid=(n // tn, m // tm),
            in_specs=[pl.BlockSpec((tm, kdim), lambda j, i: (i, 0)),
                      pl.BlockSpec((kdim, tn), lambda j, i: (0, j))],
            out_specs=pl.BlockSpec((tm, tn), lambda j, i: (i, j)),
            out_shape=jax.ShapeDtypeStruct((m, n), out_dtype),
            scratch_shapes=[pltpu.VMEM((kdim, tn), BF16)] if cast_w else [],
            compiler_params=(_params("parallel", "arbitrary", vmem=VMEM_LIMIT_BIG_BYTES) if cast_w
                             else _params("parallel", "parallel")),
            name="matmul",
        )(a, w)
    assert w.dtype == BF16 and kdim > MM_MAX_FULL_K
    tm, tn, tk = _tile(m, 1024), _tile(n, 1024, 128), _tile(kdim, MM_MAX_FULL_K, 128)
    return pl.pallas_call(
        functools.partial(_mm_kernel, epilogue=epilogue),
        grid=(m // tm, n // tn, kdim // tk),
        in_specs=[pl.BlockSpec((tm, tk), lambda i, j, k: (i, k)),
                  pl.BlockSpec((tk, tn), lambda i, j, k: (k, j))],
        out_specs=pl.BlockSpec((tm, tn), lambda i, j, k: (i, j)),
        out_shape=jax.ShapeDtypeStruct((m, n), out_dtype),
        scratch_shapes=[pltpu.VMEM((tm, tn), F32)],
        compiler_params=_params("parallel", "parallel", "arbitrary"),
        name="matmul_k",
    )(a, w)


def _glu_merge_kernel(a_ref, wv_ref, wg_ref, sb_ref, oa_ref, o_ref):
    a = a_ref[...].astype(BF16)
    value = jnp.dot(a, wv_ref[...], preferred_element_type=F32)
    gate = jnp.dot(a, wg_ref[...], preferred_element_type=F32)
    o_b = value * _sigmoid(gate)
    o_ref[...] = (oa_ref[...].astype(F32) + sb_ref[...].astype(F32) * o_b).astype(o_ref.dtype)


def _glu_merge(y, w_glu, gate_sig, oa_gated):
    m, kdim = y.shape
    d = w_glu.shape[1] // 2
    tm, tn = _tile(m, 1024), _tile(d, 512, 128)
    nj = d // tn
    return pl.pallas_call(
        _glu_merge_kernel,
        grid=(nj, m // tm),
        in_specs=[pl.BlockSpec((tm, kdim), lambda j, i: (i, 0)),
                  pl.BlockSpec((kdim, tn), lambda j, i: (0, j)),
                  pl.BlockSpec((kdim, tn), lambda j, i: (0, j + nj)),
                  pl.BlockSpec((tm, tn), lambda j, i: (i, j + nj)),
                  pl.BlockSpec((tm, tn), lambda j, i: (i, j))],
        out_specs=pl.BlockSpec((tm, tn), lambda j, i: (i, j)),
        out_shape=jax.ShapeDtypeStruct((m, d), BF16),
        compiler_params=_params("parallel", "parallel"),
        name="glu_merge",
    )(y, w_glu, w_glu, gate_sig, oa_gated)


def _qproj_kernel(a_ref, w_ref, *rest, scale, use_rope):
    if use_rope:
        cos_ref, sin_ref, o_ref = rest
    else:
        (o_ref,) = rest
    nb, hb, tl, _ = o_ref.shape
    cols = w_ref.shape[1] // hb
    acc_all = jnp.dot(a_ref[...], w_ref[...], preferred_element_type=F32)
    for h in range(hb):
        acc = acc_all[:, h * cols:(h + 1) * cols]
        o_ref[:, h, :, :QK_NOPE] = (acc[:, :QK_NOPE] * scale).reshape(nb, tl, QK_NOPE).astype(o_ref.dtype)
        rope = acc[:, QK_NOPE:QK_NOPE + QK_ROPE]
        if use_rope:
            rope = rope * cos_ref[...] + acc[:, QK_NOPE + QK_ROPE:QK_NOPE + 2 * QK_ROPE] * sin_ref[...]
        o_ref[:, h, :, QK_NOPE:] = (rope * scale).reshape(nb, tl, QK_ROPE).astype(o_ref.dtype)


HEADS_PER_STEP = 8


def _row_blocking(batch, length, target):
    if length >= target:
        return 1, _tile(length, target)
    return _tile(batch, max(1, target // length), 1), length


def _head_out_spec(nb, hb, tl, length, width):
    per_batch = length // tl
    return pl.BlockSpec((nb, hb, tl, width), lambda i, h: (i // per_batch, h, i % per_batch, 0))


def _qproj(qn, w_heads, batch, length, rope_tables):
    n, kdim = qn.shape
    heads = N_HEADS
    cols = w_heads.shape[1] // heads
    use_rope = rope_tables is not None
    nb, tl = (1, _tile(length, 1024)) if use_rope else _row_blocking(batch, length, 1024)
    tm = nb * tl
    hb = _tile(heads, HEADS_PER_STEP, 1)
    in_specs = [pl.BlockSpec((tm, kdim), lambda i, h: (i, 0)),
                pl.BlockSpec((kdim, hb * cols), lambda i, h: (0, h))]
    args = [qn, w_heads]
    if use_rope:
        per_batch = length // tl
        spec = pl.BlockSpec((tl, QK_ROPE), lambda i, h: (i % per_batch, 0))
        in_specs += [spec, spec]
        args += list(rope_tables)
    width = QK_NOPE + QK_ROPE
    return pl.pallas_call(
        functools.partial(_qproj_kernel, scale=width ** -0.5, use_rope=use_rope),
        grid=(n // tm, heads // hb),
        in_specs=in_specs,
        out_specs=_head_out_spec(nb, hb, tl, length, width),
        out_shape=jax.ShapeDtypeStruct((batch, heads, length, width), BF16),
        compiler_params=_params("parallel", "parallel"),
        name="q_proj",
    )(*args)


def _kvdec_kernel(c_ref, kr_ref, w_ref, k_ref, v_ref):
    nb, hb, tl, _ = k_ref.shape
    cols = QK_NOPE + V_HEAD
    acc_all = jnp.dot(c_ref[...].astype(BF16), w_ref[...], preferred_element_type=F32)
    k_rope = kr_ref[...].reshape(nb, tl, QK_ROPE).astype(k_ref.dtype)
    for h in range(hb):
        acc = acc_all[:, h * cols:(h + 1) * cols]
        k_ref[:, h, :, :QK_NOPE] = acc[:, :QK_NOPE].reshape(nb, tl, QK_NOPE).astype(k_ref.dtype)
        k_ref[:, h, :, QK_NOPE:] = k_rope
        v_ref[:, h] = acc[:, QK_NOPE:].reshape(nb, tl, V_HEAD).astype(v_ref.dtype)


def _kvdec(ckv, krope, w_heads, batch, length):
    n, kdim = ckv.shape
    heads, cols = N_HEADS, QK_NOPE + V_HEAD
    nb, tl = _row_blocking(batch, length, 1536)
    tm = nb * tl
    hb = _tile(heads, HEADS_PER_STEP, 1)
    return pl.pallas_call(
        _kvdec_kernel,
        grid=(n // tm, heads // hb),
        in_specs=[pl.BlockSpec((tm, kdim), lambda i, h: (i, 0)),
                  pl.BlockSpec((tm, QK_ROPE), lambda i, h: (i, 0)),
                  pl.BlockSpec((kdim, hb * cols), lambda i, h: (0, h))],
        out_specs=[_head_out_spec(nb, hb, tl, length, QK_NOPE + QK_ROPE),
                   _head_out_spec(nb, hb, tl, length, V_HEAD)],
        out_shape=[jax.ShapeDtypeStruct((batch, heads, length, QK_NOPE + QK_ROPE), BF16),
                   jax.ShapeDtypeStruct((batch, heads, length, V_HEAD), BF16)],
        compiler_params=_params("parallel", "parallel"),
        name="kv_decompress",
    )(ckv, krope, w_heads)


def _qk(q, k):
    return lax.dot_general(q, k, (((1,), (1,)), ((), ())), preferred_element_type=F32)


def _attn_kernel(q_ref, k_ref, v_ref, gate_ref, o_ref, *stat_refs, tk, nk, sub):
    hb, tq, dv = q_ref.shape[1], q_ref.shape[2], v_ref.shape[3]
    if nk > 1:
        m_ref, l_ref, acc_ref = stat_refs
    lanes = dv
    units = [(h, j, r) for h in range(hb) for j in range(nk) for r in range(tq // sub)]

    def scores(unit):
        h, j, r = unit
        return _qk(q_ref[0, h, pl.ds(r * sub, sub), :], k_ref[0, h, j * tk:(j + 1) * tk, :])

    def write_out(h, rows, acc, l):
        cols = slice(h * dv, (h + 1) * dv)
        o_ref[0, rows, cols] = (acc / l * gate_ref[0, rows, cols].astype(F32)).astype(o_ref.dtype)

    s_next = scores(units[0])
    for idx, (h, j, r) in enumerate(units):
        s = s_next
        if idx + 1 < len(units):
            s_next = scores(units[idx + 1])
        rows = pl.ds(r * sub, sub)
        keys = slice(j * tk, (j + 1) * tk)
        if j == 0:
            m = jnp.max(s, axis=-1, keepdims=True)
            p = jnp.exp(s - m)
            l = jnp.sum(p, axis=-1, keepdims=True)
            acc = jnp.dot(p.astype(BF16), v_ref[0, h, keys, :], preferred_element_type=F32)
            if nk > 1:
                m = jnp.broadcast_to(m, (sub, lanes))
                l = jnp.broadcast_to(l, (sub, lanes))
        else:
            m_old = m_ref[h, rows, :]
            m = jnp.maximum(m_old, jnp.max(s, axis=-1, keepdims=True))
            alpha = jnp.exp(m_old - m)
            p = jnp.exp(s - m[:, :1])
            l = alpha * l_ref[h, rows, :] + jnp.sum(p, axis=-1, keepdims=True)
            acc = alpha * acc_ref[h, rows, :] + jnp.dot(
                p.astype(BF16), v_ref[0, h, keys, :], preferred_element_type=F32)
        if j < nk - 1:
            m_ref[h, rows, :] = m
            l_ref[h, rows, :] = l
            acc_ref[h, rows, :] = acc
        else:
            write_out(h, rows, acc, l)


ATTN_MAX_KV_CHUNK = 2304
ATTN_STEP_ROWS = 2048


def _attention(q, k, v, gate):
    batch, heads, lq, dqk = q.shape
    lk, dv = k.shape[2], v.shape[3]
    tq = _tile(lq, 2048)
    sub = _tile(tq, 512)
    hb = _tile(heads, max(1, ATTN_STEP_ROWS // lq), 1) if tq == lq else 1
    tk = _tile(lk, ATTN_MAX_KV_CHUNK, 128) if lk > ATTN_MAX_KV_CHUNK else lk
    nk = lk // tk
    stats = [pltpu.VMEM((hb, tq, dv), F32)] * 3 if nk > 1 else []
    return pl.pallas_call(
        functools.partial(_attn_kernel, tk=tk, nk=nk, sub=sub),
        grid=(batch, heads // hb, lq // tq),
        in_specs=[pl.BlockSpec((1, hb, tq, dqk), lambda b, h, i: (b, h, i, 0)),
                  pl.BlockSpec((1, hb, lk, dqk), lambda b, h, i: (b, h, 0, 0)),
                  pl.BlockSpec((1, hb, lk, dv), lambda b, h, i: (b, h, 0, 0)),
                  pl.BlockSpec((1, tq, hb * dv), lambda b, h, i: (b, i, h))],
        out_specs=pl.BlockSpec((1, tq, hb * dv), lambda b, h, i: (b, i, h)),
        out_shape=jax.ShapeDtypeStruct((batch, lq, heads * dv), BF16),
        scratch_shapes=stats,
        compiler_params=_params("parallel", "parallel", "arbitrary"),
        name="attention",
    )(q, k, v, gate)


def _gelu_tanh(x):
    return 0.5 * x * (1.0 + jnp.tanh(math.sqrt(2.0 / math.pi) * (x + 0.044715 * (x * x * x))))


def _s5_kernel(u_ref, w_ref, g_ref, e_ref, d_ref, ar_ref, ai_ref, h0_ref, y_ref, fin_ref, *, n_chunks):
    u = u_ref[0]
    rows = u.shape[0]
    batch = rows // n_chunks
    half = g_ref.shape[2] // 2
    g = jnp.dot(u, g_ref[0], preferred_element_type=F32)
    h0 = jnp.broadcast_to(h0_ref[0][:, None, :], (batch, n_chunks, 2 * half)).reshape(rows, 2 * half)
    chunk = lax.broadcasted_iota(jnp.int32, (rows, half), 0) % n_chunks
    ar = ar_ref[0]
    ai = ai_ref[0]

    def cmul(x, step, lo):
        swapped = pltpu.roll(x, half // 2, axis=1)
        return x * ar[step:step + 1, lo:lo + half] + swapped * ai[step:step + 1, lo:lo + half]

    def scan(gd, h0d, lo, backward):
        if backward:
            z = jnp.where(chunk == n_chunks - 1, h0d, pltpu.roll(gd, rows - 1, axis=0))
        else:
            z = jnp.where(chunk == 0, h0d, pltpu.roll(gd, 1, axis=0))
        step, dist = 0, 1
        while dist < n_chunks:
            if backward:
                moved = jnp.where(chunk < n_chunks - dist, pltpu.roll(z, rows - dist, axis=0), 0.0)
            else:
                moved = jnp.where(chunk >= dist, pltpu.roll(z, dist, axis=0), 0.0)
            z = z + cmul(moved, step, lo)
            step, dist = step + 1, dist * 2
        return z

    gf, gb = g[:, :half], g[:, half:]
    pf = scan(gf, h0[:, :half], 0, False)
    pb = scan(gb, h0[:, half:], half, True)
    p = jnp.concatenate([pf, pb], axis=1).astype(BF16)
    y = (jnp.dot(u, w_ref[0], preferred_element_type=F32)
         + jnp.dot(p, e_ref[0], preferred_element_type=F32)
         + u.astype(F32) * d_ref[0])
    y_ref[0] = _gelu_tanh(y).astype(y_ref.dtype)
    def pick(x, which):
        return jnp.sum(jnp.where(chunk == which, x, 0.0).reshape(batch, n_chunks, half), axis=1)

    ff = pick(cmul(pf, 0, 0) + gf, n_chunks - 1)
    fb = pick(cmul(pb, 0, half) + gb, 0)
    fin_ref[0] = jnp.concatenate([ff, fb], axis=1)


def _s5(u_rows, ops, h0, n_chunks):
    groups, rows, width = u_rows.shape
    batch = rows // n_chunks
    wt, gt, et, dvec, ar, ai = ops
    lanes = gt.shape[2]

    def per_group(*shape):
        return pl.BlockSpec((1,) + shape, lambda g: (g,) + (0,) * len(shape))

    return pl.pallas_call(
        functools.partial(_s5_kernel, n_chunks=n_chunks),
        grid=(groups,),
        in_specs=[per_group(rows, width), per_group(width, width), per_group(width, lanes),
                  per_group(lanes, width), per_group(1, width), per_group(*ar.shape[1:]),
                  per_group(*ai.shape[1:]), per_group(batch, lanes)],
        out_specs=[per_group(rows, width), per_group(batch, lanes)],
        out_shape=[jax.ShapeDtypeStruct((groups, rows, width), BF16),
                   jax.ShapeDtypeStruct((groups, batch, lanes), F32)],
        compiler_params=_params("parallel"),
        name="s5",
    )(u_rows, wt, gt, et, dvec, ar, ai, h0)


def _s5_operators(lam_re, lam_im, log_dt, b_re, b_im, c_re, c_im, d_skip, chunk, max_chunks):
    hi = lax.Precision.HIGHEST
    groups, p_dim, j_dim = b_re.shape[1:]
    t = chunk
    lam = lax.complex(lam_re, lam_im)
    z = lam * jnp.exp(log_dt)[..., None]
    a = jnp.exp(z)
    b_bar = ((a - 1.0) / lam)[..., None] * lax.complex(b_re, b_im)
    c_mat = lax.complex(c_re, c_im)
    k_idx = jnp.arange(t + 1, dtype=F32)
    apow = jnp.exp(z[None] * k_idx[:, None, None, None])

    cb = c_mat[..., None] * b_bar[:, :, None]
    ap = jnp.moveaxis(apow[:t], 1, 0)
    kern = (jnp.einsum('dtgp,dgopi->dtgoi', ap.real, cb.real, precision=hi)
            - jnp.einsum('dtgp,dgopi->dtgoi', ap.imag, cb.imag, precision=hi))
    zeros = jnp.zeros((t - 1,) + kern.shape[2:], F32)
    k_lag = jnp.concatenate([zeros, kern[0]], axis=0) + jnp.concatenate([kern[1][::-1], zeros], axis=0)
    k_lag = k_lag.transpose(1, 3, 0, 2).reshape(groups, j_dim, (2 * t - 1) * j_dim).astype(BF16)
    w_all = jnp.stack([k_lag[:, :, (t - 1 - j) * j_dim:(2 * t - 1 - j) * j_dim] for j in range(t)], axis=1)
    w_all = w_all.reshape(groups, t * j_dim, t * j_dim)

    apow_tp = jnp.moveaxis(apow, 0, 2)
    b_t = jnp.swapaxes(b_bar, -1, -2)
    inj_f = apow_tp[0][:, t - 1::-1][:, :, None, :] * b_t[0][:, None]
    inj_b = apow_tp[1][:, :t][:, :, None, :] * b_t[1][:, None]
    g_all = jnp.concatenate([inj_f.real, inj_f.imag, inj_b.real, inj_b.imag], axis=-1)
    g_all = g_all.reshape(groups, t * j_dim, 4 * p_dim)

    apow_pt = jnp.moveaxis(apow, 0, -1)
    c_t = jnp.swapaxes(c_mat, -1, -2)
    out_f = c_t[0][:, :, None, :] * apow_pt[0][:, :, 1:t + 1][..., None]
    out_b = c_t[1][:, :, None, :] * apow_pt[1][:, :, t:0:-1][..., None]
    e_all = jnp.concatenate([out_f.real, -out_f.imag, out_b.real, -out_b.imag], axis=1)
    e_all = e_all.reshape(groups, 4 * p_dim, t * j_dim)

    d_row = jnp.tile(d_skip.reshape(groups, 1, j_dim), (1, t, 1)).reshape(groups, 1, t * j_dim)

    n_steps = max(1, (max_chunks - 1).bit_length())
    steps = (t * 2 ** jnp.arange(n_steps)).astype(F32)
    tp = jnp.exp(z[None] * steps[:, None, None, None])
    a_r = jnp.concatenate([tp[:, 0].real, tp[:, 0].real, tp[:, 1].real, tp[:, 1].real], axis=-1)
    a_i = jnp.concatenate([-tp[:, 0].imag, tp[:, 0].imag, -tp[:, 1].imag, tp[:, 1].imag], axis=-1)
    a_r = a_r.transpose(1, 0, 2)
    a_i = a_i.transpose(1, 0, 2)
    return w_all, g_all.astype(BF16), e_all.astype(BF16), d_row, a_r, a_i


def _postmix_kernel(x_ref, y_ref, gpost_ref, gpre_ref, mod_ref, x1_ref, h_ref):
    mod = mod_ref[0]
    x1 = x_ref[...] + mod[2:3] * _rms(y_ref[...].astype(F32), gpost_ref[...])
    x1_ref[...] = x1
    h_ref[...] = (_rms(x1, gpre_ref[...]) * (1.0 + mod[4:5]) + mod[3:4]).astype(h_ref.dtype)


def _postmix(x, y, g_post, g_pre, mod):
    n, d = x.shape
    tm = _tile(n // mod.shape[0], 256)
    row = pl.BlockSpec((tm, d), lambda i: (i, 0))
    gain = pl.BlockSpec((1, d), lambda i: (0, 0))
    return pl.pallas_call(
        _postmix_kernel,
        grid=(n // tm,),
        in_specs=[row, row, gain, gain, _mod_spec(mod, n, tm)],
        out_specs=[row, row],
        out_shape=[jax.ShapeDtypeStruct((n, d), F32), jax.ShapeDtypeStruct((n, d), BF16)],
        compiler_params=_params("parallel"),
        name="post_mix",
    )(x, y, g_post, g_pre, mod)


def _postmlp_kernel(x_ref, y_ref, g_ref, mod_ref, o_ref):
    o_ref[...] = x_ref[...] + mod_ref[0][5:6] * _rms(y_ref[...].astype(F32), g_ref[...])


def _postmlp(x, y, g_post, mod):
    n, d = x.shape
    tm = _tile(n // mod.shape[0], 256)
    row = pl.BlockSpec((tm, d), lambda i: (i, 0))
    return pl.pallas_call(
        _postmlp_kernel,
        grid=(n // tm,),
        in_specs=[row, row, pl.BlockSpec((1, d), lambda i: (0, 0)), _mod_spec(mod, n, tm)],
        out_specs=row,
        out_shape=jax.ShapeDtypeStruct((n, d), F32),
        compiler_params=_params("parallel"),
        name="post_mlp",
    )(x, y, g_post, mod)


def _axial_rope(n_tokens):
    rows = n_tokens // GRID_W
    row = jnp.repeat(jnp.arange(rows, dtype=F32), GRID_W)
    col = jnp.tile(jnp.arange(GRID_W, dtype=F32), rows)
    n_freq = QK_ROPE // 4
    inv_freq = ROPE_THETA ** (-jnp.arange(n_freq, dtype=F32) / n_freq)
    ang = jnp.concatenate([row[:, None] * inv_freq, col[:, None] * inv_freq], axis=-1)
    ang = jnp.concatenate([ang, ang], axis=-1)
    return jnp.cos(ang), jnp.sin(ang)


def _rotate_half_cols(w):
    half = w.shape[-1] // 2
    return jnp.concatenate([-w[..., half:], w[..., :half]], axis=-1)


def _prepare_weights(w_in, w_uq, w_ukv, w_glu, w_out, w_ff1, w_ff2, q_lora, kv_lora):
    uq = w_uq.reshape(q_lora, N_HEADS, QK_NOPE + QK_ROPE)
    uq_rope = jnp.concatenate([uq, _rotate_half_cols(uq[..., QK_NOPE:])], axis=-1).reshape(q_lora, -1)
    cast = lambda w: w.astype(BF16)
    return dict(w_in_t=cast(jnp.transpose(w_in)),
                uq_plain=cast(w_uq), uq_rope=cast(uq_rope), ukv=cast(w_ukv), w_glu=cast(w_glu),
                w_out=w_out, w_ff1=w_ff1, w_ff2=cast(w_ff2))


def _slab_permutation(j_dim):
    gb = LANES // j_dim
    src = np.arange(gb * LANES)
    i8, g8, j = src // LANES, (src % LANES) // j_dim, src % j_dim
    p = np.zeros((gb * LANES, gb * LANES), np.float32)
    p[src, g8 * LANES + i8 * j_dim + j] = 1.0
    return jnp.asarray(p, BF16)


def _to_rows_kernel(u_ref, p_ref, o_ref, *, chunk):
    gb, rows, _ = o_ref.shape
    tiles = [u_ref[pl.ds(i, rows, stride=chunk), :].astype(BF16) for i in range(chunk)]
    for k in range(chunk // gb):
        z = jnp.concatenate(tiles[gb * k:gb * (k + 1)], axis=1)
        uk = jnp.dot(z, p_ref[...], preferred_element_type=F32).astype(o_ref.dtype)
        for g in range(gb):
            o_ref[g, :, k * LANES:(k + 1) * LANES] = uk[:, g * LANES:(g + 1) * LANES]


def _to_group_rows(u, perm, groups, j_dim, chunk):
    n = u.shape[0]
    gb = LANES // j_dim
    rows = _tile(n // chunk, 256, 16)
    return pl.pallas_call(
        functools.partial(_to_rows_kernel, chunk=chunk),
        grid=(groups // gb, n // (rows * chunk)),
        in_specs=[pl.BlockSpec((rows * chunk, LANES), lambda g, r: (r, g)),
                  pl.BlockSpec(perm.shape, lambda g, r: (0, 0))],
        out_specs=pl.BlockSpec((gb, rows, chunk * j_dim), lambda g, r: (g, r, 0)),
        out_shape=jax.ShapeDtypeStruct((groups, n // chunk, chunk * j_dim), BF16),
        compiler_params=_params("parallel", "parallel"),
        name="to_group_rows",
    )(u, perm)


def _from_rows_kernel(y_ref, p_ref, o_ref, *, chunk):
    gb, rows, _ = y_ref.shape
    for k in range(chunk // gb):
        yk = jnp.concatenate([y_ref[g, :, k * LANES:(k + 1) * LANES] for g in range(gb)], axis=1)
        z = jnp.dot(yk, p_ref[...], preferred_element_type=F32)
        for i8 in range(gb):
            o_ref[pl.ds(gb * k + i8, rows, stride=chunk), :] = z[:, i8 * LANES:(i8 + 1) * LANES]


def _from_group_rows(y, perm_t, groups, j_dim, chunk):
    n_rows = y.shape[1]
    gb = LANES // j_dim
    rows = _tile(n_rows, 256, 16)
    return pl.pallas_call(
        functools.partial(_from_rows_kernel, chunk=chunk),
        grid=(groups // gb, n_rows // rows),
        in_specs=[pl.BlockSpec((gb, rows, chunk * j_dim), lambda g, r: (g, r, 0)),
                  pl.BlockSpec(perm_t.shape, lambda g, r: (0, 0))],
        out_specs=pl.BlockSpec((rows * chunk, LANES), lambda g, r: (r, g)),
        out_shape=jax.ShapeDtypeStruct((n_rows * chunk, groups * j_dim), F32),
        compiler_params=_params("parallel", "parallel"),
        name="from_group_rows",
    )(y, perm_t)


def _trunk(x, mod, wts, gains, s5_ops, s5_dims, ctx):
    batch, length, d_model = x.shape
    n = batch * length
    groups, p_dim, j_dim, chunk = s5_dims
    x2 = x.reshape(n, d_model)
    rope = None if ctx is None else ctx[3]

    h = _modnorm(x2, gains['g_pre_mix'], mod, 0, 1)
    q_lora, kv_lora = gains['g_q_lat'].shape[1], gains['g_kv_lat'].shape[1]
    s5_width = groups * j_dim
    wt = wts['w_in_t']
    qn = _mm_wt(h, wt, 0, q_lora, BF16, epilogue="rmsnorm", gain=gains['g_q_lat'])
    ckv_n, k_rope = _mm_ck(h, wt, q_lora, gains['g_kv_lat'], length, rope)
    u = _mm_wt(h, wt, q_lora + kv_lora + QK_ROPE, s5_width, F32)
    gate_sig = _mm_wt(h, wt, q_lora + kv_lora + QK_ROPE + s5_width, 2 * d_model, BF16, epilogue="sigmoid")

    q = _qproj(qn, wts['uq_plain'] if rope is None else wts['uq_rope'], batch, length, rope)
    if ctx is None:
        ckv_all, kr_all, lk = ckv_n, k_rope, length
    else:
        lk = ctx[0].shape[1] + length
        ckv_all = jnp.concatenate([ctx[0], ckv_n.reshape(batch, length, -1)], axis=1).reshape(batch * lk, -1)
        kr_all = jnp.concatenate([ctx[1], k_rope.reshape(batch, length, -1)], axis=1).reshape(batch * lk, -1)
    k, v = _kvdec(ckv_all, kr_all, wts['ukv'], batch, lk)
    oa_gated = _attention(q, k, v, gate_sig.reshape(batch, length, -1)).reshape(n, -1)

    if ctx is None:
        h0 = jnp.zeros((groups, batch, 4 * p_dim), F32)
    else:
        h0 = ctx[2].transpose(3, 0, 1, 2, 4).reshape(groups, batch, 4 * p_dim)
    perm = _slab_permutation(j_dim)
    u_rows = _to_group_rows(u, perm, groups, j_dim, chunk)
    y_rows, s5_fin = _s5(u_rows, s5_ops, h0, length // chunk)
    y = _from_group_rows(y_rows, perm.T, groups, j_dim, chunk)
    m = _glu_merge(y, wts['w_glu'], gate_sig, oa_gated)
    mixed = _mm(m, wts['w_out'], BF16)
    x1, h2 = _postmix(x2, mixed, gains['g_post_mix'], gains['g_pre_mlp'], mod)
    a1 = _mm(h2, wts['w_ff1'], BF16, epilogue="relu2")
    a2 = _mm(a1, wts['w_ff2'], BF16)
    y_out = _postmlp(x1, a2, gains['g_post_mlp'], mod).reshape(batch, length, d_model)
    s5_fin = s5_fin.reshape(groups, batch, 2, 2, p_dim).transpose(1, 2, 3, 0, 4)
    return y_out, ckv_n.reshape(batch, length, -1), k_rope.reshape(batch, length, -1), s5_fin


def kernel(x_prompt, x_sample, c, cache_ckv, cache_krope, state_s5, c_ctx, w_mod, b_mod, g_pre_mix, w_in, g_q_lat, g_kv_lat, w_uq, w_ukv, s5_lam_re, s5_lam_im, s5_log_dt, s5_b_re, s5_b_im, s5_c_re, s5_c_im, s5_d, w_glu, w_out, g_post_mix, g_pre_mlp, w_ff1, w_ff2, g_post_mlp):
    depth = w_mod.shape[0]
    d_model = x_prompt.shape[-1]
    dec_batch, dec_seq = x_sample.shape[:2]
    q_lora, kv_lora = g_q_lat.shape[1], g_kv_lat.shape[1]
    groups, p_dim, j_dim = s5_b_re.shape[2:]
    chunk = min(S5_CHUNK, x_prompt.shape[1] // 8)
    rope = _axial_rope(dec_seq)

    cond = jnp.concatenate([c_ctx[None, :], c], axis=0)
    cond = jnp.pad(cond, ((0, -cond.shape[0] % 8), (0, 0)))

    xp, xs = x_prompt, x_sample
    new_ckv, new_krope, new_s5 = [], [], []
    for l in range(depth):
        mods = _adaln(cond, w_mod[l], b_mod[l][None, :])
        mod_p = mods[0:1].reshape(1, -1, d_model)
        mod_s = mods[1:1 + dec_batch].reshape(dec_batch, -1, d_model)
        wts = _prepare_weights(w_in[l], w_uq[l], w_ukv[l], w_glu[l], w_out[l], w_ff1[l], w_ff2[l],
                               q_lora, kv_lora)
        gains = dict(g_pre_mix=g_pre_mix[l][None], g_q_lat=g_q_lat[l][None], g_kv_lat=g_kv_lat[l][None],
                     g_post_mix=g_post_mix[l][None], g_pre_mlp=g_pre_mlp[l][None], g_post_mlp=g_post_mlp[l][None])
        max_chunks = max(x_prompt.shape[1], dec_seq) // chunk
        s5_ops = _s5_operators(s5_lam_re[l], s5_lam_im[l], s5_log_dt[l], s5_b_re[l], s5_b_im[l],
                               s5_c_re[l], s5_c_im[l], s5_d[l], chunk, max_chunks)
        s5_dims = (groups, p_dim, j_dim, chunk)

        xp, ckv_n, k_rope_p, s5_last = _trunk(xp, mod_p, wts, gains, s5_ops, s5_dims, None)
        new_ckv.append(ckv_n)
        new_krope.append(k_rope_p)
        new_s5.append(s5_last)
        ctx = (cache_ckv[:, l], cache_krope[:, l], state_s5[:, l], rope)
        xs, _, _, _ = _trunk(xs, mod_s, wts, gains, s5_ops, s5_dims, ctx)
    return (xp, xs, jnp.stack(new_ckv, axis=1), jnp.stack(new_krope, axis=1),
            jnp.stack(new_s5, axis=1).astype(x_prompt.dtype))
```
